```python
import jax
import jax.numpy as jnp
from jax import lax
import numpy as np

D_MODEL = 1024
BATCH = 16
SEQ = 4096
DEPTH = 1

GRID_W = 64
CTX_LEN = 256
MIX_WIDTH = D_MODEL
HG_WIDTH = MIX_WIDTH // 2
HG_HEADS = 4
HG_KEY = HG_WIDTH // HG_HEADS
HG_VAL = HG_WIDTH // HG_HEADS
HG_CHUNK = 64
HG_SUB = 16
LOG_F_MIN = -4.0
ATT_WIDTH = MIX_WIDTH - HG_WIDTH
HEAD_DIM = 64
ATT_Q_HEADS = ATT_WIDTH // HEAD_DIM
ATT_KV_HEADS = 2
ATT_GROUP = ATT_Q_HEADS // ATT_KV_HEADS
WINDOW = 128
ATT_BLOCK = 128
ROPE_BASE = 10000.0
ROT_PAIRS = HEAD_DIM // 4
IN_COLS = 5 * HG_WIDTH + (ATT_Q_HEADS + 2 * ATT_KV_HEADS) * HEAD_DIM
N_EXPERTS = 64
TOP_K = 8
N_GROUPS = 8
TOPK_GROUPS = 4
EXPERT_FF = D_MODEL // 4
ROUTED_SCALE = 2.5
LN_EPS = 1e-5
NORM_EPS = 1e-6

kernel_name = 'hymba_hgrn2_swa_moe_dit_layer'


def layer_norm(x, w=None, b=None, eps=LN_EPS):
    x32 = x.astype(jnp.float32)
    mu = jnp.mean(x32, axis=-1, keepdims=True)
    xc = x32 - mu
    y = xc * lax.rsqrt(jnp.mean(xc * xc, axis=-1, keepdims=True) + eps)
    if w is not None:
        y = y * w.astype(jnp.float32) + b.astype(jnp.float32)
    return y.astype(x.dtype)


def modulate(x, shift, scale):
    return layer_norm(x, eps=NORM_EPS) * (1.0 + scale) + shift


def split_columns(z):
    sizes = [HG_WIDTH] * 5 + [ATT_Q_HEADS * HEAD_DIM, ATT_KV_HEADS * HEAD_DIM, ATT_KV_HEADS * HEAD_DIM]
    bounds = [int(v) for v in np.cumsum(sizes)[:-1]]
    return jnp.split(z, bounds, axis=-1)


def rope_2d_angles(n):
    rows = n // GRID_W
    row = jnp.repeat(jnp.arange(rows), GRID_W).astype(jnp.float32)
    col = jnp.tile(jnp.arange(GRID_W), rows).astype(jnp.float32)
    freqs = ROPE_BASE ** (-jnp.arange(ROT_PAIRS, dtype=jnp.float32) / ROT_PAIRS)
    return row[:, None] * freqs, col[:, None] * freqs


def rotate(x, ang):
    x1, x2 = jnp.split(x.astype(jnp.float32), 2, axis=-1)
    cos = jnp.cos(ang)[:, None, :]
    sin = jnp.sin(ang)[:, None, :]
    return jnp.concatenate([x1 * cos - x2 * sin, x1 * sin + x2 * cos], axis=-1)


def apply_rope_2d(x, ang_row, ang_col):
    half = HEAD_DIM // 2
    y = jnp.concatenate([rotate(x[..., :half], ang_row), rotate(x[..., half:], ang_col)], axis=-1)
    return y.astype(x.dtype)


def hgrn2_scan(q, k, v, logf, s0, with_output):
    bsz, nh, n, dk = q.shape
    dv = v.shape[-1]
    nc = n // HG_CHUNK
    ns = HG_CHUNK // HG_SUB

    def chunked(t):
        return jnp.moveaxis(t.reshape(bsz, nh, nc, HG_CHUNK, t.shape[-1]), 2, 0)

    pos = jnp.arange(HG_CHUNK)
    causal = pos[:, None] >= pos[None, :]
    key_ok = pos[None, :] < (jnp.arange(ns)[:, None] + 1) * HG_SUB

    def step(s, inp):
        qc, kc, vc, gc = inp
        bcum = jnp.cumsum(gc, axis=-2)
        b_last = bcum[:, :, -1:, :]
        s_new = jnp.exp(b_last[:, :, 0, :])[..., None] * s + jnp.einsum(
            'bhcd,bhce->bhde', kc * jnp.exp(b_last - bcum), vc)
        if not with_output:
            return s_new, None
        bsub = bcum.reshape(bsz, nh, ns, HG_SUB, dk)
        ref = jnp.concatenate([jnp.zeros_like(bsub[:, :, :1, 0]), bsub[:, :, :-1, -1]], axis=2)
        q_hat = qc.reshape(bsz, nh, ns, HG_SUB, dk) * jnp.exp(bsub - ref[:, :, :, None])
        e = ref[:, :, :, None, :] - bcum[:, :, None, :, :]
        e = jnp.where(key_ok[None, None, :, :, None], e, -jnp.inf)
        k_hat = kc[:, :, None] * jnp.exp(e)
        att = jnp.einsum('bhnid,bhnjd->bhnij', q_hat, k_hat).reshape(bsz, nh, HG_CHUNK, HG_CHUNK)
        att = jnp.where(causal, att, 0.0)
        o = jnp.einsum('bhij,bhje->bhie', att, vc) + jnp.einsum(
            'bhid,bhde->bhie', qc * jnp.exp(bcum), s)
        return s_new, o

    s_fin, o = lax.scan(step, s0, (chunked(q), chunked(k), chunked(v), chunked(logf)))
    if with_output:
        o = jnp.moveaxis(o, 0, 2).reshape(bsz, nh, n, dv)
    return o, s_fin


def hgrn2_group(z_q, z_ff, z_fb, z_i, lb_f, lb_b, s0_f, s0_b, with_output):
    bsz, n, _ = z_q.shape

    def heads(t):
        return t.astype(jnp.float32).reshape(bsz, n, HG_HEADS, -1).transpose(0, 2, 1, 3)

    def gates(zf, lb):
        lb32 = lb.astype(jnp.float32)
        f = lb32 + (1.0 - lb32) * jax.nn.sigmoid(zf.astype(jnp.float32))
        return heads(1.0 - f), heads(jnp.maximum(jnp.log(f), LOG_F_MIN))

    q = heads(z_q)
    v = heads(z_i)
    k_f, g_f = gates(z_ff, lb_f)
    k_b, g_b = gates(z_fb, lb_b)
    rev = lambda t: jnp.flip(t, axis=2)
    o_f, s_f = hgrn2_scan(q, k_f, v, g_f, s0_f, with_output)
    o_b, s_b = hgrn2_scan(rev(q), rev(k_b), rev(v), rev(g_b), s0_b, with_output)
    o = o_f + rev(o_b) if with_output else None
    return o, s_f, s_b


def hgrn2_readout(o, z_g, norm_w):
    bsz, nh, n, dv = o.shape
    o = o.transpose(0, 2, 1, 3)
    o = o * lax.rsqrt(jnp.mean(o * o, axis=-1, keepdims=True) + NORM_EPS)
    o = o.reshape(bsz, n, nh * dv) * norm_w.astype(jnp.float32)
    return (o * jax.nn.silu(z_g.astype(jnp.float32))).astype(z_g.dtype)


def attn_context(q, k_ctx, v_ctx, sink):
    bsz, l, _, _ = q.shape
    qg = q.reshape(bsz, l, ATT_KV_HEADS, ATT_GROUP, HEAD_DIM)
    s = jnp.einsum('blhgd,bmhd->bhglm', qg, k_ctx).astype(jnp.float32) * HEAD_DIM ** -0.5
    sink_col = jnp.broadcast_to(sink.astype(jnp.float32).reshape(ATT_KV_HEADS, ATT_GROUP)[None, :, :, None, None],
                                (bsz, ATT_KV_HEADS, ATT_GROUP, l, 1))
    p = jax.nn.softmax(jnp.concatenate([sink_col, s], axis=-1), axis=-1)[..., 1:].astype(v_ctx.dtype)
    o = jnp.einsum('bhglm,bmhd->blhgd', p, v_ctx)
    return o.reshape(bsz, l, ATT_Q_HEADS * HEAD_DIM)


def attn_latent_window(q, k, v, k_ctx, v_ctx, sink):
    bsz, n, _, _ = q.shape
    l = k_ctx.shape[1]
    nb = n // ATT_BLOCK
    pad = ((0, 0), (ATT_BLOCK, ATT_BLOCK), (0, 0), (0, 0))
    kp = jnp.pad(k, pad)
    vp = jnp.pad(v, pad)
    qi = jnp.arange(ATT_BLOCK)
    kj = jnp.arange(3 * ATT_BLOCK)
    scale = HEAD_DIM ** -0.5
    sink_col = jnp.broadcast_to(sink.astype(jnp.float32).reshape(ATT_KV_HEADS, ATT_GROUP)[None, :, :, None, None],
                                (bsz, ATT_KV_HEADS, ATT_GROUP, ATT_BLOCK, 1))

    def band(bidx):
        start = bidx * ATT_BLOCK
        qb = lax.dynamic_slice_in_dim(q, start, ATT_BLOCK, axis=1).reshape(
            bsz, ATT_BLOCK, ATT_KV_HEADS, ATT_GROUP, HEAD_DIM)
        kb = lax.dynamic_slice_in_dim(kp, start, 3 * ATT_BLOCK, axis=1)
        vb = lax.dynamic_slice_in_dim(vp, start, 3 * ATT_BLOCK, axis=1)
        qpos = start + qi
        kpos = start - ATT_BLOCK + kj
        ok = (jnp.abs(kpos[None, :] - qpos[:, None]) <= WINDOW) & (kpos >= 0)[None, :] & (kpos < n)[None, :]
        s_w = jnp.einsum('bqhgd,bkhd->bhgqk', qb, kb).astype(jnp.float32) * scale
        s_w = jnp.where(ok, s_w, -jnp.inf)
        s_c = jnp.einsum('bqhgd,bmhd->bhgqm', qb, k_ctx).astype(jnp.float32) * scale
        p = jax.nn.softmax(jnp.concatenate([sink_col, s_c, s_w], axis=-1), axis=-1).astype(v.dtype)
        o = jnp.einsum('bhgqm,bmhd->bqhgd', p[..., 1:1 + l], v_ctx) + jnp.einsum(
            'bhgqk,bkhd->bqhgd', p[..., 1 + l:], vb)
        return o.reshape(bsz, ATT_BLOCK, ATT_Q_HEADS * HEAD_DIM)

    out = lax.map(band, jnp.arange(nb))
    return jnp.moveaxis(out, 0, 1).reshape(bsz, n, ATT_Q_HEADS * HEAD_DIM)


def mixer_context(h, w_in, lb_f, lb_b, norm_w, sink, with_output):
    bsz, l, _ = h.shape
    zq, zff, zfb, zi, zg, aq, ak, av = split_columns(h @ w_in)
    k_ctx = ak.reshape(bsz, l, ATT_KV_HEADS, HEAD_DIM)
    v_ctx = av.reshape(bsz, l, ATT_KV_HEADS, HEAD_DIM)
    s0 = jnp.zeros((bsz, HG_HEADS, HG_KEY, HG_VAL), jnp.float32)
    o_h, s_f, s_b = hgrn2_group(zq, zff, zfb, zi, lb_f, lb_b, s0, s0, with_output)
    if not with_output:
        return None, k_ctx, v_ctx, s_f, s_b
    y_h = hgrn2_readout(o_h, zg, norm_w)
    y_a = attn_context(aq.reshape(bsz, l, ATT_Q_HEADS, HEAD_DIM), k_ctx, v_ctx, sink)
    return jnp.concatenate([y_h, y_a.astype(y_h.dtype)], axis=-1), k_ctx, v_ctx, s_f, s_b


def mixer_latent(h, w_in, lb_f, lb_b, norm_w, sink, k_ctx, v_ctx, s_f, s_b, ang_row, ang_col):
    bsz, n, _ = h.shape
    zq, zff, zfb, zi, zg, aq, ak, av = split_columns(h @ w_in)
    o_h, _, _ = hgrn2_group(zq, zff, zfb, zi, lb_f, lb_b, s_f, s_b, True)
    y_h = hgrn2_readout(o_h, zg, norm_w)
    q = apply_rope_2d(aq.reshape(bsz, n, ATT_Q_HEADS, HEAD_DIM), ang_row, ang_col)
    k = apply_rope_2d(ak.reshape(bsz, n, ATT_KV_HEADS, HEAD_DIM), ang_row, ang_col)
    v = av.reshape(bsz, n, ATT_KV_HEADS, HEAD_DIM)
    y_a = attn_latent_window(q, k, v, k_ctx, v_ctx, sink)
    return jnp.concatenate([y_h, y_a.astype(y_h.dtype)], axis=-1)


def swiglu(h, wg, wu, wd):
    return (jax.nn.silu(h @ wg) * (h @ wu)) @ wd


def moe(h, router_w, router_bias, w_gate, w_up, w_down, sw_gate, sw_up, sw_down):
    shape = h.shape
    ht = h.reshape(-1, shape[-1])
    t = ht.shape[0]
    scores = jax.nn.sigmoid(ht.astype(jnp.float32) @ router_w.astype(jnp.float32))
    sel = scores + router_bias.astype(jnp.float32)
    grp = sel.reshape(t, N_GROUPS, N_EXPERTS // N_GROUPS)
    grp_score = jnp.sum(lax.top_k(grp, 2)[0], axis=-1)
    _, gidx = lax.top_k(grp_score, TOPK_GROUPS)
    rows = jnp.arange(t)[:, None]
    gmask = jnp.zeros((t, N_GROUPS), bool).at[rows, gidx].set(True)
    emask = jnp.repeat(gmask, N_EXPERTS // N_GROUPS, axis=1)
    _, eidx = lax.top_k(jnp.where(emask, sel, -jnp.inf), TOP_K)
    w = jnp.take_along_axis(scores, eidx, axis=-1)
    w = w / jnp.sum(w, axis=-1, keepdims=True) * ROUTED_SCALE
    gates = jnp.zeros((t, N_EXPERTS), jnp.float32).at[rows, eidx].set(w).astype(h.dtype)
    y = swiglu(ht, sw_gate, sw_up, sw_down)
    for e in range(N_EXPERTS):
        y = y + gates[:, e:e + 1] * swiglu(ht, w_gate[e], w_up[e], w_down[e])
    return y.reshape(shape)


def setup_inputs(seed: int = 0) -> dict:
    key = jax.random.key(seed)
    ks = jax.random.split(key, 32)
    beta = (8.0 * DEPTH) ** -0.25

    def nrm(k, shape, s):
        return jax.random.normal(k, shape, jnp.float32) * s

    return {
        'x': nrm(ks[0], (BATCH, SEQ, D_MODEL), 1.0),
        'c': nrm(ks[1], (BATCH, D_MODEL), 1.0),
        'ctx': nrm(ks[2], (BATCH, CTX_LEN, D_MODEL), 1.0),
        'c_ctx': nrm(ks[3], (D_MODEL,), 1.0),
        'w_ada': nrm(ks[4], (DEPTH, D_MODEL, 6 * D_MODEL), 0.5 * D_MODEL ** -0.5),
        'b_ada': nrm(ks[5], (DEPTH, 6 * D_MODEL), 0.02),
        'w_in': nrm(ks[6], (DEPTH, D_MODEL, IN_COLS), D_MODEL ** -0.5),
        'hg_lb_fwd': nrm(ks[7], (DEPTH + 1, HG_WIDTH), 0.1),
        'hg_lb_bwd': nrm(ks[8], (DEPTH + 1, HG_WIDTH), 0.1),
        'hg_norm_w': 1.0 + nrm(ks[9], (DEPTH, HG_WIDTH), 0.02),
        'attn_sink': nrm(ks[10], (DEPTH, ATT_Q_HEADS), 0.5),
        'w_out': nrm(ks[11], (DEPTH, MIX_WIDTH, D_MODEL), beta * MIX_WIDTH ** -0.5),
        'ln1_w': 1.0 + nrm(ks[12], (DEPTH, D_MODEL), 0.02),
        'ln1_b': nrm(ks[13], (DEPTH, D_MODEL), 0.02),
        'router_w': nrm(ks[14], (DEPTH, D_MODEL, N_EXPERTS), D_MODEL ** -0.5),
        'router_bias': nrm(ks[15], (DEPTH, N_EXPERTS), 0.01),
        'exp_w_gate': nrm(ks[16], (DEPTH, N_EXPERTS, D_MODEL, EXPERT_FF), D_MODEL ** -0.5),
        'exp_w_up': nrm(ks[17], (DEPTH, N_EXPERTS, D_MODEL, EXPERT_FF), D_MODEL ** -0.5),
        'exp_w_down': nrm(ks[18], (DEPTH, N_EXPERTS, EXPERT_FF, D_MODEL), beta * EXPERT_FF ** -0.5),
        'shared_w_gate': nrm(ks[19], (DEPTH, D_MODEL, EXPERT_FF), D_MODEL ** -0.5),
        'shared_w_up': nrm(ks[20], (DEPTH, D_MODEL, EXPERT_FF), D_MODEL ** -0.5),
        'shared_w_down': nrm(ks[21], (DEPTH, EXPERT_FF, D_MODEL), beta * EXPERT_FF ** -0.5),
        'ln2_w': 1.0 + nrm(ks[22], (DEPTH, D_MODEL), 0.02),
        'ln2_b': nrm(ks[23], (DEPTH, D_MODEL), 0.02),
    }


def reference(x, c, ctx, c_ctx, w_ada, b_ada, w_in, hg_lb_fwd, hg_lb_bwd, hg_norm_w, attn_sink,
              w_out, ln1_w, ln1_b, router_w, router_bias, exp_w_gate, exp_w_up, exp_w_down,
              shared_w_gate, shared_w_up, shared_w_down, ln2_w, ln2_b):
    alpha = (2.0 * DEPTH) ** 0.25
    ang_row, ang_col = rope_2d_angles(x.shape[1])
    lb_fwd = jnp.cumsum(jax.nn.softmax(hg_lb_fwd.astype(jnp.float32), axis=0), axis=0)
    lb_bwd = jnp.cumsum(jax.nn.softmax(hg_lb_bwd.astype(jnp.float32), axis=0), axis=0)
    for layer in range(DEPTH):
        last = layer == DEPTH - 1
        mod = jax.nn.silu(c) @ w_ada[layer] + b_ada[layer]
        sh1, sc1, g1, sh2, sc2, g2 = jnp.split(mod[:, None, :], 6, axis=-1)
        mod_c = jax.nn.silu(c_ctx) @ w_ada[layer] + b_ada[layer]
        csh1, csc1, cg1, csh2, csc2, cg2 = jnp.split(mod_c, 6, axis=-1)
        y_ctx, k_ctx, v_ctx, s_f, s_b = mixer_context(
            modulate(ctx, csh1, csc1), w_in[layer], lb_fwd[layer], lb_bwd[layer],
            hg_norm_w[layer], attn_sink[layer], not last)
        y_lat = mixer_latent(modulate(x, sh1, sc1), w_in[layer], lb_fwd[layer], lb_bwd[layer],
                             hg_norm_w[layer], attn_sink[layer], k_ctx, v_ctx, s_f, s_b,
                             ang_row, ang_col)
        x = layer_norm(alpha * x + g1 * (y_lat @ w_out[layer]), ln1_w[layer], ln1_b[layer])
        ffn = moe(modulate(x, sh2, sc2), router_w[layer], router_bias[layer], exp_w_gate[layer],
                  exp_w_up[layer], exp_w_down[layer], shared_w_gate[layer], shared_w_up[layer],
                  shared_w_down[layer])
        x = layer_norm(alpha * x + g2 * ffn, ln2_w[layer], ln2_b[layer])
        if not last:
            ctx = layer_norm(alpha * ctx + cg1 * (y_ctx @ w_out[layer]), ln1_w[layer], ln1_b[layer])
            ffn_c = moe(modulate(ctx, csh2, csc2), router_w[layer], router_bias[layer],
                        exp_w_gate[layer], exp_w_up[layer], exp_w_down[layer], shared_w_gate[layer],
                        shared_w_up[layer], shared_w_down[layer])
            ctx = layer_norm(alpha * ctx + cg2 * ffn_c, ln2_w[layer], ln2_b[layer])
    return x
```

```python
import jax
import jax.numpy as jnp
import numpy as np
from jax import lax
from jax.experimental import pallas as pl
from jax.experimental.pallas import tpu as pltpu

F32 = jnp.float32
BF16 = jnp.bfloat16
HIGHEST = lax.Precision.HIGHEST

DEPTH = 1
GRID_W = 64
HG_WIDTH = 512
HG_HEADS = 4
HG_DIM = 128
HG_CHUNK = 64
HG_SUB = 16
LOG_F_MIN = -4.0
HEAD_DIM = 64
Q_HEADS = 8
KV_HEADS = 2
ATT_WIDTH = Q_HEADS * HEAD_DIM
KV_WIDTH = KV_HEADS * HEAD_DIM
BAND = 128
ROPE_BASE = 10000.0
ROT_PAIRS = HEAD_DIM // 4
N_EXPERTS = 64
TOP_K = 8
N_GROUPS = 8
GROUP_SIZE = N_EXPERTS // N_GROUPS
TOPK_GROUPS = 4
ROUTED_SCALE = 2.5
LN_EPS = 1e-5
NORM_EPS = 1e-6
ALPHA = (2.0 * DEPTH) ** 0.25

LANES = 128
V7X_VMEM_LIMIT_BYTES = 56 * 1024 * 1024


def _params(*sem):
    return pltpu.CompilerParams(dimension_semantics=sem, vmem_limit_bytes=V7X_VMEM_LIMIT_BYTES)


def _ln_rows(x, eps):
    mu = jnp.mean(x, axis=-1, keepdims=True)
    xc = x - mu
    return xc * lax.rsqrt(jnp.mean(xc * xc, axis=-1, keepdims=True) + eps)


def _silu(x):
    return x * jax.nn.sigmoid(x)


def _dot(a, b):
    return jnp.dot(a, b, preferred_element_type=F32)


def _dot_nt(a, b, precision=None):
    return lax.dot_general(a, b, (((1,), (1,)), ((), ())), precision=precision,
                           preferred_element_type=F32)


def _dot_tn(a, b):
    return lax.dot_general(a, b, (((0,), (0,)), ((), ())), preferred_element_type=F32)


def _forget_gate(z, lb):
    f = lb + (1.0 - lb) * jax.nn.sigmoid(z)
    return 1.0 - f, jnp.maximum(jnp.log(f), LOG_F_MIN)


def _ada_kernel(c_ref, w_ref, b_ref, o_ref):
    c = c_ref[...]
    o_ref[...] = jnp.dot(_silu(c), w_ref[...], precision=HIGHEST,
                         preferred_element_type=F32) + b_ref[...]


def _ada(cc, w, b):
    rows, d = cc.shape
    cols = w.shape[1]
    tn = 512
    return pl.pallas_call(
        _ada_kernel,
        grid=(cols // tn,),
        in_specs=[pl.BlockSpec((rows, d), lambda j: (0, 0)),
                  pl.BlockSpec((d, tn), lambda j: (0, j)),
                  pl.BlockSpec((1, tn), lambda j: (0, j))],
        out_specs=pl.BlockSpec((rows, tn), lambda j: (0, j)),
        out_shape=jax.ShapeDtypeStruct((rows, cols), F32),
        compiler_params=_params("arbitrary"),
        name="adaln",
    )(cc, w, b)


_C_Q, _C_FF, _C_FB, _C_I, _C_G = 0, 512, 1024, 1536, 2048
_C_AQ, _C_AQR = 2560, 3072
_C_AK, _C_AKR, _C_AKS, _C_AKSR = 3584, 3712, 3840, 3968
_C_AV, _C_AVS = 4096, 4224
_C_TOTAL = 4352


def _inproj_kernel(x_ref, sh_ref, sc_ref, w_ref, lbf_ref, lbb_ref, cos_ref, sin_ref,
                   q_ref, kf_ref, gf_ref, kb_ref, gb_ref, v_ref, zg_ref,
                   aq_ref, ak_ref, aks_ref, av_ref, avs_ref):
    h = (_ln_rows(x_ref[0], NORM_EPS) * (1.0 + sc_ref[0]) + sh_ref[0]).astype(BF16)

    def proj(lo, n):
        return _dot(h, w_ref[:, lo:lo + n])

    q_ref[0] = proj(_C_Q, HG_WIDTH).astype(BF16)
    k, g = _forget_gate(proj(_C_FF, HG_WIDTH), lbf_ref[...])
    kf_ref[0] = k.astype(BF16)
    gf_ref[0] = g
    k, g = _forget_gate(proj(_C_FB, HG_WIDTH), lbb_ref[...])
    kb_ref[0] = k.astype(BF16)
    gb_ref[0] = g
    v_ref[0] = proj(_C_I, HG_WIDTH).astype(BF16)
    zg_ref[0] = proj(_C_G, HG_WIDTH).astype(BF16)
    cos = cos_ref[...]
    sin = sin_ref[...]
    cos4 = jnp.concatenate([cos] * (ATT_WIDTH // LANES), axis=1)
    sin4 = jnp.concatenate([sin] * (ATT_WIDTH // LANES), axis=1)
    scale = HEAD_DIM ** -0.5
    aq_ref[0] = ((proj(_C_AQ, ATT_WIDTH) * cos4 + proj(_C_AQR, ATT_WIDTH) * sin4) * scale).astype(BF16)
    ak_ref[0] = (proj(_C_AK, KV_WIDTH) * cos + proj(_C_AKR, KV_WIDTH) * sin).astype(BF16)
    aks_ref[0] = (proj(_C_AKS, KV_WIDTH) * cos + proj(_C_AKSR, KV_WIDTH) * sin).astype(BF16)
    av_ref[0] = proj(_C_AV, KV_WIDTH).astype(BF16)
    avs_ref[0] = proj(_C_AVS, KV_WIDTH).astype(BF16)


def _inproj(x, sh, sc, w, lbf, lbb, cos, sin, tm):
    b, n, d = x.shape
    row = lambda bi, i: (bi, i, 0)
    per_b = lambda bi, i: (bi, 0, 0)
    const = lambda bi, i: (0, 0)
    tab = lambda bi, i: (i, 0)

    def out(width, dtype):
        return jax.ShapeDtypeStruct((b, n, width), dtype), pl.BlockSpec((1, tm, width), row)

    outs = [out(HG_WIDTH, BF16), out(HG_WIDTH, BF16), out(HG_WIDTH, F32), out(HG_WIDTH, BF16),
            out(HG_WIDTH, F32), out(HG_WIDTH, BF16), out(HG_WIDTH, BF16),
            out(ATT_WIDTH, BF16), out(KV_WIDTH, BF16), out(KV_WIDTH, BF16),
            out(KV_WIDTH, BF16), out(KV_WIDTH, BF16)]
    return pl.pallas_call(
        _inproj_kernel,
        grid=(b, n // tm),
        in_specs=[pl.BlockSpec((1, tm, d), row),
                  pl.BlockSpec((1, 1, d), per_b), pl.BlockSpec((1, 1, d), per_b),
                  pl.BlockSpec(w.shape, const),
                  pl.BlockSpec((1, HG_WIDTH), const), pl.BlockSpec((1, HG_WIDTH), const),
                  pl.BlockSpec((tm, LANES), tab), pl.BlockSpec((tm, LANES), tab)],
        out_specs=[o[1] for o in outs],
        out_shape=[o[0] for o in outs],
        compiler_params=_params("arbitrary", "arbitrary"),
        name="latent_inproj",
    )(x, sh, sc, w, lbf, lbb, cos, sin)


_X_FF, _X_FB, _X_I, _X_AK, _X_AKS, _X_AV, _X_AVS, _X_TOTAL = 0, 512, 1024, 1536, 1664, 1792, 1920, 2048


def _ctx_kernel(c_ref, sh_ref, sc_ref, w_ref, lbf_ref, lbb_ref,
                k_ref, ks_ref, v_ref, vs_ref, sf_ref, sb_ref):
    h = (_ln_rows(c_ref[0], NORM_EPS) * (1.0 + sc_ref[...]) + sh_ref[...]).astype(BF16)

    def proj(lo, n):
        return _dot(h, w_ref[:, lo:lo + n])

    k_ref[0] = proj(_X_AK, KV_WIDTH).astype(BF16)
    ks_ref[0] = proj(_X_AKS, KV_WIDTH).astype(BF16)
    v_ref[0] = proj(_X_AV, KV_WIDTH).astype(BF16)
    vs_ref[0] = proj(_X_AVS, KV_WIDTH).astype(BF16)

    kf, gf = _forget_gate(proj(_X_FF, HG_WIDTH), lbf_ref[...])
    kb, gb = _forget_gate(proj(_X_FB, HG_WIDTH), lbb_ref[...])
    vi = proj(_X_I, HG_WIDTH).astype(BF16)
    n = h.shape[0]
    r = lax.broadcasted_iota(jnp.int32, (n, n), 0)
    c = lax.broadcasted_iota(jnp.int32, (n, n), 1)
    bf = jnp.dot((c <= r).astype(F32), gf, precision=HIGHEST, preferred_element_type=F32)
    bb = jnp.dot((c >= r).astype(F32), gb, precision=HIGHEST, preferred_element_type=F32)
    kdf = (kf * jnp.exp(bf[n - 1:n] - bf)).astype(BF16)
    kdb = (kb * jnp.exp(bb[0:1] - bb)).astype(BF16)
    for hd in range(HG_HEADS):
        sl = slice(hd * HG_DIM, (hd + 1) * HG_DIM)
        sf_ref[0, hd] = _dot_tn(vi[:, sl], kdf[:, sl])
        sb_ref[0, hd] = _dot_tn(vi[:, sl], kdb[:, sl])


def _ctx(ctx, sh, sc, w, lbf, lbb):
    b, n, d = ctx.shape
    per_b = lambda bi: (bi, 0, 0)
    const = lambda bi: (0, 0)
    kv = (jax.ShapeDtypeStruct((b, n, KV_WIDTH), BF16), pl.BlockSpec((1, n, KV_WIDTH), per_b))
    st = (jax.ShapeDtypeStruct((b, HG_HEADS, HG_DIM, HG_DIM), F32),
          pl.BlockSpec((1, HG_HEADS, HG_DIM, HG_DIM), lambda bi: (bi, 0, 0, 0)))
    outs = [kv, kv, kv, kv, st, st]
    return pl.pallas_call(
        _ctx_kernel,
        grid=(b,),
        in_specs=[pl.BlockSpec((1, n, d), per_b),
                  pl.BlockSpec((1, d), const), pl.BlockSpec((1, d), const),
                  pl.BlockSpec(w.shape, const),
                  pl.BlockSpec((1, HG_WIDTH), const), pl.BlockSpec((1, HG_WIDTH), const)],
        out_specs=[o[1] for o in outs],
        out_shape=[o[0] for o in outs],
        compiler_params=_params("arbitrary"),
        name="context_side",
    )(ctx, sh, sc, w, lbf, lbb)


def _hgrn_chunk(q, k, v, g, st, tri, reverse):
    cs, us = HG_CHUNK, HG_SUB
    b = jnp.dot(tri, g, precision=HIGHEST, preferred_element_type=F32)
    last = 0 if reverse else cs - 1
    b_last = b[last:last + 1]
    qf = q.astype(F32)
    kf = k.astype(F32)
    att_rows = []
    for s in range(cs // us):
        if reverse:
            rows = slice(cs - us * (s + 1), cs - us * s)
            keys = slice(cs - us * (s + 1), cs)
            ref = b[cs - us * s:cs - us * s + 1] if s > 0 else 0.0
        else:
            rows = slice(us * s, us * (s + 1))
            keys = slice(0, us * (s + 1))
            ref = b[us * s - 1:us * s] if s > 0 else 0.0
        qh = (qf[rows] * jnp.exp(b[rows] - ref)).astype(BF16)
        kh = (kf[keys] * jnp.exp(ref - b[keys])).astype(BF16)
        pad = jnp.zeros((cs - us * (s + 1), HG_DIM), BF16)
        if pad.shape[0]:
            kh = jnp.concatenate([pad, kh] if reverse else [kh, pad], axis=0)
        att_rows.append(_dot_nt(qh, kh))
    att = jnp.concatenate(att_rows[::-1] if reverse else att_rows, axis=0)
    ri = lax.broadcasted_iota(jnp.int32, (cs, cs), 0)
    ci = lax.broadcasted_iota(jnp.int32, (cs, cs), 1)
    att = jnp.where((ci >= ri) if reverse else (ci <= ri), att, 0.0)
    o = _dot(att.astype(BF16), v) + _dot_nt((qf * jnp.exp(b)).astype(BF16), st.astype(BF16))
    kdec = (kf * jnp.exp(b_last - b)).astype(BF16)
    st_new = st * jnp.exp(b_last) + _dot_tn(v, kdec)
    return o, st_new


def _hgrn_kernel(q_ref, kf_ref, gf_ref, kb_ref, gb_ref, v_ref, zg_ref, s0f_ref, s0b_ref, nw_ref,
                 y_ref, of_ref, ob_ref):
    n = q_ref.shape[1]
    cs = HG_CHUNK
    nc = n // cs
    r = lax.broadcasted_iota(jnp.int32, (cs, cs), 0)
    c = lax.broadcasted_iota(jnp.int32, (cs, cs), 1)
    lower = (c <= r).astype(F32)
    upper = (c >= r).astype(F32)

    def body(i, carry):
        sf, sb = carry
        fwd = pl.ds(pl.multiple_of(i * cs, cs), cs)
        bwd = pl.ds(pl.multiple_of((nc - 1 - i) * cs, cs), cs)
        o, sf = _hgrn_chunk(q_ref[0, fwd, :], kf_ref[0, fwd, :], v_ref[0, fwd, :], gf_ref[0, fwd, :],
                            sf, lower, False)
        of_ref[fwd, :] = o
        o, sb = _hgrn_chunk(q_ref[0, bwd, :], kb_ref[0, bwd, :], v_ref[0, bwd, :], gb_ref[0, bwd, :],
                            sb, upper, True)
        ob_ref[bwd, :] = o
        return sf, sb

    lax.fori_loop(0, nc, body, (s0f_ref[0, 0], s0b_ref[0, 0]))

    rb = min(n, 512)

    def readout(j, carry):
        sl = pl.ds(pl.multiple_of(j * rb, rb), rb)
        o = of_ref[sl, :] + ob_ref[sl, :]
        o = o * lax.rsqrt(jnp.mean(o * o, axis=-1, keepdims=True) + NORM_EPS) * nw_ref[...]
        y_ref[0, sl, :] = (o * _silu(zg_ref[0, sl, :].astype(F32))).astype(BF16)
        return carry

    lax.fori_loop(0, n // rb, readout, 0)


def _hgrn(q, kf, gf, kb, gb, v, zg, s0f, s0b, norm_w):
    b, n, _ = q.shape
    head = lambda bi, hi: (bi, 0, hi)
    st = lambda bi, hi: (bi, hi, 0, 0)
    seq = pl.BlockSpec((1, n, HG_DIM), head)
    state = pl.BlockSpec((1, 1, HG_DIM, HG_DIM), st)
    return pl.pallas_call(
        _hgrn_kernel,
        grid=(b, HG_HEADS),
        in_specs=[seq, seq, seq, seq, seq, seq, seq, state, state,
                  pl.BlockSpec((1, HG_DIM), lambda bi, hi: (0, hi))],
        out_specs=seq,
        out_shape=jax.ShapeDtypeStruct((b, n, HG_WIDTH), BF16),
        scratch_shapes=[pltpu.VMEM((n, HG_DIM), F32), pltpu.VMEM((n, HG_DIM), F32)],
        compiler_params=_params("arbitrary", "arbitrary"),
        name="hgrn2",
    )(q, kf, gf, kb, gb, v, zg, s0f, s0b, norm_w)


def _attn_kernel(sink_ref, q_ref, kp_ref, kc_ref, kn_ref, ksp_ref, ksc_ref, ksn_ref,
                 vp_ref, vc_ref, vn_ref, vsp_ref, vsc_ref, vsn_ref,
                 xk_ref, xks_ref, xv_ref, xvs_ref, y_ref):
    i = pl.program_id(1)
    nb = pl.num_programs(1)
    low = lax.broadcasted_iota(jnp.int32, (1, LANES), 1) < HEAD_DIM
    ri = lax.broadcasted_iota(jnp.int32, (BAND, BAND), 0)
    ci = lax.broadcasted_iota(jnp.int32, (BAND, BAND), 1)
    ok_prev = (ci >= ri) & (i > 0)
    ok_next = (ci <= ri) & (i < nb - 1)
    neg = -jnp.inf

    def half(ref, keep_low):
        x = ref[0]
        return jnp.where(low if keep_low else ~low, x, jnp.zeros_like(x))

    group = Q_HEADS // KV_HEADS
    for kvh in range(KV_HEADS):
        plain_low = kvh == 0
        k_src = {True: (xk_ref, kp_ref, kc_ref, kn_ref), False: (xks_ref, ksp_ref, ksc_ref, ksn_ref)}
        v_src = {True: (xv_ref, vp_ref, vc_ref, vn_ref), False: (xvs_ref, vsp_ref, vsc_ref, vsn_ref)}
        keys = {True: [half(r, True) for r in k_src[plain_low]],
                False: [half(r, False) for r in k_src[not plain_low]]}
        vals = {True: [half(r, True) for r in v_src[plain_low]],
                False: [half(r, False) for r in v_src[not plain_low]]}
        for pair in range(group // 2):
            tile = kvh * (group // 2) + pair
            q2 = q_ref[0, :, tile * LANES:(tile + 1) * LANES]
            acc = jnp.zeros((BAND, LANES), F32)
            for sub in range(2):
                in_low = sub == 0
                kx, kp, kc, kn = keys[in_low]
                vx, vp, vc, vn = vals[in_low]
                sink = sink_ref[tile * 2 + sub]
                s_x = _dot_nt(q2, kx)
                s_p = jnp.where(ok_prev, _dot_nt(q2, kp), neg)
                s_c = _dot_nt(q2, kc)
                s_n = jnp.where(ok_next, _dot_nt(q2, kn), neg)
                m = jnp.maximum(jnp.maximum(jnp.max(s_x, axis=1, keepdims=True),
                                            jnp.max(s_p, axis=1, keepdims=True)),
                                jnp.maximum(jnp.max(s_c, axis=1, keepdims=True),
                                            jnp.max(s_n, axis=1, keepdims=True)))
                m = jnp.maximum(m, sink)
                e_x = jnp.exp(s_x - m)
                e_p = jnp.exp(s_p - m)
                e_c = jnp.exp(s_c - m)
                e_n = jnp.exp(s_n - m)
                denom = (jnp.sum(e_x, axis=1, keepdims=True) + jnp.sum(e_p, axis=1, keepdims=True)
                         + jnp.sum(e_c, axis=1, keepdims=True) + jnp.sum(e_n, axis=1, keepdims=True)
                         + jnp.exp(sink - m))
                o = (_dot(e_x.astype(BF16), vx) + _dot(e_p.astype(BF16), vp)
                     + _dot(e_c.astype(BF16), vc) + _dot(e_n.astype(BF16), vn))
                acc = acc + o / denom
            y_ref[0, :, tile * LANES:(tile + 1) * LANES] = acc.astype(BF16)


def _attn(sink, aq, ak, aks, av, avs, xk, xks, xv, xvs):
    b, n, _ = aq.shape
    nb = n // BAND
    cur = lambda bi, i: (bi, i, 0)
    prev = lambda bi, i: (bi, jnp.maximum(i - 1, 0), 0)
    nxt = lambda bi, i: (bi, jnp.minimum(i + 1, nb - 1), 0)
    per_b = lambda bi, i: (bi, 0, 0)
    kv = lambda f: pl.BlockSpec((1, BAND, KV_WIDTH), f)
    cx = pl.BlockSpec((1, xk.shape[1], KV_WIDTH), per_b)
    return pl.pallas_call(
        _attn_kernel,
        grid=(b, nb),
        in_specs=[pl.BlockSpec(memory_space=pltpu.SMEM),
                  pl.BlockSpec((1, BAND, ATT_WIDTH), cur),
                  kv(prev), kv(cur), kv(nxt), kv(prev), kv(cur), kv(nxt),
                  kv(prev), kv(cur), kv(nxt), kv(prev), kv(cur), kv(nxt),
                  cx, cx, cx, cx],
        out_specs=pl.BlockSpec((1, BAND, ATT_WIDTH), cur),
        out_shape=jax.ShapeDtypeStruct((b, n, ATT_WIDTH), BF16),
        compiler_params=_params("arbitrary", "arbitrary"),
        name="window_attn",
    )(sink, aq, ak, ak, ak, aks, aks, aks, av, av, av, avs, avs, avs, xk, xks, xv, xvs)


def _route(hf, wr_t, bias):
    tm = hf.shape[0]
    scores = jax.nn.sigmoid(_dot_nt(wr_t, hf, precision=HIGHEST))
    sel = scores + bias
    grp = sel.reshape(N_GROUPS, GROUP_SIZE, tm)
    j = lax.broadcasted_iota(jnp.int32, grp.shape, 1)
    m1 = jnp.max(grp, axis=1, keepdims=True)
    first = jnp.min(jnp.where(grp == m1, j, GROUP_SIZE), axis=1, keepdims=True)
    m2 = jnp.max(jnp.where(j == first, -jnp.inf, grp), axis=1, keepdims=True)
    gs = (m1 + m2).reshape(N_GROUPS, tm)
    gi = lax.broadcasted_iota(jnp.int32, gs.shape, 0)
    rank = jnp.zeros(gs.shape, jnp.int32)
    for g in range(N_GROUPS):
        other = gs[g:g + 1]
        rank = rank + ((other > gs) | ((other == gs) & (g < gi))).astype(jnp.int32)
    gsel = rank < TOPK_GROUPS
    emask = jnp.broadcast_to(gsel[:, None, :], grp.shape).reshape(N_EXPERTS, tm)
    cand = jnp.where(emask, sel, -jnp.inf)
    ei = lax.broadcasted_iota(jnp.int32, cand.shape, 0)
    rank = jnp.zeros(cand.shape, jnp.int32)
    for e in range(N_EXPERTS):
        other = cand[e:e + 1]
        rank = rank + ((other > cand) | ((other == cand) & (e < ei))).astype(jnp.int32)
    w = jnp.where(rank < TOP_K, scores, 0.0)
    return w / jnp.sum(w, axis=0, keepdims=True) * ROUTED_SCALE


def _outproj_kernel(x_ref, yh_ref, ya_ref, w_ref, g1_ref, sh_ref, sc_ref, lw_ref, lb_ref,
                    wr_ref, rb_ref, x1_ref, h2_ref, gates_ref):
    y = _dot(yh_ref[0], w_ref[:HG_WIDTH, :]) + _dot(ya_ref[0], w_ref[HG_WIDTH:, :])
    x1 = _ln_rows(ALPHA * x_ref[0] + g1_ref[0] * y, LN_EPS) * lw_ref[...] + lb_ref[...]
    x1_ref[0] = x1
    hf = _ln_rows(x1, NORM_EPS) * (1.0 + sc_ref[0]) + sh_ref[0]
    h2_ref[0] = hf.astype(BF16)
    gates_t = _route(hf, wr_ref[...], rb_ref[...])
    pad = jnp.zeros((LANES - N_EXPERTS, gates_t.shape[1]), F32)
    gates_ref[0] = jnp.concatenate([gates_t, pad], axis=0).T


def _outproj(x, yh, ya, w, g1, sh2, sc2, lw, lb, wr_t, rbias, tm):
    b, n, d = x.shape
    row = lambda bi, i: (bi, i, 0)
    per_b = lambda bi, i: (bi, 0, 0)
    const = lambda bi, i: (0, 0)
    return pl.pallas_call(
        _outproj_kernel,
        grid=(b, n // tm),
        in_specs=[pl.BlockSpec((1, tm, d), row),
                  pl.BlockSpec((1, tm, HG_WIDTH), row), pl.BlockSpec((1, tm, ATT_WIDTH), row),
                  pl.BlockSpec(w.shape, const),
                  pl.BlockSpec((1, 1, d), per_b), pl.BlockSpec((1, 1, d), per_b),
                  pl.BlockSpec((1, 1, d), per_b),
                  pl.BlockSpec((1, d), const), pl.BlockSpec((1, d), const),
                  pl.BlockSpec(wr_t.shape, const), pl.BlockSpec(rbias.shape, const)],
        out_specs=[pl.BlockSpec((1, tm, d), row), pl.BlockSpec((1, tm, d), row),
                   pl.BlockSpec((1, tm, LANES), row)],
        out_shape=[jax.ShapeDtypeStruct((b, n, d), F32), jax.ShapeDtypeStruct((b, n, d), BF16),
                   jax.ShapeDtypeStruct((b, n, LANES), F32)],
        compiler_params=_params("arbitrary", "arbitrary"),
        name="outproj_ln_router",
    )(x, yh, ya, w, g1, sh2, sc2, lw, lb, wr_t, rbias)


def _moe_kernel(h_ref, gates_ref, x1_ref, g2_ref, lw_ref, lb_ref, wg_ref, wu_ref, wd_ref,
                swg_ref, swu_ref, swd_ref, o_ref, acc_ref):
    e = pl.program_id(2)
    h = h_ref[0]

    @pl.when(e == 0)
    def _():
        a = _silu(_dot(h, swg_ref[...])) * _dot(h, swu_ref[...])
        acc_ref[...] = _dot(a.astype(BF16), swd_ref[...])

    onehot = (lax.broadcasted_iota(jnp.int32, (LANES, LANES), 0) == e).astype(F32)
    gate = jnp.dot(gates_ref[0], onehot, precision=HIGHEST, preferred_element_type=F32)
    ff = wg_ref.shape[2]
    gate = jnp.concatenate([gate] * (ff // LANES), axis=1)
    a = _silu(_dot(h, wg_ref[0])) * _dot(h, wu_ref[0]) * gate
    acc_ref[...] += _dot(a.astype(BF16), wd_ref[0])

    @pl.when(e == pl.num_programs(2) - 1)
    def _():
        u = ALPHA * x1_ref[0] + g2_ref[0] * acc_ref[...]
        o_ref[0] = _ln_rows(u, LN_EPS) * lw_ref[...] + lb_ref[...]


def _moe(h2, gates, x1, g2, lw, lb, wg, wu, wd, swg, swu, swd, tm):
    b, n, d = x1.shape
    ne, _, ff = wg.shape
    row = lambda bi, i, e: (bi, i, 0)
    per_b = lambda bi, i, e: (bi, 0, 0)
    const = lambda bi, i, e: (0, 0)
    exp = lambda bi, i, e: (e, 0, 0)
    return pl.pallas_call(
        _moe_kernel,
        grid=(b, n // tm, ne),
        in_specs=[pl.BlockSpec((1, tm, d), row), pl.BlockSpec((1, tm, LANES), row),
                  pl.BlockSpec((1, tm, d), row), pl.BlockSpec((1, 1, d), per_b),
                  pl.BlockSpec((1, d), const), pl.BlockSpec((1, d), const),
                  pl.BlockSpec((1, d, ff), exp), pl.BlockSpec((1, d, ff), exp),
                  pl.BlockSpec((1, ff, d), exp),
                  pl.BlockSpec((d, ff), const), pl.BlockSpec((d, ff), const),
                  pl.BlockSpec((ff, d), const)],
        out_specs=pl.BlockSpec((1, tm, d), row),
        out_shape=jax.ShapeDtypeStruct((b, n, d), F32),
        scratch_shapes=[pltpu.VMEM((tm, d), F32)],
        compiler_params=_params("arbitrary", "arbitrary", "arbitrary"),
        name="moe_dense",
    )(h2, gates, x1, g2, lw, lb, wg, wu, wd, swg, swu, swd)


def _rotate_half_columns(w):
    d, cols = w.shape
    blk = w.reshape(d, cols // (2 * ROT_PAIRS), 2, ROT_PAIRS)
    return jnp.stack([-blk[:, :, 1], blk[:, :, 0]], axis=2).reshape(d, cols)


def _swap_kv_heads(w):
    return jnp.concatenate([w[:, HEAD_DIM:], w[:, :HEAD_DIM]], axis=1)


def _split_w_in(w_in):
    bounds = np.cumsum([HG_WIDTH] * 5 + [ATT_WIDTH, KV_WIDTH])
    return jnp.split(w_in, [int(v) for v in bounds], axis=1)


def _latent_weight(w_in):
    zq, zff, zfb, zi, zg, aq, ak, av = _split_w_in(w_in)
    aks = _swap_kv_heads(ak)
    cols = [zq, zff, zfb, zi, zg, aq, _rotate_half_columns(aq), ak, _rotate_half_columns(ak),
            aks, _rotate_half_columns(aks), av, _swap_kv_heads(av)]
    return jnp.concatenate(cols, axis=1).astype(BF16)


def _context_weight(w_in):
    _, zff, zfb, zi, _, _, ak, av = _split_w_in(w_in)
    return jnp.concatenate([zff, zfb, zi, ak, _swap_kv_heads(ak), av, _swap_kv_heads(av)],
                           axis=1).astype(BF16)


def _rope_tables(n):
    pos = jnp.arange(n)
    freqs = ROPE_BASE ** (-jnp.arange(ROT_PAIRS, dtype=F32) / ROT_PAIRS)
    ang_row = (pos // GRID_W).astype(F32)[:, None] * freqs
    ang_col = (pos % GRID_W).astype(F32)[:, None] * freqs
    ang = jnp.concatenate([ang_row, ang_row, ang_col, ang_col], axis=1)
    ang = jnp.concatenate([ang] * (LANES // HEAD_DIM), axis=1)
    return jnp.cos(ang), jnp.sin(ang)


def kernel(x, c, ctx, c_ctx, w_ada, b_ada, w_in, hg_lb_fwd, hg_lb_bwd, hg_norm_w, attn_sink, w_out, ln1_w, ln1_b, router_w, router_bias, exp_w_gate, exp_w_up, exp_w_down, shared_w_gate, shared_w_up, shared_w_down, ln2_w, ln2_b):
    b, n, d = x.shape
    layer = 0
    rows = -(-(b + 1) // 8) * 8
    cc = jnp.zeros((rows, d), F32).at[:b].set(c).at[b].set(c_ctx)
    mod = _ada(cc, w_ada[layer], b_ada[layer][None, :])
    sh1, sc1, g1, sh2, sc2, g2 = [m[:, None, :] for m in jnp.split(mod[:b], 6, axis=1)]
    csh1, csc1 = mod[b:b + 1, :d], mod[b:b + 1, d:2 * d]

    lb_f = jnp.cumsum(jax.nn.softmax(hg_lb_fwd.astype(F32), axis=0), axis=0)[layer][None, :]
    lb_b = jnp.cumsum(jax.nn.softmax(hg_lb_bwd.astype(F32), axis=0), axis=0)[layer][None, :]
    cos, sin = _rope_tables(n)

    xk, xks, xv, xvs, s0f, s0b = _ctx(ctx, csh1, csc1, _context_weight(w_in[layer]), lb_f, lb_b)
    tm = min(n, 512)
    (q, kf, gf, kb, gb, v, zg, aq, ak, aks, av, avs) = _inproj(
        x, sh1, sc1, _latent_weight(w_in[layer]), lb_f, lb_b, cos, sin, tm)
    yh = _hgrn(q, kf, gf, kb, gb, v, zg, s0f, s0b, hg_norm_w[layer][None, :])
    ya = _attn(attn_sink[layer], aq, ak, aks, av, avs, xk, xks, xv, xvs)

    x1, h2, gates = _outproj(x, yh, ya, w_out[layer].astype(BF16), g1, sh2, sc2,
                             ln1_w[layer][None, :], ln1_b[layer][None, :],
                             router_w[layer].T, router_bias[layer][:, None], tm)
    return _moe(h2, gates, x1, g2, ln2_w[layer][None, :], ln2_b[layer][None, :],
                exp_w_gate[layer].astype(BF16), exp_w_up[layer].astype(BF16),
                exp_w_down[layer].astype(BF16), shared_w_gate[layer].astype(BF16),
                shared_w_up[layer].astype(BF16), shared_w_down[layer].astype(BF16), min(n, 1024))
```

```python
import jax
import jax.numpy as jnp
import numpy as np
from jax import lax
from jax.experimental import pallas as pl
from jax.experimental.pallas import tpu as pltpu

F32 = jnp.float32
BF16 = jnp.bfloat16
HIGHEST = lax.Precision.HIGHEST

DEPTH = 1
GRID_W = 64
HG_WIDTH = 512
HG_HEADS = 4
HG_DIM = 128
HG_CHUNK = 64
HG_SUB = 16
LOG_F_MIN = -4.0
HEAD_DIM = 64
Q_HEADS = 8
KV_HEADS = 2
ATT_WIDTH = Q_HEADS * HEAD_DIM
KV_WIDTH = KV_HEADS * HEAD_DIM
BAND = 128
ROPE_BASE = 10000.0
ROT_PAIRS = HEAD_DIM // 4
N_EXPERTS = 64
TOP_K = 8
N_GROUPS = 8
GROUP_SIZE = N_EXPERTS // N_GROUPS
TOPK_GROUPS = 4
ROUTED_SCALE = 2.5
LN_EPS = 1e-5
NORM_EPS = 1e-6
ALPHA = (2.0 * DEPTH) ** 0.25

LANES = 128
BF16_SUBLANES = 16
MXU_DIM = 256

SORT_TILE = MXU_DIM
SORT_UNIT = BF16_SUBLANES
EXPERT_TILE = 512
_LOCAL_WORST = SORT_TILE * TOP_K + N_EXPERTS * (SORT_UNIT - 1)
LOCAL_ROWS = -(-_LOCAL_WORST // MXU_DIM) * MXU_DIM
V7X_VMEM_LIMIT_BYTES = 56 * 1024 * 1024


def _params(*sem):
    return pltpu.CompilerParams(dimension_semantics=sem, vmem_limit_bytes=V7X_VMEM_LIMIT_BYTES)


def _ln_rows(x, eps):
    mu = jnp.mean(x, axis=-1, keepdims=True)
    xc = x - mu
    return xc * lax.rsqrt(jnp.mean(xc * xc, axis=-1, keepdims=True) + eps)


def _silu(x):
    return x * jax.nn.sigmoid(x)


def _dot(a, b):
    return jnp.dot(a, b, preferred_element_type=F32)


def _dot_nt(a, b, precision=None):
    return lax.dot_general(a, b, (((1,), (1,)), ((), ())), precision=precision,
                           preferred_element_type=F32)


def _dot_tn(a, b):
    return lax.dot_general(a, b, (((0,), (0,)), ((), ())), preferred_element_type=F32)


def _forget_gate(z, lb):
    f = lb + (1.0 - lb) * jax.nn.sigmoid(z)
    return 1.0 - f, jnp.maximum(jnp.log(f), LOG_F_MIN)


def _ada_kernel(c_ref, w_ref, b_ref, o_ref):
    c = c_ref[...]
    o_ref[...] = jnp.dot(_silu(c), w_ref[...], precision=HIGHEST,
                         preferred_element_type=F32) + b_ref[...]


def _ada(cc, w, b):
    rows, d = cc.shape
    cols = w.shape[1]
    tn = 512
    return pl.pallas_call(
        _ada_kernel,
        grid=(cols // tn,),
        in_specs=[pl.BlockSpec((rows, d), lambda j: (0, 0)),
                  pl.BlockSpec((d, tn), lambda j: (0, j)),
                  pl.BlockSpec((1, tn), lambda j: (0, j))],
        out_specs=pl.BlockSpec((rows, tn), lambda j: (0, j)),
        out_shape=jax.ShapeDtypeStruct((rows, cols), F32),
        compiler_params=_params("arbitrary"),
        name="adaln",
    )(cc, w, b)


_C_Q, _C_FF, _C_FB, _C_I, _C_G = 0, 512, 1024, 1536, 2048
_C_AQ, _C_AQR = 2560, 3072
_C_AK, _C_AKR, _C_AKS, _C_AKSR = 3584, 3712, 3840, 3968
_C_AV, _C_AVS = 4096, 4224
_C_TOTAL = 4352


def _inproj_kernel(x_ref, sh_ref, sc_ref, w_ref, lbf_ref, lbb_ref, cos_ref, sin_ref,
                   q_ref, kf_ref, gf_ref, kb_ref, gb_ref, v_ref, zg_ref,
                   aq_ref, ak_ref, aks_ref, av_ref, avs_ref):
    h = (_ln_rows(x_ref[0], NORM_EPS) * (1.0 + sc_ref[0]) + sh_ref[0]).astype(BF16)

    def proj(lo, n):
        return _dot(h, w_ref[:, lo:lo + n])

    q_ref[0] = proj(_C_Q, HG_WIDTH).astype(BF16)
    k, g = _forget_gate(proj(_C_FF, HG_WIDTH), lbf_ref[...])
    kf_ref[0] = k.astype(BF16)
    gf_ref[0] = g
    k, g = _forget_gate(proj(_C_FB, HG_WIDTH), lbb_ref[...])
    kb_ref[0] = k.astype(BF16)
    gb_ref[0] = g
    v_ref[0] = proj(_C_I, HG_WIDTH).astype(BF16)
    zg_ref[0] = proj(_C_G, HG_WIDTH).astype(BF16)
    cos = cos_ref[...]
    sin = sin_ref[...]
    cos4 = jnp.concatenate([cos] * (ATT_WIDTH // LANES), axis=1)
    sin4 = jnp.concatenate([sin] * (ATT_WIDTH // LANES), axis=1)
    scale = HEAD_DIM ** -0.5
    aq_ref[0] = ((proj(_C_AQ, ATT_WIDTH) * cos4 + proj(_C_AQR, ATT_WIDTH) * sin4) * scale).astype(BF16)
    ak_ref[0] = (proj(_C_AK, KV_WIDTH) * cos + proj(_C_AKR, KV_WIDTH) * sin).astype(BF16)
    aks_ref[0] = (proj(_C_AKS, KV_WIDTH) * cos + proj(_C_AKSR, KV_WIDTH) * sin).astype(BF16)
    av_ref[0] = proj(_C_AV, KV_WIDTH).astype(BF16)
    avs_ref[0] = proj(_C_AVS, KV_WIDTH).astype(BF16)


def _inproj(x, sh, sc, w, lbf, lbb, cos, sin, tm):
    b, n, d = x.shape
    row = lambda bi, i: (bi, i, 0)
    per_b = lambda bi, i: (bi, 0, 0)
    const = lambda bi, i: (0, 0)
    tab = lambda bi, i: (i, 0)

    def out(width, dtype):
        return jax.ShapeDtypeStruct((b, n, width), dtype), pl.BlockSpec((1, tm, width), row)

    outs = [out(HG_WIDTH, BF16), out(HG_WIDTH, BF16), out(HG_WIDTH, F32), out(HG_WIDTH, BF16),
            out(HG_WIDTH, F32), out(HG_WIDTH, BF16), out(HG_WIDTH, BF16),
            out(ATT_WIDTH, BF16), out(KV_WIDTH, BF16), out(KV_WIDTH, BF16),
            out(KV_WIDTH, BF16), out(KV_WIDTH, BF16)]
    return pl.pallas_call(
        _inproj_kernel,
        grid=(b, n // tm),
        in_specs=[pl.BlockSpec((1, tm, d), row),
                  pl.BlockSpec((1, 1, d), per_b), pl.BlockSpec((1, 1, d), per_b),
                  pl.BlockSpec(w.shape, const),
                  pl.BlockSpec((1, HG_WIDTH), const), pl.BlockSpec((1, HG_WIDTH), const),
                  pl.BlockSpec((tm, LANES), tab), pl.BlockSpec((tm, LANES), tab)],
        out_specs=[o[1] for o in outs],
        out_shape=[o[0] for o in outs],
        compiler_params=_params("arbitrary", "arbitrary"),
        name="latent_inproj",
    )(x, sh, sc, w, lbf, lbb, cos, sin)


_X_FF, _X_FB, _X_I, _X_AK, _X_AKS, _X_AV, _X_AVS, _X_TOTAL = 0, 512, 1024, 1536, 1664, 1792, 1920, 2048


def _ctx_kernel(c_ref, sh_ref, sc_ref, w_ref, lbf_ref, lbb_ref,
                k_ref, ks_ref, v_ref, vs_ref, sf_ref, sb_ref):
    h = (_ln_rows(c_ref[0], NORM_EPS) * (1.0 + sc_ref[...]) + sh_ref[...]).astype(BF16)

    def proj(lo, n):
        return _dot(h, w_ref[:, lo:lo + n])

    k_ref[0] = proj(_X_AK, KV_WIDTH).astype(BF16)
    ks_ref[0] = proj(_X_AKS, KV_WIDTH).astype(BF16)
    v_ref[0] = proj(_X_AV, KV_WIDTH).astype(BF16)
    vs_ref[0] = proj(_X_AVS, KV_WIDTH).astype(BF16)

    kf, gf = _forget_gate(proj(_X_FF, HG_WIDTH), lbf_ref[...])
    kb, gb = _forget_gate(proj(_X_FB, HG_WIDTH), lbb_ref[...])
    vi = proj(_X_I, HG_WIDTH).astype(BF16)
    n = h.shape[0]
    r = lax.broadcasted_iota(jnp.int32, (n, n), 0)
    c = lax.broadcasted_iota(jnp.int32, (n, n), 1)
    bf = jnp.dot((c <= r).astype(F32), gf, precision=HIGHEST, preferred_element_type=F32)
    bb = jnp.dot((c >= r).astype(F32), gb, precision=HIGHEST, preferred_element_type=F32)
    kdf = (kf * jnp.exp(bf[n - 1:n] - bf)).astype(BF16)
    kdb = (kb * jnp.exp(bb[0:1] - bb)).astype(BF16)
    for hd in range(HG_HEADS):
        sl = slice(hd * HG_DIM, (hd + 1) * HG_DIM)
        sf_ref[0, hd] = _dot_tn(vi[:, sl], kdf[:, sl])
        sb_ref[0, hd] = _dot_tn(vi[:, sl], kdb[:, sl])


def _ctx(ctx, sh, sc, w, lbf, lbb):
    b, n, d = ctx.shape
    per_b = lambda bi: (bi, 0, 0)
    const = lambda bi: (0, 0)
    kv = (jax.ShapeDtypeStruct((b, n, KV_WIDTH), BF16), pl.BlockSpec((1, n, KV_WIDTH), per_b))
    st = (jax.ShapeDtypeStruct((b, HG_HEADS, HG_DIM, HG_DIM), F32),
          pl.BlockSpec((1, HG_HEADS, HG_DIM, HG_DIM), lambda bi: (bi, 0, 0, 0)))
    outs = [kv, kv, kv, kv, st, st]
    return pl.pallas_call(
        _ctx_kernel,
        grid=(b,),
        in_specs=[pl.BlockSpec((1, n, d), per_b),
                  pl.BlockSpec((1, d), const), pl.BlockSpec((1, d), const),
                  pl.BlockSpec(w.shape, const),
                  pl.BlockSpec((1, HG_WIDTH), const), pl.BlockSpec((1, HG_WIDTH), const)],
        out_specs=[o[1] for o in outs],
        out_shape=[o[0] for o in outs],
        compiler_params=_params("arbitrary"),
        name="context_side",
    )(ctx, sh, sc, w, lbf, lbb)


def _hgrn_chunk(q, k, v, g, st, tri, reverse):
    cs, us = HG_CHUNK, HG_SUB
    b = jnp.dot(tri, g, precision=HIGHEST, preferred_element_type=F32)
    last = 0 if reverse else cs - 1
    b_last = b[last:last + 1]
    qf = q.astype(F32)
    kf = k.astype(F32)
    att_rows = []
    for s in range(cs // us):
        if reverse:
            rows = slice(cs - us * (s + 1), cs - us * s)
            keys = slice(cs - us * (s + 1), cs)
            ref = b[cs - us * s:cs - us * s + 1] if s > 0 else 0.0
        else:
            rows = slice(us * s, us * (s + 1))
            keys = slice(0, us * (s + 1))
            ref = b[us * s - 1:us * s] if s > 0 else 0.0
        qh = (qf[rows] * jnp.exp(b[rows] - ref)).astype(BF16)
        kh = (kf[keys] * jnp.exp(ref - b[keys])).astype(BF16)
        pad = jnp.zeros((cs - us * (s + 1), HG_DIM), BF16)
        if pad.shape[0]:
            kh = jnp.concatenate([pad, kh] if reverse else [kh, pad], axis=0)
        att_rows.append(_dot_nt(qh, kh))
    att = jnp.concatenate(att_rows[::-1] if reverse else att_rows, axis=0)
    ri = lax.broadcasted_iota(jnp.int32, (cs, cs), 0)
    ci = lax.broadcasted_iota(jnp.int32, (cs, cs), 1)
    att = jnp.where((ci >= ri) if reverse else (ci <= ri), att, 0.0)
    o = _dot(att.astype(BF16), v) + _dot_nt((qf * jnp.exp(b)).astype(BF16), st.astype(BF16))
    kdec = (kf * jnp.exp(b_last - b)).astype(BF16)
    st_new = st * jnp.exp(b_last) + _dot_tn(v, kdec)
    return o, st_new


def _hgrn_kernel(q_ref, kf_ref, gf_ref, kb_ref, gb_ref, v_ref, zg_ref, s0f_ref, s0b_ref, nw_ref,
                 y_ref, of_ref, ob_ref):
    n = q_ref.shape[1]
    cs = HG_CHUNK
    nc = n // cs
    r = lax.broadcasted_iota(jnp.int32, (cs, cs), 0)
    c = lax.broadcasted_iota(jnp.int32, (cs, cs), 1)
    lower = (c <= r).astype(F32)
    upper = (c >= r).astype(F32)

    def body(i, carry):
        sf, sb = carry
        fwd = pl.ds(pl.multiple_of(i * cs, cs), cs)
        bwd = pl.ds(pl.multiple_of((nc - 1 - i) * cs, cs), cs)
        o, sf = _hgrn_chunk(q_ref[0, fwd, :], kf_ref[0, fwd, :], v_ref[0, fwd, :], gf_ref[0, fwd, :],
                            sf, lower, False)
        of_ref[fwd, :] = o
        o, sb = _hgrn_chunk(q_ref[0, bwd, :], kb_ref[0, bwd, :], v_ref[0, bwd, :], gb_ref[0, bwd, :],
                            sb, upper, True)
        ob_ref[bwd, :] = o
        return sf, sb

    lax.fori_loop(0, nc, body, (s0f_ref[0, 0], s0b_ref[0, 0]))

    rb = min(n, 512)

    def readout(j, carry):
        sl = pl.ds(pl.multiple_of(j * rb, rb), rb)
        o = of_ref[sl, :] + ob_ref[sl, :]
        o = o * lax.rsqrt(jnp.mean(o * o, axis=-1, keepdims=True) + NORM_EPS) * nw_ref[...]
        y_ref[0, sl, :] = (o * _silu(zg_ref[0, sl, :].astype(F32))).astype(BF16)
        return carry

    lax.fori_loop(0, n // rb, readout, 0)


def _hgrn(q, kf, gf, kb, gb, v, zg, s0f, s0b, norm_w):
    b, n, _ = q.shape
    head = lambda bi, hi: (bi, 0, hi)
    st = lambda bi, hi: (bi, hi, 0, 0)
    seq = pl.BlockSpec((1, n, HG_DIM), head)
    state = pl.BlockSpec((1, 1, HG_DIM, HG_DIM), st)
    return pl.pallas_call(
        _hgrn_kernel,
        grid=(b, HG_HEADS),
        in_specs=[seq, seq, seq, seq, seq, seq, seq, state, state,
                  pl.BlockSpec((1, HG_DIM), lambda bi, hi: (0, hi))],
        out_specs=seq,
        out_shape=jax.ShapeDtypeStruct((b, n, HG_WIDTH), BF16),
        scratch_shapes=[pltpu.VMEM((n, HG_DIM), F32), pltpu.VMEM((n, HG_DIM), F32)],
        compiler_params=_params("arbitrary", "arbitrary"),
        name="hgrn2",
    )(q, kf, gf, kb, gb, v, zg, s0f, s0b, norm_w)


def _attn_kernel(sink_ref, q_ref, kp_ref, kc_ref, kn_ref, ksp_ref, ksc_ref, ksn_ref,
                 vp_ref, vc_ref, vn_ref, vsp_ref, vsc_ref, vsn_ref,
                 xk_ref, xks_ref, xv_ref, xvs_ref, y_ref):
    i = pl.program_id(1)
    nb = pl.num_programs(1)
    low = lax.broadcasted_iota(jnp.int32, (1, LANES), 1) < HEAD_DIM
    ri = lax.broadcasted_iota(jnp.int32, (BAND, BAND), 0)
    ci = lax.broadcasted_iota(jnp.int32, (BAND, BAND), 1)
    ok_prev = (ci >= ri) & (i > 0)
    ok_next = (ci <= ri) & (i < nb - 1)
    neg = -jnp.inf

    def half(ref, keep_low):
        x = ref[0]
        return jnp.where(low if keep_low else ~low, x, jnp.zeros_like(x))

    group = Q_HEADS // KV_HEADS
    for kvh in range(KV_HEADS):
        plain_low = kvh == 0
        k_src = {True: (xk_ref, kp_ref, kc_ref, kn_ref), False: (xks_ref, ksp_ref, ksc_ref, ksn_ref)}
        v_src = {True: (xv_ref, vp_ref, vc_ref, vn_ref), False: (xvs_ref, vsp_ref, vsc_ref, vsn_ref)}
        keys = {True: [half(r, True) for r in k_src[plain_low]],
                False: [half(r, False) for r in k_src[not plain_low]]}
        vals = {True: [half(r, True) for r in v_src[plain_low]],
                False: [half(r, False) for r in v_src[not plain_low]]}
        for pair in range(group // 2):
            tile = kvh * (group // 2) + pair
            q2 = q_ref[0, :, tile * LANES:(tile + 1) * LANES]
            acc = jnp.zeros((BAND, LANES), F32)
            for sub in range(2):
                in_low = sub == 0
                kx, kp, kc, kn = keys[in_low]
                vx, vp, vc, vn = vals[in_low]
                sink = sink_ref[tile * 2 + sub]
                s_x = _dot_nt(q2, kx)
                s_p = jnp.where(ok_prev, _dot_nt(q2, kp), neg)
                s_c = _dot_nt(q2, kc)
                s_n = jnp.where(ok_next, _dot_nt(q2, kn), neg)
                m = jnp.maximum(jnp.maximum(jnp.max(s_x, axis=1, keepdims=True),
                                            jnp.max(s_p, axis=1, keepdims=True)),
                                jnp.maximum(jnp.max(s_c, axis=1, keepdims=True),
                                            jnp.max(s_n, axis=1, keepdims=True)))
                m = jnp.maximum(m, sink)
                e_x = jnp.exp(s_x - m)
                e_p = jnp.exp(s_p - m)
                e_c = jnp.exp(s_c - m)
                e_n = jnp.exp(s_n - m)
                denom = (jnp.sum(e_x, axis=1, keepdims=True) + jnp.sum(e_p, axis=1, keepdims=True)
                         + jnp.sum(e_c, axis=1, keepdims=True) + jnp.sum(e_n, axis=1, keepdims=True)
                         + jnp.exp(sink - m))
                o = (_dot(e_x.astype(BF16), vx) + _dot(e_p.astype(BF16), vp)
                     + _dot(e_c.astype(BF16), vc) + _dot(e_n.astype(BF16), vn))
                acc = acc + o / denom
            y_ref[0, :, tile * LANES:(tile + 1) * LANES] = acc.astype(BF16)


def _attn(sink, aq, ak, aks, av, avs, xk, xks, xv, xvs):
    b, n, _ = aq.shape
    nb = n // BAND
    cur = lambda bi, i: (bi, i, 0)
    prev = lambda bi, i: (bi, jnp.maximum(i - 1, 0), 0)
    nxt = lambda bi, i: (bi, jnp.minimum(i + 1, nb - 1), 0)
    per_b = lambda bi, i: (bi, 0, 0)
    kv = lambda f: pl.BlockSpec((1, BAND, KV_WIDTH), f)
    cx = pl.BlockSpec((1, xk.shape[1], KV_WIDTH), per_b)
    return pl.pallas_call(
        _attn_kernel,
        grid=(b, nb),
        in_specs=[pl.BlockSpec(memory_space=pltpu.SMEM),
                  pl.BlockSpec((1, BAND, ATT_WIDTH), cur),
                  kv(prev), kv(cur), kv(nxt), kv(prev), kv(cur), kv(nxt),
                  kv(prev), kv(cur), kv(nxt), kv(prev), kv(cur), kv(nxt),
                  cx, cx, cx, cx],
        out_specs=pl.BlockSpec((1, BAND, ATT_WIDTH), cur),
        out_shape=jax.ShapeDtypeStruct((b, n, ATT_WIDTH), BF16),
        compiler_params=_params("arbitrary", "arbitrary"),
        name="window_attn",
    )(sink, aq, ak, ak, ak, aks, aks, aks, av, av, av, avs, avs, avs, xk, xks, xv, xvs)


def _route(hf, wr_t, bias):
    tm = hf.shape[0]
    scores = jax.nn.sigmoid(_dot_nt(wr_t, hf, precision=HIGHEST))
    sel = scores + bias
    grp = sel.reshape(N_GROUPS, GROUP_SIZE, tm)
    j = lax.broadcasted_iota(jnp.int32, grp.shape, 1)
    m1 = jnp.max(grp, axis=1, keepdims=True)
    first = jnp.min(jnp.where(grp == m1, j, GROUP_SIZE), axis=1, keepdims=True)
    m2 = jnp.max(jnp.where(j == first, -jnp.inf, grp), axis=1, keepdims=True)
    gs = (m1 + m2).reshape(N_GROUPS, tm)
    gi = lax.broadcasted_iota(jnp.int32, gs.shape, 0)
    rank = jnp.zeros(gs.shape, jnp.int32)
    for g in range(N_GROUPS):
        other = gs[g:g + 1]
        rank = rank + ((other > gs) | ((other == gs) & (g < gi))).astype(jnp.int32)
    gsel = rank < TOPK_GROUPS
    emask = jnp.broadcast_to(gsel[:, None, :], grp.shape).reshape(N_EXPERTS, tm)
    cand = jnp.where(emask, sel, -jnp.inf)
    ei = lax.broadcasted_iota(jnp.int32, cand.shape, 0)
    rank = jnp.zeros(cand.shape, jnp.int32)
    for e in range(N_EXPERTS):
        other = cand[e:e + 1]
        rank = rank + ((other > cand) | ((other == cand) & (e < ei))).astype(jnp.int32)
    chosen = rank < TOP_K
    w = jnp.where(chosen, scores, 0.0)
    return w / jnp.sum(w, axis=0, keepdims=True) * ROUTED_SCALE, jnp.where(chosen, 1.0, 0.0)


def _sort_rows(chosen, gates_t):
    ne, ts = chosen.shape
    sel = chosen.astype(BF16)
    chosen = chosen > 0.5
    r = lax.broadcasted_iota(jnp.int32, (ts, ts), 0)
    c = lax.broadcasted_iota(jnp.int32, (ts, ts), 1)
    seen = _dot(sel, jnp.where(r <= c, 1.0, 0.0).astype(BF16))
    total = _dot(sel, jnp.ones((ts, ts), BF16))
    padded = jnp.floor((total + (SORT_UNIT - 1)) * (1.0 / SORT_UNIT)) * SORT_UNIT
    er = lax.broadcasted_iota(jnp.int32, (ne, ne), 0)
    ec = lax.broadcasted_iota(jnp.int32, (ne, ne), 1)
    before = jnp.where(ec < er, 1.0, 0.0).astype(BF16)
    start = _dot(before, padded.astype(BF16))
    choice = _dot(before, sel)
    row = start + seen - 1.0
    dest, gate = [], []
    for k in range(TOP_K):
        mk = chosen & (choice == k)
        dest.append(jnp.sum(jnp.where(mk, row, 0.0), axis=0, keepdims=True))
        gate.append(jnp.sum(jnp.where(mk, gates_t, 0.0), axis=0, keepdims=True))
    sel_pad = jnp.concatenate([sel, jnp.zeros((LANES - ne, ts), BF16)], axis=0)
    counts = _dot_nt(jnp.ones((8, ts), BF16), sel_pad)
    return (jnp.concatenate(dest, axis=0).astype(jnp.int32), jnp.concatenate(gate, axis=0), counts)


def _outproj_kernel(x_ref, yh_ref, ya_ref, w_ref, g1_ref, sh_ref, sc_ref, lw_ref, lb_ref,
                    wr_ref, rb_ref, x1_ref, h2_ref, dest_ref, gate_ref, cnt_ref):
    y = _dot(yh_ref[0], w_ref[:HG_WIDTH, :]) + _dot(ya_ref[0], w_ref[HG_WIDTH:, :])
    x1 = _ln_rows(ALPHA * x_ref[0] + g1_ref[0] * y, LN_EPS) * lw_ref[...] + lb_ref[...]
    x1_ref[0] = x1
    hf = _ln_rows(x1, NORM_EPS) * (1.0 + sc_ref[0]) + sh_ref[0]
    h2_ref[0] = hf.astype(BF16)
    gates_t, chosen = _route(hf, wr_ref[...], rb_ref[...])
    for s in range(hf.shape[0] // SORT_TILE):
        sl = slice(s * SORT_TILE, (s + 1) * SORT_TILE)
        dest_ref[s], gate_ref[s], cnt_ref[s] = _sort_rows(chosen[:, sl], gates_t[:, sl])


def _outproj(x, yh, ya, w, g1, sh2, sc2, lw, lb, wr_t, rbias, tm):
    b, n, d = x.shape
    row = lambda bi, i: (bi, i, 0)
    per_b = lambda bi, i: (bi, 0, 0)
    const = lambda bi, i: (0, 0)
    nt = b * n // SORT_TILE
    per_step = tm // SORT_TILE
    tiles = lambda bi, i: (bi * (n // tm) + i, 0, 0)
    return pl.pallas_call(
        _outproj_kernel,
        grid=(b, n // tm),
        in_specs=[pl.BlockSpec((1, tm, d), row),
                  pl.BlockSpec((1, tm, HG_WIDTH), row), pl.BlockSpec((1, tm, ATT_WIDTH), row),
                  pl.BlockSpec(w.shape, const),
                  pl.BlockSpec((1, 1, d), per_b), pl.BlockSpec((1, 1, d), per_b),
                  pl.BlockSpec((1, 1, d), per_b),
                  pl.BlockSpec((1, d), const), pl.BlockSpec((1, d), const),
                  pl.BlockSpec(wr_t.shape, const), pl.BlockSpec(rbias.shape, const)],
        out_specs=[pl.BlockSpec((1, tm, d), row), pl.BlockSpec((1, tm, d), row),
                   pl.BlockSpec((per_step, TOP_K, SORT_TILE), tiles),
                   pl.BlockSpec((per_step, TOP_K, SORT_TILE), tiles),
                   pl.BlockSpec((per_step, 8, LANES), tiles)],
        out_shape=[jax.ShapeDtypeStruct((b, n, d), F32), jax.ShapeDtypeStruct((b, n, d), BF16),
                   jax.ShapeDtypeStruct((nt, TOP_K, SORT_TILE), jnp.int32),
                   jax.ShapeDtypeStruct((nt, TOP_K, SORT_TILE), F32),
                   jax.ShapeDtypeStruct((nt, 8, LANES), F32)],
        compiler_params=_params("arbitrary", "arbitrary"),
        name="outproj_ln_router",
    )(x, yh, ya, w, g1, sh2, sc2, lw, lb, wr_t, rbias)


def _moe_layout(counts, max_tiles):
    n_pad = (counts + (SORT_UNIT - 1)) // SORT_UNIT * SORT_UNIT
    total = jnp.sum(n_pad, axis=0)
    region = (total + (EXPERT_TILE - 1)) // EXPERT_TILE * EXPERT_TILE
    base = jnp.cumsum(region) - region
    chunk_row = base[None, :] + jnp.cumsum(n_pad, axis=0) - n_pad
    tile_end = jnp.cumsum(region // EXPERT_TILE)
    n_used = tile_end[-1]
    j = jnp.minimum(jnp.arange(max_tiles, dtype=jnp.int32), n_used - 1)
    tile_expert = jnp.sum((tile_end[None, :] <= j[:, None]).astype(jnp.int32), axis=1)
    chunks = jnp.concatenate([chunk_row, n_pad], axis=1).astype(jnp.int32)[:, None, :]
    tails = jnp.concatenate([base + total, (region - total) // SORT_UNIT]).astype(jnp.int32)
    return chunks, tails, tile_expert, n_used.astype(jnp.int32)[None]


def _permutation(dest_ref, gate_ref=None):
    r = lax.broadcasted_iota(jnp.int32, (LOCAL_ROWS, SORT_TILE), 0)
    p = jnp.zeros((LOCAL_ROWS, SORT_TILE), F32)
    for k in range(TOP_K):
        hit = r == dest_ref[0, k:k + 1, :]
        p = p + jnp.where(hit, 1.0 if gate_ref is None else gate_ref[0, k:k + 1, :], 0.0)
    return p.astype(BF16)


def _unit(ref, row):
    return ref.at[pl.ds(pl.multiple_of(row, SORT_UNIT), SORT_UNIT), :]


def _for_each_unit(chunks_ref, fn):
    def per_expert(e, local):
        rows = chunks_ref[0, 0, N_EXPERTS + e]
        start = chunks_ref[0, 0, e]

        def per_unit(u, carry):
            fn(local + u * SORT_UNIT, start + u * SORT_UNIT)
            return carry

        lax.fori_loop(0, rows // SORT_UNIT, per_unit, 0)
        return local + rows

    return lax.fori_loop(0, N_EXPERTS, per_expert, 0) // SORT_UNIT


def _dispatch_kernel(chunks_ref, tails_ref, h_ref, dest_ref, xs_ref, buf_ref, zero_ref, sem):
    buf_ref[...] = _dot(_permutation(dest_ref), h_ref[...]).astype(BF16)
    units = _for_each_unit(
        chunks_ref, lambda loc, glob: pltpu.make_async_copy(_unit(buf_ref, loc), _unit(xs_ref, glob), sem).start())

    def drain(u, carry):
        pltpu.make_async_copy(_unit(buf_ref, 0), _unit(xs_ref, 0), sem).wait()
        return carry

    lax.fori_loop(0, units, drain, 0)

    @pl.when(pl.program_id(0) == pl.num_programs(0) - 1)
    def _():
        zero_ref[...] = jnp.zeros_like(zero_ref)

        def per_expert(e, total):
            n = tails_ref[N_EXPERTS + e]
            start = tails_ref[e]

            def per_unit(u, carry):
                pltpu.make_async_copy(zero_ref, _unit(xs_ref, start + u * SORT_UNIT), sem).start()
                return carry

            lax.fori_loop(0, n, per_unit, 0)
            return total + n

        total = lax.fori_loop(0, N_EXPERTS, per_expert, 0)

        def drain_zero(u, carry):
            pltpu.make_async_copy(zero_ref, _unit(xs_ref, 0), sem).wait()
            return carry

        lax.fori_loop(0, total, drain_zero, 0)


def _dispatch(chunks, tails, h2, dest, max_rows):
    t, d = h2.shape
    nt = t // SORT_TILE
    return pl.pallas_call(
        _dispatch_kernel,
        grid=(nt,),
        in_specs=[pl.BlockSpec((1, 1, 2 * N_EXPERTS), lambda i: (i, 0, 0), memory_space=pltpu.SMEM),
                  pl.BlockSpec(memory_space=pltpu.SMEM),
                  pl.BlockSpec((SORT_TILE, d), lambda i: (i, 0)),
                  pl.BlockSpec((1, TOP_K, SORT_TILE), lambda i: (i, 0, 0))],
        out_specs=pl.BlockSpec(memory_space=pl.ANY),
        out_shape=jax.ShapeDtypeStruct((max_rows, d), BF16),
        scratch_shapes=[pltpu.VMEM((LOCAL_ROWS, d), BF16), pltpu.VMEM((SORT_UNIT, d), BF16),
                        pltpu.SemaphoreType.DMA],
        compiler_params=_params("arbitrary"),
        name="moe_dispatch",
    )(chunks, tails, h2, dest)


def _expert_kernel(te_ref, nu_ref, xs_ref, wg_ref, wu_ref, wd_ref, ys_ref):
    @pl.when(pl.program_id(0) < nu_ref[0])
    def _():
        x = xs_ref[...]
        a = _silu(_dot(x, wg_ref[0])) * _dot(x, wu_ref[0])
        ys_ref[...] = _dot(a.astype(BF16), wd_ref[0]).astype(BF16)


def _experts(tile_expert, n_used, xs, wg, wu, wd):
    rows, d = xs.shape
    ff = wg.shape[2]
    row = lambda j, te, nu: (jnp.minimum(j, nu[0] - 1), 0)
    exp = lambda j, te, nu: (te[j], 0, 0)
    return pl.pallas_call(
        _expert_kernel,
        grid_spec=pltpu.PrefetchScalarGridSpec(
            num_scalar_prefetch=2,
            grid=(rows // EXPERT_TILE,),
            in_specs=[pl.BlockSpec((EXPERT_TILE, d), row),
                      pl.BlockSpec((1, d, ff), exp), pl.BlockSpec((1, d, ff), exp),
                      pl.BlockSpec((1, ff, d), exp)],
            out_specs=pl.BlockSpec((EXPERT_TILE, d), row)),
        out_shape=jax.ShapeDtypeStruct((rows, d), BF16),
        compiler_params=_params("arbitrary"),
        name="moe_experts",
    )(tile_expert, n_used, xs, wg, wu, wd)


def _combine_kernel(chunks_ref, ys_ref, dest_ref, gate_ref, h_ref, x1_ref, g2_ref, lw_ref, lb_ref,
                    swg_ref, swu_ref, swd_ref, o_ref, buf_ref, sem):
    @pl.when(pl.program_id(0) == 0)
    def _():
        buf_ref[...] = jnp.zeros_like(buf_ref)

    units = _for_each_unit(
        chunks_ref, lambda loc, glob: pltpu.make_async_copy(_unit(ys_ref, glob), _unit(buf_ref, loc), sem).start())
    h = h_ref[...]
    a = _silu(_dot(h, swg_ref[...])) * _dot(h, swu_ref[...])
    ffn = _dot(a.astype(BF16), swd_ref[...])
    p = _permutation(dest_ref, gate_ref)

    def drain(u, carry):
        pltpu.make_async_copy(_unit(ys_ref, 0), _unit(buf_ref, 0), sem).wait()
        return carry

    lax.fori_loop(0, units, drain, 0)
    ffn = ffn + _dot_tn(p, buf_ref[...])
    u = ALPHA * x1_ref[...] + g2_ref[0] * ffn
    o_ref[...] = _ln_rows(u, LN_EPS) * lw_ref[...] + lb_ref[...]


def _combine(chunks, ys, dest, gate, h2, x1, g2, lw, lb, swg, swu, swd, tiles_per_batch):
    t, d = h2.shape
    ff = swg.shape[1]
    nt = t // SORT_TILE
    tile = lambda i: (i, 0, 0)
    row = lambda i: (i, 0)
    const = lambda i: (0, 0)
    return pl.pallas_call(
        _combine_kernel,
        grid=(nt,),
        in_specs=[pl.BlockSpec((1, 1, 2 * N_EXPERTS), tile, memory_space=pltpu.SMEM),
                  pl.BlockSpec(memory_space=pl.ANY),
                  pl.BlockSpec((1, TOP_K, SORT_TILE), tile), pl.BlockSpec((1, TOP_K, SORT_TILE), tile),
                  pl.BlockSpec((SORT_TILE, d), row), pl.BlockSpec((SORT_TILE, d), row),
                  pl.BlockSpec((1, 1, d), lambda i: (i // tiles_per_batch, 0, 0)),
                  pl.BlockSpec((1, d), const), pl.BlockSpec((1, d), const),
                  pl.BlockSpec((d, ff), const), pl.BlockSpec((d, ff), const), pl.BlockSpec((ff, d), const)],
        out_specs=pl.BlockSpec((SORT_TILE, d), row),
        out_shape=jax.ShapeDtypeStruct((t, d), F32),
        scratch_shapes=[pltpu.VMEM((LOCAL_ROWS, d), BF16), pltpu.SemaphoreType.DMA],
        compiler_params=_params("arbitrary"),
        name="moe_combine",
    )(chunks, ys, dest, gate, h2, x1, g2, lw, lb, swg, swu, swd)


def _moe(h2, x1, dest, gate, counts, g2, lw, lb, wg, wu, wd, swg, swu, swd):
    b, n, d = x1.shape
    t = b * n
    nt = t // SORT_TILE
    max_rows = t * TOP_K + nt * N_EXPERTS * (SORT_UNIT - 1) + N_EXPERTS * (EXPERT_TILE - SORT_UNIT)
    max_rows = -(-max_rows // EXPERT_TILE) * EXPERT_TILE
    chunks, tails, tile_expert, n_used = _moe_layout(
        counts[:, 0, :N_EXPERTS].astype(jnp.int32), max_rows // EXPERT_TILE)
    h2 = h2.reshape(t, d)
    xs = _dispatch(chunks, tails, h2, dest, max_rows)
    ys = _experts(tile_expert, n_used, xs, wg, wu, wd)
    out = _combine(chunks, ys, dest, gate, h2, x1.reshape(t, d), g2, lw, lb, swg, swu, swd, n // SORT_TILE)
    return out.reshape(b, n, d)


def _rotate_half_columns(w):
    d, cols = w.shape
    blk = w.reshape(d, cols // (2 * ROT_PAIRS), 2, ROT_PAIRS)
    return jnp.stack([-blk[:, :, 1], blk[:, :, 0]], axis=2).reshape(d, cols)


def _swap_kv_heads(w):
    return jnp.concatenate([w[:, HEAD_DIM:], w[:, :HEAD_DIM]], axis=1)


def _split_w_in(w_in):
    bounds = np.cumsum([HG_WIDTH] * 5 + [ATT_WIDTH, KV_WIDTH])
    return jnp.split(w_in, [int(v) for v in bounds], axis=1)


def _latent_weight(w_in):
    zq, zff, zfb, zi, zg, aq, ak, av = _split_w_in(w_in)
    aks = _swap_kv_heads(ak)
    cols = [zq, zff, zfb, zi, zg, aq, _rotate_half_columns(aq), ak, _rotate_half_columns(ak),
            aks, _rotate_half_columns(aks), av, _swap_kv_heads(av)]
    return jnp.concatenate(cols, axis=1).astype(BF16)


def _context_weight(w_in):
    _, zff, zfb, zi, _, _, ak, av = _split_w_in(w_in)
    return jnp.concatenate([zff, zfb, zi, ak, _swap_kv_heads(ak), av, _swap_kv_heads(av)],
                           axis=1).astype(BF16)


def _rope_tables(n):
    pos = jnp.arange(n)
    freqs = ROPE_BASE ** (-jnp.arange(ROT_PAIRS, dtype=F32) / ROT_PAIRS)
    ang_row = (pos // GRID_W).astype(F32)[:, None] * freqs
    ang_col = (pos % GRID_W).astype(F32)[:, None] * freqs
    ang = jnp.concatenate([ang_row, ang_row, ang_col, ang_col], axis=1)
    ang = jnp.concatenate([ang] * (LANES // HEAD_DIM), axis=1)
    return jnp.cos(ang), jnp.sin(ang)


def kernel(x, c, ctx, c_ctx, w_ada, b_ada, w_in, hg_lb_fwd, hg_lb_bwd, hg_norm_w, attn_sink, w_out, ln1_w, ln1_b, router_w, router_bias, exp_w_gate, exp_w_up, exp_w_down, shared_w_gate, shared_w_up, shared_w_down, ln2_w, ln2_b):
    b, n, d = x.shape
    layer = 0
    rows = -(-(b + 1) // 8) * 8
    cc = jnp.zeros((rows, d), F32).at[:b].set(c).at[b].set(c_ctx)
    mod = _ada(cc, w_ada[layer], b_ada[layer][None, :])
    sh1, sc1, g1, sh2, sc2, g2 = [m[:, None, :] for m in jnp.split(mod[:b], 6, axis=1)]
    csh1, csc1 = mod[b:b + 1, :d], mod[b:b + 1, d:2 * d]

    lb_f = jnp.cumsum(jax.nn.softmax(hg_lb_fwd.astype(F32), axis=0), axis=0)[layer][None, :]
    lb_b = jnp.cumsum(jax.nn.softmax(hg_lb_bwd.astype(F32), axis=0), axis=0)[layer][None, :]
    cos, sin = _rope_tables(n)

    xk, xks, xv, xvs, s0f, s0b = _ctx(ctx, csh1, csc1, _context_weight(w_in[layer]), lb_f, lb_b)
    tm = min(n, 512)
    (q, kf, gf, kb, gb, v, zg, aq, ak, aks, av, avs) = _inproj(
        x, sh1, sc1, _latent_weight(w_in[layer]), lb_f, lb_b, cos, sin, tm)
    yh = _hgrn(q, kf, gf, kb, gb, v, zg, s0f, s0b, hg_norm_w[layer][None, :])
    ya = _attn(attn_sink[layer], aq, ak, aks, av, avs, xk, xks, xv, xvs)

    x1, h2, dest, gate, counts = _outproj(x, yh, ya, w_out[layer].astype(BF16), g1, sh2, sc2,
                                          ln1_w[layer][None, :], ln1_b[layer][None, :],
                                          router_w[layer].T, router_bias[layer][:, None], tm)
    return _moe(h2, x1, dest, gate, counts, g2, ln2_w[layer][None, :], ln2_b[layer][None, :],
                exp_w_gate[layer].astype(BF16), exp_w_up[layer].astype(BF16),
                exp_w_down[layer].astype(BF16), shared_w_gate[layer].astype(BF16),
                shared_w_up[layer].astype(BF16), shared_w_down[layer].astype(BF16))
```

```python
import jax
import jax.numpy as jnp
import numpy as np
from jax import lax
from jax.experimental import pallas as pl
from jax.experimental.pallas import tpu as pltpu

F32 = jnp.float32
BF16 = jnp.bfloat16
HIGHEST = lax.Precision.HIGHEST

DEPTH = 1
GRID_W = 64
HG_WIDTH = 512
HG_HEADS = 4
HG_DIM = 128
HG_CHUNK = 64
HG_SUB = 16
LOG_F_MIN = -4.0
HEAD_DIM = 64
Q_HEADS = 8
KV_HEADS = 2
ATT_WIDTH = Q_HEADS * HEAD_DIM
KV_WIDTH = KV_HEADS * HEAD_DIM
BAND = 128
ROPE_BASE = 10000.0
ROT_PAIRS = HEAD_DIM // 4
N_EXPERTS = 64
TOP_K = 8
N_GROUPS = 8
GROUP_SIZE = N_EXPERTS // N_GROUPS
TOPK_GROUPS = 4
ROUTED_SCALE = 2.5
LN_EPS = 1e-5
NORM_EPS = 1e-6
ALPHA = (2.0 * DEPTH) ** 0.25

LANES = 128
BF16_SUBLANES = 16
MXU_DIM = 256

SORT_TILE = MXU_DIM
SORT_UNIT = BF16_SUBLANES
EXPERT_TILE = 1024
_LOCAL_WORST = SORT_TILE * TOP_K + N_EXPERTS * (SORT_UNIT - 1)
LOCAL_ROWS = -(-_LOCAL_WORST // MXU_DIM) * MXU_DIM
USUAL_ROWS = SORT_TILE * TOP_K + 2 * MXU_DIM
V7X_VMEM_LIMIT_BYTES = 56 * 1024 * 1024


def _params(*sem):
    return pltpu.CompilerParams(dimension_semantics=sem, vmem_limit_bytes=V7X_VMEM_LIMIT_BYTES)


def _ln_rows(x, eps):
    mu = jnp.mean(x, axis=-1, keepdims=True)
    xc = x - mu
    return xc * lax.rsqrt(jnp.mean(xc * xc, axis=-1, keepdims=True) + eps)


def _silu(x):
    return x * jax.nn.sigmoid(x)


def _dot(a, b):
    return jnp.dot(a, b, preferred_element_type=F32)


def _dot_nt(a, b, precision=None):
    return lax.dot_general(a, b, (((1,), (1,)), ((), ())), precision=precision,
                           preferred_element_type=F32)


def _dot_tn(a, b):
    return lax.dot_general(a, b, (((0,), (0,)), ((), ())), preferred_element_type=F32)


def _forget_gate(z, lb):
    f = lb + (1.0 - lb) * jax.nn.sigmoid(z)
    return 1.0 - f, jnp.maximum(jnp.log(f), LOG_F_MIN)


def _chunk_scan(g, reverse):
    n = g.shape[0]
    pos = lax.broadcasted_iota(jnp.int32, g.shape, 0) % HG_CHUNK
    step = 1
    while step < HG_CHUNK:
        if reverse:
            g = g + jnp.where(pos < HG_CHUNK - step, pltpu.roll(g, n - step, axis=0), 0.0)
        else:
            g = g + jnp.where(pos >= step, pltpu.roll(g, step, axis=0), 0.0)
        step *= 2
    return g


def _ada_kernel(c_ref, w_ref, b_ref, o_ref):
    c = c_ref[...]
    o_ref[...] = jnp.dot(_silu(c), w_ref[...], precision=HIGHEST,
                         preferred_element_type=F32) + b_ref[...]


def _ada(cc, w, b):
    rows, d = cc.shape
    cols = w.shape[1]
    tn = 512
    return pl.pallas_call(
        _ada_kernel,
        grid=(cols // tn,),
        in_specs=[pl.BlockSpec((rows, d), lambda j: (0, 0)),
                  pl.BlockSpec((d, tn), lambda j: (0, j)),
                  pl.BlockSpec((1, tn), lambda j: (0, j))],
        out_specs=pl.BlockSpec((rows, tn), lambda j: (0, j)),
        out_shape=jax.ShapeDtypeStruct((rows, cols), F32),
        compiler_params=_params("arbitrary"),
        name="adaln",
    )(cc, w, b)


_C_Q, _C_FF, _C_FB, _C_I, _C_G = 0, 512, 1024, 1536, 2048
_C_AQ, _C_AQR = 2560, 3072
_C_AK, _C_AKR, _C_AKS, _C_AKSR = 3584, 3712, 3840, 3968
_C_AV, _C_AVS = 4096, 4224
_C_TOTAL = 4352


def _inproj_kernel(x_ref, sh_ref, sc_ref, w_ref, lbf_ref, lbb_ref, cos_ref, sin_ref,
                   q_ref, kf_ref, gf_ref, kb_ref, gb_ref, v_ref, zg_ref,
                   aq_ref, ak_ref, aks_ref, av_ref, avs_ref):
    h = (_ln_rows(x_ref[0], NORM_EPS) * (1.0 + sc_ref[0]) + sh_ref[0]).astype(BF16)

    def proj(lo, n):
        return _dot(h, w_ref[:, lo:lo + n])

    q_ref[0] = proj(_C_Q, HG_WIDTH).astype(BF16)
    k, g = _forget_gate(proj(_C_FF, HG_WIDTH), lbf_ref[...])
    kf_ref[0] = k.astype(BF16)
    gf_ref[0] = _chunk_scan(g, False)
    k, g = _forget_gate(proj(_C_FB, HG_WIDTH), lbb_ref[...])
    kb_ref[0] = k.astype(BF16)
    gb_ref[0] = _chunk_scan(g, True)
    v_ref[0] = proj(_C_I, HG_WIDTH).astype(BF16)
    zg_ref[0] = proj(_C_G, HG_WIDTH).astype(BF16)
    cos = cos_ref[...]
    sin = sin_ref[...]
    cos4 = jnp.concatenate([cos] * (ATT_WIDTH // LANES), axis=1)
    sin4 = jnp.concatenate([sin] * (ATT_WIDTH // LANES), axis=1)
    scale = HEAD_DIM ** -0.5
    aq_ref[0] = ((proj(_C_AQ, ATT_WIDTH) * cos4 + proj(_C_AQR, ATT_WIDTH) * sin4) * scale).astype(BF16)
    ak_ref[0] = (proj(_C_AK, KV_WIDTH) * cos + proj(_C_AKR, KV_WIDTH) * sin).astype(BF16)
    aks_ref[0] = (proj(_C_AKS, KV_WIDTH) * cos + proj(_C_AKSR, KV_WIDTH) * sin).astype(BF16)
    av_ref[0] = proj(_C_AV, KV_WIDTH).astype(BF16)
    avs_ref[0] = proj(_C_AVS, KV_WIDTH).astype(BF16)


def _inproj(x, sh, sc, w, lbf, lbb, cos, sin, tm):
    b, n, d = x.shape
    row = lambda bi, i: (bi, i, 0)
    per_b = lambda bi, i: (bi, 0, 0)
    const = lambda bi, i: (0, 0)
    tab = lambda bi, i: (i, 0)

    def out(width, dtype):
        return jax.ShapeDtypeStruct((b, n, width), dtype), pl.BlockSpec((1, tm, width), row)

    outs = [out(HG_WIDTH, BF16), out(HG_WIDTH, BF16), out(HG_WIDTH, F32), out(HG_WIDTH, BF16),
            out(HG_WIDTH, F32), out(HG_WIDTH, BF16), out(HG_WIDTH, BF16),
            out(ATT_WIDTH, BF16), out(KV_WIDTH, BF16), out(KV_WIDTH, BF16),
            out(KV_WIDTH, BF16), out(KV_WIDTH, BF16)]
    return pl.pallas_call(
        _inproj_kernel,
        grid=(b, n // tm),
        in_specs=[pl.BlockSpec((1, tm, d), row),
                  pl.BlockSpec((1, 1, d), per_b), pl.BlockSpec((1, 1, d), per_b),
                  pl.BlockSpec(w.shape, const),
                  pl.BlockSpec((1, HG_WIDTH), const), pl.BlockSpec((1, HG_WIDTH), const),
                  pl.BlockSpec((tm, LANES), tab), pl.BlockSpec((tm, LANES), tab)],
        out_specs=[o[1] for o in outs],
        out_shape=[o[0] for o in outs],
        compiler_params=_params("arbitrary", "arbitrary"),
        name="latent_inproj",
    )(x, sh, sc, w, lbf, lbb, cos, sin)


_X_FF, _X_FB, _X_I, _X_AK, _X_AKS, _X_AV, _X_AVS, _X_TOTAL = 0, 512, 1024, 1536, 1664, 1792, 1920, 2048


def _ctx_kernel(c_ref, sh_ref, sc_ref, w_ref, lbf_ref, lbb_ref,
                k_ref, ks_ref, v_ref, vs_ref, sf_ref, sb_ref):
    h = (_ln_rows(c_ref[0], NORM_EPS) * (1.0 + sc_ref[...]) + sh_ref[...]).astype(BF16)

    def proj(lo, n):
        return _dot(h, w_ref[:, lo:lo + n])

    k_ref[0] = proj(_X_AK, KV_WIDTH).astype(BF16)
    ks_ref[0] = proj(_X_AKS, KV_WIDTH).astype(BF16)
    v_ref[0] = proj(_X_AV, KV_WIDTH).astype(BF16)
    vs_ref[0] = proj(_X_AVS, KV_WIDTH).astype(BF16)

    kf, gf = _forget_gate(proj(_X_FF, HG_WIDTH), lbf_ref[...])
    kb, gb = _forget_gate(proj(_X_FB, HG_WIDTH), lbb_ref[...])
    vi = proj(_X_I, HG_WIDTH).astype(BF16)
    n = h.shape[0]
    r = lax.broadcasted_iota(jnp.int32, (n, n), 0)
    c = lax.broadcasted_iota(jnp.int32, (n, n), 1)
    bf = jnp.dot((c <= r).astype(F32), gf, precision=HIGHEST, preferred_element_type=F32)
    bb = jnp.dot((c >= r).astype(F32), gb, precision=HIGHEST, preferred_element_type=F32)
    kdf = (kf * jnp.exp(bf[n - 1:n] - bf)).astype(BF16)
    kdb = (kb * jnp.exp(bb[0:1] - bb)).astype(BF16)
    for hd in range(HG_HEADS):
        sl = slice(hd * HG_DIM, (hd + 1) * HG_DIM)
        sf_ref[0, hd] = _dot_tn(vi[:, sl], kdf[:, sl])
        sb_ref[0, hd] = _dot_tn(vi[:, sl], kdb[:, sl])


def _ctx(ctx, sh, sc, w, lbf, lbb):
    b, n, d = ctx.shape
    per_b = lambda bi: (bi, 0, 0)
    const = lambda bi: (0, 0)
    kv = (jax.ShapeDtypeStruct((b, n, KV_WIDTH), BF16), pl.BlockSpec((1, n, KV_WIDTH), per_b))
    st = (jax.ShapeDtypeStruct((b, HG_HEADS, HG_DIM, HG_DIM), F32),
          pl.BlockSpec((1, HG_HEADS, HG_DIM, HG_DIM), lambda bi: (bi, 0, 0, 0)))
    outs = [kv, kv, kv, kv, st, st]
    return pl.pallas_call(
        _ctx_kernel,
        grid=(b,),
        in_specs=[pl.BlockSpec((1, n, d), per_b),
                  pl.BlockSpec((1, d), const), pl.BlockSpec((1, d), const),
                  pl.BlockSpec(w.shape, const),
                  pl.BlockSpec((1, HG_WIDTH), const), pl.BlockSpec((1, HG_WIDTH), const)],
        out_specs=[o[1] for o in outs],
        out_shape=[o[0] for o in outs],
        compiler_params=_params("arbitrary"),
        name="context_side",
    )(ctx, sh, sc, w, lbf, lbb)


def _hgrn_chunk(q, k, v, b, st, reverse):
    cs, us = HG_CHUNK, HG_SUB
    ns = cs // us
    last = 0 if reverse else cs - 1
    b_last = b[last:last + 1]
    qf = q.astype(F32)
    kf = k.astype(F32)
    q_ref_rows, k_blocks = [], []
    for s in range(ns):
        if reverse:
            keys = slice(cs - us * (s + 1), cs)
            ref = b[cs - us * s:cs - us * s + 1] if s > 0 else jnp.zeros_like(b_last)
        else:
            keys = slice(0, us * (s + 1))
            ref = b[us * s - 1:us * s] if s > 0 else jnp.zeros_like(b_last)
        q_ref_rows.append(jnp.broadcast_to(ref, (us, HG_DIM)))
        kh = (kf[keys] * jnp.exp(ref - b[keys])).astype(BF16)
        pad = jnp.zeros((cs - us * (s + 1), HG_DIM), BF16)
        if pad.shape[0]:
            kh = jnp.concatenate([pad, kh] if reverse else [kh, pad], axis=0)
        k_blocks.append(kh)
    q_ref = jnp.concatenate(q_ref_rows[::-1] if reverse else q_ref_rows, axis=0)
    qh = qf * jnp.exp(b - q_ref)
    sub = lax.broadcasted_iota(jnp.int32, (cs, HG_DIM), 0) // us
    if reverse:
        sub = ns - 1 - sub
    q_cat = jnp.concatenate([jnp.where(sub == s, qh, 0.0).astype(BF16) for s in range(ns)], axis=1)
    att = _dot_nt(q_cat, jnp.concatenate(k_blocks, axis=1))
    ri = lax.broadcasted_iota(jnp.int32, (cs, cs), 0)
    ci = lax.broadcasted_iota(jnp.int32, (cs, cs), 1)
    att = jnp.where((ci >= ri) if reverse else (ci <= ri), att, 0.0)
    o = _dot(att.astype(BF16), v) + _dot_nt((qf * jnp.exp(b)).astype(BF16), st.astype(BF16))
    kdec = (kf * jnp.exp(b_last - b)).astype(BF16)
    st_new = st * jnp.exp(b_last) + _dot_tn(v, kdec)
    return o, st_new


def _hgrn_kernel(q_ref, kf_ref, gf_ref, kb_ref, gb_ref, v_ref, zg_ref, s0f_ref, s0b_ref, nw_ref,
                 y_ref, of_ref, ob_ref):
    n = q_ref.shape[1]
    cs = HG_CHUNK
    nc = n // cs

    def body(i, carry):
        sf, sb = carry
        fwd = pl.ds(pl.multiple_of(i * cs, cs), cs)
        bwd = pl.ds(pl.multiple_of((nc - 1 - i) * cs, cs), cs)
        o, sf = _hgrn_chunk(q_ref[0, fwd, :], kf_ref[0, fwd, :], v_ref[0, fwd, :], gf_ref[0, fwd, :],
                            sf, False)
        of_ref[fwd, :] = o
        o, sb = _hgrn_chunk(q_ref[0, bwd, :], kb_ref[0, bwd, :], v_ref[0, bwd, :], gb_ref[0, bwd, :],
                            sb, True)
        ob_ref[bwd, :] = o
        return sf, sb

    lax.fori_loop(0, nc, body, (s0f_ref[0, 0], s0b_ref[0, 0]), unroll=4)

    rb = min(n, 512)

    def readout(j, carry):
        sl = pl.ds(pl.multiple_of(j * rb, rb), rb)
        o = of_ref[sl, :] + ob_ref[sl, :]
        o = o * lax.rsqrt(jnp.mean(o * o, axis=-1, keepdims=True) + NORM_EPS) * nw_ref[...]
        y_ref[0, sl, :] = (o * _silu(zg_ref[0, sl, :].astype(F32))).astype(BF16)
        return carry

    lax.fori_loop(0, n // rb, readout, 0)


def _hgrn(q, kf, gf, kb, gb, v, zg, s0f, s0b, norm_w):
    b, n, _ = q.shape
    head = lambda bi, hi: (bi, 0, hi)
    st = lambda bi, hi: (bi, hi, 0, 0)
    seq = pl.BlockSpec((1, n, HG_DIM), head)
    state = pl.BlockSpec((1, 1, HG_DIM, HG_DIM), st)
    return pl.pallas_call(
        _hgrn_kernel,
        grid=(b, HG_HEADS),
        in_specs=[seq, seq, seq, seq, seq, seq, seq, state, state,
                  pl.BlockSpec((1, HG_DIM), lambda bi, hi: (0, hi))],
        out_specs=seq,
        out_shape=jax.ShapeDtypeStruct((b, n, HG_WIDTH), BF16),
        scratch_shapes=[pltpu.VMEM((n, HG_DIM), F32), pltpu.VMEM((n, HG_DIM), F32)],
        compiler_params=_params("arbitrary", "arbitrary"),
        name="hgrn2",
    )(q, kf, gf, kb, gb, v, zg, s0f, s0b, norm_w)


def _attn_kernel(sink_ref, q_ref, kp_ref, kc_ref, kn_ref, ksp_ref, ksc_ref, ksn_ref,
                 vp_ref, vc_ref, vn_ref, vsp_ref, vsc_ref, vsn_ref,
                 xk_ref, xks_ref, xv_ref, xvs_ref, y_ref):
    i = pl.program_id(1)
    nb = pl.num_programs(1)
    low = lax.broadcasted_iota(jnp.int32, (1, LANES), 1) < HEAD_DIM
    ctx_len = xk_ref.shape[1]
    rows = 2 * BAND
    ri = lax.broadcasted_iota(jnp.int32, (rows, BAND), 0) % BAND
    ci = lax.broadcasted_iota(jnp.int32, (rows, BAND), 1)
    ok_prev = (ci >= ri) & (i > 0)
    ok_next = (ci <= ri) & (i < nb - 1)
    p0, c0, n0 = ctx_len, ctx_len + BAND, ctx_len + 2 * BAND
    top = lax.broadcasted_iota(jnp.int32, (rows, 1), 0) < BAND

    def keys_of(refs, keep_low):
        x = jnp.concatenate([r[0] for r in refs], axis=0)
        return jnp.where(low if keep_low else ~low, x, jnp.zeros_like(x))

    group = Q_HEADS // KV_HEADS
    k_plain, k_swap = (xk_ref, kp_ref, kc_ref, kn_ref), (xks_ref, ksp_ref, ksc_ref, ksn_ref)
    v_plain, v_swap = (xv_ref, vp_ref, vc_ref, vn_ref), (xvs_ref, vsp_ref, vsc_ref, vsn_ref)
    for kvh in range(KV_HEADS):
        plain_low = kvh == 0
        tile = kvh * (group // 2)
        q = jnp.concatenate([q_ref[0, :, tile * LANES:(tile + 1) * LANES],
                             q_ref[0, :, (tile + 1) * LANES:(tile + 2) * LANES]], axis=0)
        acc = jnp.zeros((rows, LANES), F32)
        for sub in range(2):
            in_low = sub == 0
            use_plain = plain_low == in_low
            k = keys_of(k_plain if use_plain else k_swap, in_low)
            v = keys_of(v_plain if use_plain else v_swap, in_low)
            sink = jnp.where(top, sink_ref[group * kvh + sub], sink_ref[group * kvh + 2 + sub])
            s = _dot_nt(q, k)
            s = jnp.concatenate([s[:, :p0], jnp.where(ok_prev, s[:, p0:c0], -jnp.inf), s[:, c0:n0],
                                 jnp.where(ok_next, s[:, n0:], -jnp.inf)], axis=1)
            m = jnp.maximum(jnp.max(s, axis=1, keepdims=True), sink)
            e = jnp.exp(s - m)
            denom = jnp.sum(e, axis=1, keepdims=True) + jnp.exp(sink - m)
            acc = acc + _dot(e.astype(BF16), v) / denom
        y_ref[0, :, tile * LANES:(tile + 1) * LANES] = acc[:BAND].astype(BF16)
        y_ref[0, :, (tile + 1) * LANES:(tile + 2) * LANES] = acc[BAND:].astype(BF16)


def _attn(sink, aq, ak, aks, av, avs, xk, xks, xv, xvs):
    b, n, _ = aq.shape
    nb = n // BAND
    cur = lambda bi, i: (bi, i, 0)
    prev = lambda bi, i: (bi, jnp.maximum(i - 1, 0), 0)
    nxt = lambda bi, i: (bi, jnp.minimum(i + 1, nb - 1), 0)
    per_b = lambda bi, i: (bi, 0, 0)
    kv = lambda f: pl.BlockSpec((1, BAND, KV_WIDTH), f)
    cx = pl.BlockSpec((1, xk.shape[1], KV_WIDTH), per_b)
    return pl.pallas_call(
        _attn_kernel,
        grid=(b, nb),
        in_specs=[pl.BlockSpec(memory_space=pltpu.SMEM),
                  pl.BlockSpec((1, BAND, ATT_WIDTH), cur),
                  kv(prev), kv(cur), kv(nxt), kv(prev), kv(cur), kv(nxt),
                  kv(prev), kv(cur), kv(nxt), kv(prev), kv(cur), kv(nxt),
                  cx, cx, cx, cx],
        out_specs=pl.BlockSpec((1, BAND, ATT_WIDTH), cur),
        out_shape=jax.ShapeDtypeStruct((b, n, ATT_WIDTH), BF16),
        compiler_params=_params("arbitrary", "arbitrary"),
        name="window_attn",
    )(sink, aq, ak, ak, ak, aks, aks, aks, av, av, av, avs, avs, avs, xk, xks, xv, xvs)


def _route(hf, wr_t, bias):
    tm = hf.shape[0]
    scores = jax.nn.sigmoid(_dot_nt(wr_t, hf, precision=HIGHEST))
    sel = scores + bias
    grp = sel.reshape(N_GROUPS, GROUP_SIZE, tm)
    j = lax.broadcasted_iota(jnp.int32, grp.shape, 1)
    m1 = jnp.max(grp, axis=1, keepdims=True)
    first = jnp.min(jnp.where(grp == m1, j, GROUP_SIZE), axis=1, keepdims=True)
    m2 = jnp.max(jnp.where(j == first, -jnp.inf, grp), axis=1, keepdims=True)
    gs = (m1 + m2).reshape(N_GROUPS, tm)
    gi = lax.broadcasted_iota(jnp.int32, gs.shape, 0)
    rank = jnp.zeros(gs.shape, jnp.int32)
    for g in range(N_GROUPS):
        other = gs[g:g + 1]
        rank = rank + ((other > gs) | ((other == gs) & (g < gi))).astype(jnp.int32)
    gsel = rank < TOPK_GROUPS
    emask = jnp.broadcast_to(gsel[:, None, :], grp.shape).reshape(N_EXPERTS, tm)
    cand = jnp.where(emask, sel, -jnp.inf)
    ei = lax.broadcasted_iota(jnp.int32, cand.shape, 0)
    rank = jnp.zeros(cand.shape, jnp.int32)
    for e in range(N_EXPERTS):
        other = cand[e:e + 1]
        rank = rank + ((other > cand) | ((other == cand) & (e < ei))).astype(jnp.int32)
    chosen = rank < TOP_K
    w = jnp.where(chosen, scores, 0.0)
    return w / jnp.sum(w, axis=0, keepdims=True) * ROUTED_SCALE, jnp.where(chosen, 1.0, 0.0)


def _sort_rows(chosen, gates_t):
    ne, ts = chosen.shape
    sel = chosen.astype(BF16)
    chosen = chosen > 0.5
    r = lax.broadcasted_iota(jnp.int32, (ts, ts), 0)
    c = lax.broadcasted_iota(jnp.int32, (ts, ts), 1)
    seen = _dot(sel, jnp.where(r <= c, 1.0, 0.0).astype(BF16))
    total = _dot(sel, jnp.ones((ts, ts), BF16))
    padded = jnp.floor((total + (SORT_UNIT - 1)) * (1.0 / SORT_UNIT)) * SORT_UNIT
    er = lax.broadcasted_iota(jnp.int32, (ne, ne), 0)
    ec = lax.broadcasted_iota(jnp.int32, (ne, ne), 1)
    before = jnp.where(ec < er, 1.0, 0.0).astype(BF16)
    start = _dot(before, padded.astype(BF16))
    choice = _dot(before, sel)
    row = start + seen - 1.0
    dest, gate = [], []
    for k in range(TOP_K):
        mk = chosen & (choice == k)
        dest.append(jnp.sum(jnp.where(mk, row, 0.0), axis=0, keepdims=True))
        gate.append(jnp.sum(jnp.where(mk, gates_t, 0.0), axis=0, keepdims=True))
    sel_pad = jnp.concatenate([sel, jnp.zeros((LANES - ne, ts), BF16)], axis=0)
    counts = _dot_nt(jnp.ones((8, ts), BF16), sel_pad)
    dest = jnp.concatenate(dest, axis=0)
    pad = jnp.zeros((LANES - TOP_K, ts), F32)
    dest_cols = jnp.concatenate([dest, pad], axis=0).T
    gate_cols = jnp.concatenate(gate + [pad], axis=0).T
    return dest.astype(jnp.int32), dest_cols.astype(jnp.int32), gate_cols, counts


def _outproj_kernel(x_ref, yh_ref, ya_ref, w_ref, g1_ref, sh_ref, sc_ref, lw_ref, lb_ref,
                    wr_ref, rb_ref, x1_ref, h2_ref, dest_ref, destc_ref, gatec_ref, cnt_ref):
    y = _dot(yh_ref[0], w_ref[:HG_WIDTH, :]) + _dot(ya_ref[0], w_ref[HG_WIDTH:, :])
    x1 = _ln_rows(ALPHA * x_ref[0] + g1_ref[0] * y, LN_EPS) * lw_ref[...] + lb_ref[...]
    x1_ref[0] = x1
    hf = _ln_rows(x1, NORM_EPS) * (1.0 + sc_ref[0]) + sh_ref[0]
    h2_ref[0] = hf.astype(BF16)
    gates_t, chosen = _route(hf, wr_ref[...], rb_ref[...])
    for s in range(hf.shape[0] // SORT_TILE):
        sl = slice(s * SORT_TILE, (s + 1) * SORT_TILE)
        dest_ref[s], destc_ref[s], gatec_ref[s], cnt_ref[s] = _sort_rows(chosen[:, sl], gates_t[:, sl])


def _outproj(x, yh, ya, w, g1, sh2, sc2, lw, lb, wr_t, rbias, tm):
    b, n, d = x.shape
    row = lambda bi, i: (bi, i, 0)
    per_b = lambda bi, i: (bi, 0, 0)
    const = lambda bi, i: (0, 0)
    nt = b * n // SORT_TILE
    per_step = tm // SORT_TILE
    tiles = lambda bi, i: (bi * (n // tm) + i, 0, 0)
    return pl.pallas_call(
        _outproj_kernel,
        grid=(b, n // tm),
        in_specs=[pl.BlockSpec((1, tm, d), row),
                  pl.BlockSpec((1, tm, HG_WIDTH), row), pl.BlockSpec((1, tm, ATT_WIDTH), row),
                  pl.BlockSpec(w.shape, const),
                  pl.BlockSpec((1, 1, d), per_b), pl.BlockSpec((1, 1, d), per_b),
                  pl.BlockSpec((1, 1, d), per_b),
                  pl.BlockSpec((1, d), const), pl.BlockSpec((1, d), const),
                  pl.BlockSpec(wr_t.shape, const), pl.BlockSpec(rbias.shape, const)],
        out_specs=[pl.BlockSpec((1, tm, d), row), pl.BlockSpec((1, tm, d), row),
                   pl.BlockSpec((per_step, TOP_K, SORT_TILE), tiles),
                   pl.BlockSpec((per_step, SORT_TILE, LANES), tiles),
                   pl.BlockSpec((per_step, SORT_TILE, LANES), tiles),
                   pl.BlockSpec((per_step, 8, LANES), tiles)],
        out_shape=[jax.ShapeDtypeStruct((b, n, d), F32), jax.ShapeDtypeStruct((b, n, d), BF16),
                   jax.ShapeDtypeStruct((nt, TOP_K, SORT_TILE), jnp.int32),
                   jax.ShapeDtypeStruct((nt, SORT_TILE, LANES), jnp.int32),
                   jax.ShapeDtypeStruct((nt, SORT_TILE, LANES), F32),
                   jax.ShapeDtypeStruct((nt, 8, LANES), F32)],
        compiler_params=_params("arbitrary", "arbitrary"),
        name="outproj_ln_router",
    )(x, yh, ya, w, g1, sh2, sc2, lw, lb, wr_t, rbias)


def _moe_layout(counts, max_tiles):
    n_pad = (counts + (SORT_UNIT - 1)) // SORT_UNIT * SORT_UNIT
    total = jnp.sum(n_pad, axis=0)
    region = (total + (EXPERT_TILE - 1)) // EXPERT_TILE * EXPERT_TILE
    base = jnp.cumsum(region) - region
    chunk_row = base[None, :] + jnp.cumsum(n_pad, axis=0) - n_pad
    tile_end = jnp.cumsum(region // EXPERT_TILE)
    n_used = tile_end[-1]
    j = jnp.minimum(jnp.arange(max_tiles, dtype=jnp.int32), n_used - 1)
    tile_expert = jnp.sum((tile_end[None, :] <= j[:, None]).astype(jnp.int32), axis=1)
    tile_rows = jnp.broadcast_to(jnp.sum(n_pad, axis=1, keepdims=True), n_pad.shape)
    chunks = jnp.concatenate([chunk_row, n_pad, tile_rows], axis=1).astype(jnp.int32)[:, None, :]
    tails = jnp.concatenate([base + total, (region - total) // SORT_UNIT]).astype(jnp.int32)
    return chunks, tails, tile_expert, n_used.astype(jnp.int32)[None]


def _row_cases(rows_used, fn):
    @pl.when(rows_used <= USUAL_ROWS)
    def _():
        fn(USUAL_ROWS)

    @pl.when(rows_used > USUAL_ROWS)
    def _():
        fn(LOCAL_ROWS)


def _unit(ref, row):
    return ref.at[pl.ds(pl.multiple_of(row, SORT_UNIT), SORT_UNIT), :]


def _for_each_unit(chunks_ref, fn):
    def per_expert(e, local):
        rows = chunks_ref[0, 0, N_EXPERTS + e]
        start = chunks_ref[0, 0, e]

        def per_unit(u, carry):
            fn(local + u * SORT_UNIT, start + u * SORT_UNIT)
            return carry

        lax.fori_loop(0, rows // SORT_UNIT, per_unit, 0)
        return local + rows

    return lax.fori_loop(0, N_EXPERTS, per_expert, 0) // SORT_UNIT


def _dispatch_kernel(chunks_ref, tails_ref, h_ref, dest_ref, xs_ref, buf_ref, zero_ref, pending_ref, sem):
    i = pl.program_id(0)
    slot = i % 2
    buf = buf_ref.at[slot]

    def drain(s):
        def body(u, carry):
            pltpu.make_async_copy(_unit(buf_ref.at[s], 0), _unit(xs_ref, 0), sem.at[s]).wait()
            return carry

        lax.fori_loop(0, pending_ref[s], body, 0)

    @pl.when(i >= 2)
    def _():
        drain(slot)

    def permute(rows):
        r = lax.broadcasted_iota(jnp.int32, (rows, SORT_TILE), 0)
        p = jnp.zeros((rows, SORT_TILE), F32)
        for k in range(TOP_K):
            p = jnp.where(r == dest_ref[0, k:k + 1, :], 1.0, p)
        buf[0:rows, :] = _dot(p.astype(BF16), h_ref[...]).astype(BF16)

    _row_cases(chunks_ref[0, 0, 2 * N_EXPERTS], permute)
    pending_ref[slot] = _for_each_unit(
        chunks_ref,
        lambda loc, glob: pltpu.make_async_copy(_unit(buf, loc), _unit(xs_ref, glob), sem.at[slot]).start())

    @pl.when(i == pl.num_programs(0) - 1)
    def _():
        zero_ref[...] = jnp.zeros_like(zero_ref)

        def per_expert(e, total):
            n = tails_ref[N_EXPERTS + e]
            start = tails_ref[e]

            def per_unit(u, carry):
                pltpu.make_async_copy(zero_ref, _unit(xs_ref, start + u * SORT_UNIT), sem.at[2]).start()
                return carry

            lax.fori_loop(0, n, per_unit, 0)
            return total + n

        total = lax.fori_loop(0, N_EXPERTS, per_expert, 0)

        def drain_zero(u, carry):
            pltpu.make_async_copy(zero_ref, _unit(xs_ref, 0), sem.at[2]).wait()
            return carry

        lax.fori_loop(0, total, drain_zero, 0)

        @pl.when(i >= 1)
        def _():
            drain(1 - slot)

        drain(slot)


def _dispatch(chunks, tails, h2, dest, max_rows):
    t, d = h2.shape
    nt = t // SORT_TILE
    return pl.pallas_call(
        _dispatch_kernel,
        grid=(nt,),
        in_specs=[pl.BlockSpec((1, 1, 3 * N_EXPERTS), lambda i: (i, 0, 0), memory_space=pltpu.SMEM),
                  pl.BlockSpec(memory_space=pltpu.SMEM),
                  pl.BlockSpec((SORT_TILE, d), lambda i: (i, 0)),
                  pl.BlockSpec((1, TOP_K, SORT_TILE), lambda i: (i, 0, 0))],
        out_specs=pl.BlockSpec(memory_space=pl.ANY),
        out_shape=jax.ShapeDtypeStruct((max_rows, d), BF16),
        scratch_shapes=[pltpu.VMEM((2, LOCAL_ROWS, d), BF16), pltpu.VMEM((SORT_UNIT, d), BF16),
                        pltpu.SMEM((2,), jnp.int32), pltpu.SemaphoreType.DMA((3,))],
        compiler_params=_params("arbitrary"),
        name="moe_dispatch",
    )(chunks, tails, h2, dest)


def _expert_kernel(te_ref, nu_ref, xs_ref, wg_ref, wu_ref, wd_ref, ys_ref):
    @pl.when(pl.program_id(0) < nu_ref[0])
    def _():
        x = xs_ref[...]
        a = _silu(_dot(x, wg_ref[0])) * _dot(x, wu_ref[0])
        ys_ref[...] = _dot(a.astype(BF16), wd_ref[0]).astype(BF16)


def _experts(tile_expert, n_used, xs, wg, wu, wd):
    rows, d = xs.shape
    ff = wg.shape[2]
    row = lambda j, te, nu: (jnp.minimum(j, jnp.maximum(nu[0] - 1, 0)), 0)
    exp = lambda j, te, nu: (te[j], 0, 0)
    return pl.pallas_call(
        _expert_kernel,
        grid_spec=pltpu.PrefetchScalarGridSpec(
            num_scalar_prefetch=2,
            grid=(rows // EXPERT_TILE,),
            in_specs=[pl.BlockSpec((EXPERT_TILE, d), row),
                      pl.BlockSpec((1, d, ff), exp), pl.BlockSpec((1, d, ff), exp),
                      pl.BlockSpec((1, ff, d), exp)],
            out_specs=pl.BlockSpec((EXPERT_TILE, d), row)),
        out_shape=jax.ShapeDtypeStruct((rows, d), BF16),
        compiler_params=_params("arbitrary"),
        name="moe_experts",
    )(tile_expert, n_used, xs, wg, wu, wd)


def _combine_kernel(chunks_ref, next_ref, ys_ref, dest_ref, gate_ref, h_ref, x1_ref, g2_ref, lw_ref, lb_ref,
                    swg_ref, swu_ref, swd_ref, o_ref, buf_ref, sem):
    i = pl.program_id(0)
    slot = i % 2

    def fetch(meta_ref, s):
        _for_each_unit(
            meta_ref,
            lambda loc, glob: pltpu.make_async_copy(_unit(ys_ref, glob), _unit(buf_ref.at[s], loc), sem.at[s]).start())

    @pl.when(i == 0)
    def _():
        buf_ref[...] = jnp.zeros_like(buf_ref)
        fetch(chunks_ref, 0)

    @pl.when(i + 1 < pl.num_programs(0))
    def _():
        fetch(next_ref, 1 - slot)

    h = h_ref[...]
    a = _silu(_dot(h, swg_ref[...])) * _dot(h, swu_ref[...])
    shared = _dot(a.astype(BF16), swd_ref[...])
    rows_used = chunks_ref[0, 0, 2 * N_EXPERTS]

    def wait(u, carry):
        pltpu.make_async_copy(_unit(ys_ref, 0), _unit(buf_ref.at[slot], 0), sem.at[slot]).wait()
        return carry

    lax.fori_loop(0, rows_used // SORT_UNIT, wait, 0)

    def unpermute(rows):
        lane = lax.broadcasted_iota(jnp.int32, (SORT_TILE, rows), 1)
        p = jnp.zeros((SORT_TILE, rows), F32)
        for k in range(TOP_K):
            p = jnp.where(lane == dest_ref[0, :, k:k + 1], gate_ref[0, :, k:k + 1], p)
        ffn = shared + _dot(p.astype(BF16), buf_ref[slot, 0:rows, :])
        u = ALPHA * x1_ref[...] + g2_ref[0] * ffn
        o_ref[...] = _ln_rows(u, LN_EPS) * lw_ref[...] + lb_ref[...]

    _row_cases(rows_used, unpermute)


def _combine(chunks, ys, dest, gate, h2, x1, g2, lw, lb, swg, swu, swd, tiles_per_batch):
    t, d = h2.shape
    ff = swg.shape[1]
    nt = t // SORT_TILE
    tile = lambda i: (i, 0, 0)
    nxt = lambda i: (jnp.minimum(i + 1, nt - 1), 0, 0)
    row = lambda i: (i, 0)
    const = lambda i: (0, 0)
    return pl.pallas_call(
        _combine_kernel,
        grid=(nt,),
        in_specs=[pl.BlockSpec((1, 1, 3 * N_EXPERTS), tile, memory_space=pltpu.SMEM),
                  pl.BlockSpec((1, 1, 3 * N_EXPERTS), nxt, memory_space=pltpu.SMEM),
                  pl.BlockSpec(memory_space=pl.ANY),
                  pl.BlockSpec((1, SORT_TILE, LANES), tile), pl.BlockSpec((1, SORT_TILE, LANES), tile),
                  pl.BlockSpec((SORT_TILE, d), row), pl.BlockSpec((SORT_TILE, d), row),
                  pl.BlockSpec((1, 1, d), lambda i: (i // tiles_per_batch, 0, 0)),
                  pl.BlockSpec((1, d), const), pl.BlockSpec((1, d), const),
                  pl.BlockSpec((d, ff), const), pl.BlockSpec((d, ff), const), pl.BlockSpec((ff, d), const)],
        out_specs=pl.BlockSpec((SORT_TILE, d), row),
        out_shape=jax.ShapeDtypeStruct((t, d), F32),
        scratch_shapes=[pltpu.VMEM((2, LOCAL_ROWS, d), BF16), pltpu.SemaphoreType.DMA((2,))],
        compiler_params=_params("arbitrary"),
        name="moe_combine",
    )(chunks, chunks, ys, dest, gate, h2, x1, g2, lw, lb, swg, swu, swd)


def _moe(h2, x1, dest, dest_cols, gate_cols, counts, g2, lw, lb, wg, wu, wd, swg, swu, swd):
    b, n, d = x1.shape
    t = b * n
    nt = t // SORT_TILE
    max_rows = t * TOP_K + nt * N_EXPERTS * (SORT_UNIT - 1) + N_EXPERTS * (EXPERT_TILE - SORT_UNIT)
    max_rows = -(-max_rows // EXPERT_TILE) * EXPERT_TILE
    chunks, tails, tile_expert, n_used = _moe_layout(
        counts[:, 0, :N_EXPERTS].astype(jnp.int32), max_rows // EXPERT_TILE)
    h2 = h2.reshape(t, d)
    xs = _dispatch(chunks, tails, h2, dest, max_rows)
    ys = _experts(tile_expert, n_used, xs, wg, wu, wd)
    out = _combine(chunks, ys, dest_cols, gate_cols, h2, x1.reshape(t, d), g2, lw, lb, swg, swu, swd,
                   n // SORT_TILE)
    return out.reshape(b, n, d)


def _rotate_half_columns(w):
    d, cols = w.shape
    blk = w.reshape(d, cols // (2 * ROT_PAIRS), 2, ROT_PAIRS)
    return jnp.stack([-blk[:, :, 1], blk[:, :, 0]], axis=2).reshape(d, cols)


def _swap_kv_heads(w):
    return jnp.concatenate([w[:, HEAD_DIM:], w[:, :HEAD_DIM]], axis=1)


def _split_w_in(w_in):
    bounds = np.cumsum([HG_WIDTH] * 5 + [ATT_WIDTH, KV_WIDTH])
    return jnp.split(w_in, [int(v) for v in bounds], axis=1)


def _latent_weight(w_in):
    zq, zff, zfb, zi, zg, aq, ak, av = _split_w_in(w_in)
    aks = _swap_kv_heads(ak)
    cols = [zq, zff, zfb, zi, zg, aq, _rotate_half_columns(aq), ak, _rotate_half_columns(ak),
            aks, _rotate_half_columns(aks), av, _swap_kv_heads(av)]
    return jnp.concatenate(cols, axis=1).astype(BF16)


def _context_weight(w_in):
    _, zff, zfb, zi, _, _, ak, av = _split_w_in(w_in)
    return jnp.concatenate([zff, zfb, zi, ak, _swap_kv_heads(ak), av, _swap_kv_heads(av)],
                           axis=1).astype(BF16)


def _rope_tables(n):
    pos = jnp.arange(n)
    freqs = ROPE_BASE ** (-jnp.arange(ROT_PAIRS, dtype=F32) / ROT_PAIRS)
    ang_row = (pos // GRID_W).astype(F32)[:, None] * freqs
    ang_col = (pos % GRID_W).astype(F32)[:, None] * freqs
    ang = jnp.concatenate([ang_row, ang_row, ang_col, ang_col], axis=1)
    ang = jnp.concatenate([ang] * (LANES // HEAD_DIM), axis=1)
    return jnp.cos(ang), jnp.sin(ang)


def kernel(x, c, ctx, c_ctx, w_ada, b_ada, w_in, hg_lb_fwd, hg_lb_bwd, hg_norm_w, attn_sink, w_out, ln1_w, ln1_b, router_w, router_bias, exp_w_gate, exp_w_up, exp_w_down, shared_w_gate, shared_w_up, shared_w_down, ln2_w, ln2_b):
    b, n, d = x.shape
    layer = 0
    rows = -(-(b + 1) // 8) * 8
    cc = jnp.zeros((rows, d), F32).at[:b].set(c).at[b].set(c_ctx)
    mod = _ada(cc, w_ada[layer], b_ada[layer][None, :])
    sh1, sc1, g1, sh2, sc2, g2 = [m[:, None, :] for m in jnp.split(mod[:b], 6, axis=1)]
    csh1, csc1 = mod[b:b + 1, :d], mod[b:b + 1, d:2 * d]

    lb_f = jnp.cumsum(jax.nn.softmax(hg_lb_fwd.astype(F32), axis=0), axis=0)[layer][None, :]
    lb_b = jnp.cumsum(jax.nn.softmax(hg_lb_bwd.astype(F32), axis=0), axis=0)[layer][None, :]
    cos, sin = _rope_tables(n)

    xk, xks, xv, xvs, s0f, s0b = _ctx(ctx, csh1, csc1, _context_weight(w_in[layer]), lb_f, lb_b)
    tm = min(n, 512)
    (q, kf, gf, kb, gb, v, zg, aq, ak, aks, av, avs) = _inproj(
        x, sh1, sc1, _latent_weight(w_in[layer]), lb_f, lb_b, cos, sin, tm)
    yh = _hgrn(q, kf, gf, kb, gb, v, zg, s0f, s0b, hg_norm_w[layer][None, :])
    ya = _attn(attn_sink[layer], aq, ak, aks, av, avs, xk, xks, xv, xvs)

    x1, h2, dest, dest_cols, gate_cols, counts = _outproj(
        x, yh, ya, w_out[layer].astype(BF16), g1, sh2, sc2, ln1_w[layer][None, :], ln1_b[layer][None, :],
        router_w[layer].T, router_bias[layer][:, None], tm)
    return _moe(h2, x1, dest, dest_cols, gate_cols, counts, g2, ln2_w[layer][None, :], ln2_b[layer][None, :],
                exp_w_gate[layer].astype(BF16), exp_w_up[layer].astype(BF16),
                exp_w_down[layer].astype(BF16), shared_w_gate[layer].astype(BF16),
                shared_w_up[layer].astype(BF16), shared_w_down[layer].astype(BF16))
```

```python
import jax
import jax.numpy as jnp
import numpy as np
from jax import lax
from jax.experimental import pallas as pl
from jax.experimental.pallas import tpu as pltpu

F32 = jnp.float32
BF16 = jnp.bfloat16
HIGHEST = lax.Precision.HIGHEST

DEPTH = 1
GRID_W = 64
HG_WIDTH = 512
HG_HEADS = 4
HG_DIM = 128
HG_CHUNK = 64
HG_SUB = 16
LOG_F_MIN = -4.0
HEAD_DIM = 64
Q_HEADS = 8
KV_HEADS = 2
ATT_WIDTH = Q_HEADS * HEAD_DIM
KV_WIDTH = KV_HEADS * HEAD_DIM
BAND = 128
ROPE_BASE = 10000.0
ROT_PAIRS = HEAD_DIM // 4
N_EXPERTS = 64
TOP_K = 8
N_GROUPS = 8
GROUP_SIZE = N_EXPERTS // N_GROUPS
TOPK_GROUPS = 4
ROUTED_SCALE = 2.5
LN_EPS = 1e-5
NORM_EPS = 1e-6
ALPHA = (2.0 * DEPTH) ** 0.25

LANES = 128
BF16_SUBLANES = 16
MXU_DIM = 256

SORT_TILE = MXU_DIM
SORT_UNIT = BF16_SUBLANES
EXPERT_TILE = 1024
_LOCAL_WORST = SORT_TILE * TOP_K + N_EXPERTS * (SORT_UNIT - 1)
LOCAL_ROWS = -(-_LOCAL_WORST // MXU_DIM) * MXU_DIM
USUAL_ROWS = SORT_TILE * TOP_K + 2 * MXU_DIM
LOCAL_UNITS = LOCAL_ROWS // SORT_UNIT
TABLE_WIDTH = LOCAL_UNITS + LANES - LOCAL_UNITS % LANES
V7X_VMEM_LIMIT_BYTES = 56 * 1024 * 1024


def _params(*sem):
    return pltpu.CompilerParams(dimension_semantics=sem, vmem_limit_bytes=V7X_VMEM_LIMIT_BYTES)


def _ln_rows(x, eps):
    mu = jnp.mean(x, axis=-1, keepdims=True)
    xc = x - mu
    return xc * lax.rsqrt(jnp.mean(xc * xc, axis=-1, keepdims=True) + eps)


def _silu(x):
    return x * jax.nn.sigmoid(x)


def _dot(a, b):
    return jnp.dot(a, b, preferred_element_type=F32)


def _dot_nt(a, b, precision=None):
    return lax.dot_general(a, b, (((1,), (1,)), ((), ())), precision=precision,
                           preferred_element_type=F32)


def _dot_tn(a, b):
    return lax.dot_general(a, b, (((0,), (0,)), ((), ())), preferred_element_type=F32)


def _forget_gate(z, lb):
    f = lb + (1.0 - lb) * jax.nn.sigmoid(z)
    return 1.0 - f, jnp.maximum(jnp.log(f), LOG_F_MIN)


def _chunk_scan(g, reverse):
    n = g.shape[0]
    pos = lax.broadcasted_iota(jnp.int32, g.shape, 0) % HG_CHUNK
    step = 1
    while step < HG_CHUNK:
        if reverse:
            g = g + jnp.where(pos < HG_CHUNK - step, pltpu.roll(g, n - step, axis=0), 0.0)
        else:
            g = g + jnp.where(pos >= step, pltpu.roll(g, step, axis=0), 0.0)
        step *= 2
    return g


def _ada_kernel(c_ref, w_ref, b_ref, o_ref):
    c = c_ref[...]
    o_ref[...] = jnp.dot(_silu(c), w_ref[...], precision=HIGHEST,
                         preferred_element_type=F32) + b_ref[...]


def _ada(cc, w, b):
    rows, d = cc.shape
    cols = w.shape[1]
    tn = 512
    return pl.pallas_call(
        _ada_kernel,
        grid=(cols // tn,),
        in_specs=[pl.BlockSpec((rows, d), lambda j: (0, 0)),
                  pl.BlockSpec((d, tn), lambda j: (0, j)),
                  pl.BlockSpec((1, tn), lambda j: (0, j))],
        out_specs=pl.BlockSpec((rows, tn), lambda j: (0, j)),
        out_shape=jax.ShapeDtypeStruct((rows, cols), F32),
        compiler_params=_params("arbitrary"),
        name="adaln",
    )(cc, w, b)


_C_Q, _C_FF, _C_FB, _C_I, _C_G = 0, 512, 1024, 1536, 2048
_C_AQ, _C_AQR = 2560, 3072
_C_AK, _C_AKR, _C_AKS, _C_AKSR = 3584, 3712, 3840, 3968
_C_AV, _C_AVS = 4096, 4224
_C_TOTAL = 4352


def _inproj_kernel(x_ref, sh_ref, sc_ref, w_ref, lbf_ref, lbb_ref, cos_ref, sin_ref,
                   q_ref, kf_ref, gf_ref, kb_ref, gb_ref, v_ref, zg_ref,
                   aq_ref, ak_ref, aks_ref, av_ref, avs_ref):
    h = (_ln_rows(x_ref[0], NORM_EPS) * (1.0 + sc_ref[0]) + sh_ref[0]).astype(BF16)

    def proj(lo, n):
        return _dot(h, w_ref[:, lo:lo + n])

    q_ref[0] = proj(_C_Q, HG_WIDTH).astype(BF16)
    k, g = _forget_gate(proj(_C_FF, HG_WIDTH), lbf_ref[...])
    kf_ref[0] = k.astype(BF16)
    gf_ref[0] = _chunk_scan(g, False)
    k, g = _forget_gate(proj(_C_FB, HG_WIDTH), lbb_ref[...])
    kb_ref[0] = k.astype(BF16)
    gb_ref[0] = _chunk_scan(g, True)
    v_ref[0] = proj(_C_I, HG_WIDTH).astype(BF16)
    zg_ref[0] = proj(_C_G, HG_WIDTH).astype(BF16)
    cos = cos_ref[...]
    sin = sin_ref[...]
    cos4 = jnp.concatenate([cos] * (ATT_WIDTH // LANES), axis=1)
    sin4 = jnp.concatenate([sin] * (ATT_WIDTH // LANES), axis=1)
    scale = HEAD_DIM ** -0.5
    aq_ref[0] = ((proj(_C_AQ, ATT_WIDTH) * cos4 + proj(_C_AQR, ATT_WIDTH) * sin4) * scale).astype(BF16)
    ak_ref[0] = (proj(_C_AK, KV_WIDTH) * cos + proj(_C_AKR, KV_WIDTH) * sin).astype(BF16)
    aks_ref[0] = (proj(_C_AKS, KV_WIDTH) * cos + proj(_C_AKSR, KV_WIDTH) * sin).astype(BF16)
    av_ref[0] = proj(_C_AV, KV_WIDTH).astype(BF16)
    avs_ref[0] = proj(_C_AVS, KV_WIDTH).astype(BF16)


def _inproj(x, sh, sc, w, lbf, lbb, cos, sin, tm):
    b, n, d = x.shape
    row = lambda bi, i: (bi, i, 0)
    per_b = lambda bi, i: (bi, 0, 0)
    const = lambda bi, i: (0, 0)
    tab = lambda bi, i: (i, 0)

    def out(width, dtype):
        return jax.ShapeDtypeStruct((b, n, width), dtype), pl.BlockSpec((1, tm, width), row)

    outs = [out(HG_WIDTH, BF16), out(HG_WIDTH, BF16), out(HG_WIDTH, F32), out(HG_WIDTH, BF16),
            out(HG_WIDTH, F32), out(HG_WIDTH, BF16), out(HG_WIDTH, BF16),
            out(ATT_WIDTH, BF16), out(KV_WIDTH, BF16), out(KV_WIDTH, BF16),
            out(KV_WIDTH, BF16), out(KV_WIDTH, BF16)]
    return pl.pallas_call(
        _inproj_kernel,
        grid=(b, n // tm),
        in_specs=[pl.BlockSpec((1, tm, d), row),
                  pl.BlockSpec((1, 1, d), per_b), pl.BlockSpec((1, 1, d), per_b),
                  pl.BlockSpec(w.shape, const),
                  pl.BlockSpec((1, HG_WIDTH), const), pl.BlockSpec((1, HG_WIDTH), const),
                  pl.BlockSpec((tm, LANES), tab), pl.BlockSpec((tm, LANES), tab)],
        out_specs=[o[1] for o in outs],
        out_shape=[o[0] for o in outs],
        compiler_params=_params("arbitrary", "arbitrary"),
        name="latent_inproj",
    )(x, sh, sc, w, lbf, lbb, cos, sin)


_X_FF, _X_FB, _X_I, _X_AK, _X_AKS, _X_AV, _X_AVS, _X_TOTAL = 0, 512, 1024, 1536, 1664, 1792, 1920, 2048


def _ctx_kernel(c_ref, sh_ref, sc_ref, w_ref, lbf_ref, lbb_ref,
                k_ref, ks_ref, v_ref, vs_ref, sf_ref, sb_ref):
    h = (_ln_rows(c_ref[0], NORM_EPS) * (1.0 + sc_ref[...]) + sh_ref[...]).astype(BF16)

    def proj(lo, n):
        return _dot(h, w_ref[:, lo:lo + n])

    k_ref[0] = proj(_X_AK, KV_WIDTH).astype(BF16)
    ks_ref[0] = proj(_X_AKS, KV_WIDTH).astype(BF16)
    v_ref[0] = proj(_X_AV, KV_WIDTH).astype(BF16)
    vs_ref[0] = proj(_X_AVS, KV_WIDTH).astype(BF16)

    kf, gf = _forget_gate(proj(_X_FF, HG_WIDTH), lbf_ref[...])
    kb, gb = _forget_gate(proj(_X_FB, HG_WIDTH), lbb_ref[...])
    vi = proj(_X_I, HG_WIDTH).astype(BF16)
    n = h.shape[0]
    r = lax.broadcasted_iota(jnp.int32, (n, n), 0)
    c = lax.broadcasted_iota(jnp.int32, (n, n), 1)
    bf = jnp.dot((c <= r).astype(F32), gf, precision=HIGHEST, preferred_element_type=F32)
    bb = jnp.dot((c >= r).astype(F32), gb, precision=HIGHEST, preferred_element_type=F32)
    kdf = (kf * jnp.exp(bf[n - 1:n] - bf)).astype(BF16)
    kdb = (kb * jnp.exp(bb[0:1] - bb)).astype(BF16)
    for hd in range(HG_HEADS):
        sl = slice(hd * HG_DIM, (hd + 1) * HG_DIM)
        sf_ref[0, hd] = _dot_tn(vi[:, sl], kdf[:, sl])
        sb_ref[0, hd] = _dot_tn(vi[:, sl], kdb[:, sl])


def _ctx(ctx, sh, sc, w, lbf, lbb):
    b, n, d = ctx.shape
    per_b = lambda bi: (bi, 0, 0)
    const = lambda bi: (0, 0)
    kv = (jax.ShapeDtypeStruct((b, n, KV_WIDTH), BF16), pl.BlockSpec((1, n, KV_WIDTH), per_b))
    st = (jax.ShapeDtypeStruct((b, HG_HEADS, HG_DIM, HG_DIM), F32),
          pl.BlockSpec((1, HG_HEADS, HG_DIM, HG_DIM), lambda bi: (bi, 0, 0, 0)))
    outs = [kv, kv, kv, kv, st, st]
    return pl.pallas_call(
        _ctx_kernel,
        grid=(b,),
        in_specs=[pl.BlockSpec((1, n, d), per_b),
                  pl.BlockSpec((1, d), const), pl.BlockSpec((1, d), const),
                  pl.BlockSpec(w.shape, const),
                  pl.BlockSpec((1, HG_WIDTH), const), pl.BlockSpec((1, HG_WIDTH), const)],
        out_specs=[o[1] for o in outs],
        out_shape=[o[0] for o in outs],
        compiler_params=_params("arbitrary"),
        name="context_side",
    )(ctx, sh, sc, w, lbf, lbb)


def _hgrn_chunk(q, k, v, b, st, reverse):
    cs, us = HG_CHUNK, HG_SUB
    ns = cs // us
    last = 0 if reverse else cs - 1
    b_last = b[last:last + 1]
    qf = q.astype(F32)
    kf = k.astype(F32)
    q_ref_rows, k_blocks = [], []
    for s in range(ns):
        if reverse:
            keys = slice(cs - us * (s + 1), cs)
            ref = b[cs - us * s:cs - us * s + 1] if s > 0 else jnp.zeros_like(b_last)
        else:
            keys = slice(0, us * (s + 1))
            ref = b[us * s - 1:us * s] if s > 0 else jnp.zeros_like(b_last)
        q_ref_rows.append(jnp.broadcast_to(ref, (us, HG_DIM)))
        kh = (kf[keys] * jnp.exp(ref - b[keys])).astype(BF16)
        pad = jnp.zeros((cs - us * (s + 1), HG_DIM), BF16)
        if pad.shape[0]:
            kh = jnp.concatenate([pad, kh] if reverse else [kh, pad], axis=0)
        k_blocks.append(kh)
    q_ref = jnp.concatenate(q_ref_rows[::-1] if reverse else q_ref_rows, axis=0)
    qh = qf * jnp.exp(b - q_ref)
    sub = lax.broadcasted_iota(jnp.int32, (cs, HG_DIM), 0) // us
    if reverse:
        sub = ns - 1 - sub
    q_cat = jnp.concatenate([jnp.where(sub == s, qh, 0.0).astype(BF16) for s in range(ns)], axis=1)
    att = _dot_nt(q_cat, jnp.concatenate(k_blocks, axis=1))
    ri = lax.broadcasted_iota(jnp.int32, (cs, cs), 0)
    ci = lax.broadcasted_iota(jnp.int32, (cs, cs), 1)
    att = jnp.where((ci >= ri) if reverse else (ci <= ri), att, 0.0)
    o = _dot(att.astype(BF16), v) + _dot_nt((qf * jnp.exp(b)).astype(BF16), st.astype(BF16))
    kdec = (kf * jnp.exp(b_last - b)).astype(BF16)
    st_new = st * jnp.exp(b_last) + _dot_tn(v, kdec)
    return o, st_new


def _hgrn_kernel(q_ref, kf_ref, gf_ref, kb_ref, gb_ref, v_ref, zg_ref, s0f_ref, s0b_ref, nw_ref,
                 y_ref, of_ref, ob_ref):
    n = q_ref.shape[1]
    cs = HG_CHUNK
    nc = n // cs

    def body(i, carry):
        sf, sb = carry
        fwd = pl.ds(pl.multiple_of(i * cs, cs), cs)
        bwd = pl.ds(pl.multiple_of((nc - 1 - i) * cs, cs), cs)
        o, sf = _hgrn_chunk(q_ref[0, fwd, :], kf_ref[0, fwd, :], v_ref[0, fwd, :], gf_ref[0, fwd, :],
                            sf, False)
        of_ref[fwd, :] = o
        o, sb = _hgrn_chunk(q_ref[0, bwd, :], kb_ref[0, bwd, :], v_ref[0, bwd, :], gb_ref[0, bwd, :],
                            sb, True)
        ob_ref[bwd, :] = o
        return sf, sb

    lax.fori_loop(0, nc, body, (s0f_ref[0, 0], s0b_ref[0, 0]), unroll=4)

    rb = min(n, 512)

    def readout(j, carry):
        sl = pl.ds(pl.multiple_of(j * rb, rb), rb)
        o = of_ref[sl, :] + ob_ref[sl, :]
        o = o * lax.rsqrt(jnp.mean(o * o, axis=-1, keepdims=True) + NORM_EPS) * nw_ref[...]
        y_ref[0, sl, :] = (o * _silu(zg_ref[0, sl, :].astype(F32))).astype(BF16)
        return carry

    lax.fori_loop(0, n // rb, readout, 0)


def _hgrn(q, kf, gf, kb, gb, v, zg, s0f, s0b, norm_w):
    b, n, _ = q.shape
    head = lambda bi, hi: (bi, 0, hi)
    st = lambda bi, hi: (bi, hi, 0, 0)
    seq = pl.BlockSpec((1, n, HG_DIM), head)
    state = pl.BlockSpec((1, 1, HG_DIM, HG_DIM), st)
    return pl.pallas_call(
        _hgrn_kernel,
        grid=(b, HG_HEADS),
        in_specs=[seq, seq, seq, seq, seq, seq, seq, state, state,
                  pl.BlockSpec((1, HG_DIM), lambda bi, hi: (0, hi))],
        out_specs=seq,
        out_shape=jax.ShapeDtypeStruct((b, n, HG_WIDTH), BF16),
        scratch_shapes=[pltpu.VMEM((n, HG_DIM), F32), pltpu.VMEM((n, HG_DIM), F32)],
        compiler_params=_params("arbitrary", "arbitrary"),
        name="hgrn2",
    )(q, kf, gf, kb, gb, v, zg, s0f, s0b, norm_w)


def _attn_kernel(sink_ref, q_ref, kp_ref, kc_ref, kn_ref, ksp_ref, ksc_ref, ksn_ref,
                 vp_ref, vc_ref, vn_ref, vsp_ref, vsc_ref, vsn_ref,
                 xk_ref, xks_ref, xv_ref, xvs_ref, y_ref):
    i = pl.program_id(1)
    nb = pl.num_programs(1)
    low = lax.broadcasted_iota(jnp.int32, (1, LANES), 1) < HEAD_DIM
    ctx_len = xk_ref.shape[1]
    rows = 2 * BAND
    ri = lax.broadcasted_iota(jnp.int32, (rows, BAND), 0) % BAND
    ci = lax.broadcasted_iota(jnp.int32, (rows, BAND), 1)
    ok_prev = (ci >= ri) & (i > 0)
    ok_next = (ci <= ri) & (i < nb - 1)
    p0, c0, n0 = ctx_len, ctx_len + BAND, ctx_len + 2 * BAND
    top = lax.broadcasted_iota(jnp.int32, (rows, 1), 0) < BAND

    def keys_of(refs, keep_low):
        x = jnp.concatenate([r[0] for r in refs], axis=0)
        return jnp.where(low if keep_low else ~low, x, jnp.zeros_like(x))

    group = Q_HEADS // KV_HEADS
    k_plain, k_swap = (xk_ref, kp_ref, kc_ref, kn_ref), (xks_ref, ksp_ref, ksc_ref, ksn_ref)
    v_plain, v_swap = (xv_ref, vp_ref, vc_ref, vn_ref), (xvs_ref, vsp_ref, vsc_ref, vsn_ref)
    for kvh in range(KV_HEADS):
        plain_low = kvh == 0
        tile = kvh * (group // 2)
        q = jnp.concatenate([q_ref[0, :, tile * LANES:(tile + 1) * LANES],
                             q_ref[0, :, (tile + 1) * LANES:(tile + 2) * LANES]], axis=0)
        acc = jnp.zeros((rows, LANES), F32)
        for sub in range(2):
            in_low = sub == 0
            use_plain = plain_low == in_low
            k = keys_of(k_plain if use_plain else k_swap, in_low)
            v = keys_of(v_plain if use_plain else v_swap, in_low)
            sink = jnp.where(top, sink_ref[group * kvh + sub], sink_ref[group * kvh + 2 + sub])
            s = _dot_nt(q, k)
            s = jnp.concatenate([s[:, :p0], jnp.where(ok_prev, s[:, p0:c0], -jnp.inf), s[:, c0:n0],
                                 jnp.where(ok_next, s[:, n0:], -jnp.inf)], axis=1)
            m = jnp.maximum(jnp.max(s, axis=1, keepdims=True), sink)
            e = jnp.exp(s - m)
            denom = jnp.sum(e, axis=1, keepdims=True) + jnp.exp(sink - m)
            acc = acc + _dot(e.astype(BF16), v) / denom
        y_ref[0, :, tile * LANES:(tile + 1) * LANES] = acc[:BAND].astype(BF16)
        y_ref[0, :, (tile + 1) * LANES:(tile + 2) * LANES] = acc[BAND:].astype(BF16)


def _attn(sink, aq, ak, aks, av, avs, xk, xks, xv, xvs):
    b, n, _ = aq.shape
    nb = n // BAND
    cur = lambda bi, i: (bi, i, 0)
    prev = lambda bi, i: (bi, jnp.maximum(i - 1, 0), 0)
    nxt = lambda bi, i: (bi, jnp.minimum(i + 1, nb - 1), 0)
    per_b = lambda bi, i: (bi, 0, 0)
    kv = lambda f: pl.BlockSpec((1, BAND, KV_WIDTH), f)
    cx = pl.BlockSpec((1, xk.shape[1], KV_WIDTH), per_b)
    return pl.pallas_call(
        _attn_kernel,
        grid=(b, nb),
        in_specs=[pl.BlockSpec(memory_space=pltpu.SMEM),
                  pl.BlockSpec((1, BAND, ATT_WIDTH), cur),
                  kv(prev), kv(cur), kv(nxt), kv(prev), kv(cur), kv(nxt),
                  kv(prev), kv(cur), kv(nxt), kv(prev), kv(cur), kv(nxt),
                  cx, cx, cx, cx],
        out_specs=pl.BlockSpec((1, BAND, ATT_WIDTH), cur),
        out_shape=jax.ShapeDtypeStruct((b, n, ATT_WIDTH), BF16),
        compiler_params=_params("arbitrary", "arbitrary"),
        name="window_attn",
    )(sink, aq, ak, ak, ak, aks, aks, aks, av, av, av, avs, avs, avs, xk, xks, xv, xvs)


def _route(hf, wr_t, bias):
    tm = hf.shape[0]
    ne = wr_t.shape[0]
    h_hi = hf.astype(BF16)
    h_lo = (hf - h_hi.astype(F32)).astype(BF16)
    w_hi = wr_t.astype(BF16)
    w_lo = (wr_t - w_hi.astype(F32)).astype(BF16)
    first = _dot_nt(jnp.concatenate([w_hi, w_lo], axis=0), h_hi)
    scores = jax.nn.sigmoid(first[:ne] + first[ne:] + _dot_nt(w_hi, h_lo))
    sel = scores + bias
    grp = sel.reshape(N_GROUPS, GROUP_SIZE, tm)
    j = lax.broadcasted_iota(jnp.int32, grp.shape, 1)
    m1 = jnp.max(grp, axis=1, keepdims=True)
    first = jnp.min(jnp.where(grp == m1, j, GROUP_SIZE), axis=1, keepdims=True)
    m2 = jnp.max(jnp.where(j == first, -jnp.inf, grp), axis=1, keepdims=True)
    gs = (m1 + m2).reshape(N_GROUPS, tm)
    gi = lax.broadcasted_iota(jnp.int32, gs.shape, 0)
    rank = jnp.zeros(gs.shape, jnp.int32)
    for g in range(N_GROUPS):
        other = gs[g:g + 1]
        rank = rank + ((other > gs) | ((other == gs) & (g < gi))).astype(jnp.int32)
    gsel = rank < TOPK_GROUPS
    emask = jnp.broadcast_to(gsel[:, None, :], grp.shape).reshape(N_EXPERTS, tm)
    cand = jnp.where(emask, sel, -jnp.inf)
    ei = lax.broadcasted_iota(jnp.int32, cand.shape, 0)
    chosen = jnp.zeros(cand.shape, jnp.bool_)
    for _ in range(TOP_K):
        best = jnp.max(cand, axis=0, keepdims=True)
        first = jnp.min(jnp.where(cand == best, ei, N_EXPERTS), axis=0, keepdims=True)
        hit = ei == first
        chosen = chosen | hit
        cand = jnp.where(hit, -jnp.inf, cand)
    w = jnp.where(chosen, scores, 0.0)
    return w / jnp.sum(w, axis=0, keepdims=True) * ROUTED_SCALE, jnp.where(chosen, 1.0, 0.0)


def _sort_rows(chosen, gates_t):
    ne, ts = chosen.shape
    sel = chosen.astype(BF16)
    chosen = chosen > 0.5
    r = lax.broadcasted_iota(jnp.int32, (ts, ts), 0)
    c = lax.broadcasted_iota(jnp.int32, (ts, ts), 1)
    seen = _dot(sel, jnp.where(r <= c, 1.0, 0.0).astype(BF16))
    total = _dot(sel, jnp.ones((ts, ts), BF16))
    padded = jnp.floor((total + (SORT_UNIT - 1)) * (1.0 / SORT_UNIT)) * SORT_UNIT
    er = lax.broadcasted_iota(jnp.int32, (ne, ne), 0)
    ec = lax.broadcasted_iota(jnp.int32, (ne, ne), 1)
    before = jnp.where(ec < er, 1.0, 0.0).astype(BF16)
    start = _dot(before, padded.astype(BF16))
    choice = _dot(before, sel)
    row = start + seen - 1.0
    dest, gate = [], []
    for k in range(TOP_K):
        mk = chosen & (choice == k)
        dest.append(jnp.sum(jnp.where(mk, row, 0.0), axis=0, keepdims=True))
        gate.append(jnp.sum(jnp.where(mk, gates_t, 0.0), axis=0, keepdims=True))
    sel_pad = jnp.concatenate([sel, jnp.zeros((LANES - ne, ts), BF16)], axis=0)
    counts = _dot_nt(jnp.ones((8, ts), BF16), sel_pad)
    dest = jnp.concatenate(dest, axis=0)
    pad = jnp.zeros((LANES - TOP_K, ts), F32)
    dest_cols = jnp.concatenate([dest, pad], axis=0).T
    gate_cols = jnp.concatenate(gate + [pad], axis=0).T
    return dest.astype(jnp.int32), dest_cols.astype(jnp.int32), gate_cols, counts


def _outproj_kernel(x_ref, yh_ref, ya_ref, w_ref, g1_ref, sh_ref, sc_ref, lw_ref, lb_ref,
                    wr_ref, rb_ref, x1_ref, h2_ref, dest_ref, destc_ref, gatec_ref, cnt_ref):
    y = _dot(yh_ref[0], w_ref[:HG_WIDTH, :]) + _dot(ya_ref[0], w_ref[HG_WIDTH:, :])
    x1 = _ln_rows(ALPHA * x_ref[0] + g1_ref[0] * y, LN_EPS) * lw_ref[...] + lb_ref[...]
    x1_ref[0] = x1
    hf = _ln_rows(x1, NORM_EPS) * (1.0 + sc_ref[0]) + sh_ref[0]
    h2_ref[0] = hf.astype(BF16)
    gates_t, chosen = _route(hf, wr_ref[...], rb_ref[...])
    for s in range(hf.shape[0] // SORT_TILE):
        sl = slice(s * SORT_TILE, (s + 1) * SORT_TILE)
        dest_ref[s], destc_ref[s], gatec_ref[s], cnt_ref[s] = _sort_rows(chosen[:, sl], gates_t[:, sl])


def _outproj(x, yh, ya, w, g1, sh2, sc2, lw, lb, wr_t, rbias, tm):
    b, n, d = x.shape
    row = lambda bi, i: (bi, i, 0)
    per_b = lambda bi, i: (bi, 0, 0)
    const = lambda bi, i: (0, 0)
    nt = b * n // SORT_TILE
    per_step = tm // SORT_TILE
    tiles = lambda bi, i: (bi * (n // tm) + i, 0, 0)
    return pl.pallas_call(
        _outproj_kernel,
        grid=(b, n // tm),
        in_specs=[pl.BlockSpec((1, tm, d), row),
                  pl.BlockSpec((1, tm, HG_WIDTH), row), pl.BlockSpec((1, tm, ATT_WIDTH), row),
                  pl.BlockSpec(w.shape, const),
                  pl.BlockSpec((1, 1, d), per_b), pl.BlockSpec((1, 1, d), per_b),
                  pl.BlockSpec((1, 1, d), per_b),
                  pl.BlockSpec((1, d), const), pl.BlockSpec((1, d), const),
                  pl.BlockSpec(wr_t.shape, const), pl.BlockSpec(rbias.shape, const)],
        out_specs=[pl.BlockSpec((1, tm, d), row), pl.BlockSpec((1, tm, d), row),
                   pl.BlockSpec((per_step, TOP_K, SORT_TILE), tiles),
                   pl.BlockSpec((per_step, SORT_TILE, LANES), tiles),
                   pl.BlockSpec((per_step, SORT_TILE, LANES), tiles),
                   pl.BlockSpec((per_step, 8, LANES), tiles)],
        out_shape=[jax.ShapeDtypeStruct((b, n, d), F32), jax.ShapeDtypeStruct((b, n, d), BF16),
                   jax.ShapeDtypeStruct((nt, TOP_K, SORT_TILE), jnp.int32),
                   jax.ShapeDtypeStruct((nt, SORT_TILE, LANES), jnp.int32),
                   jax.ShapeDtypeStruct((nt, SORT_TILE, LANES), F32),
                   jax.ShapeDtypeStruct((nt, 8, LANES), F32)],
        compiler_params=_params("arbitrary", "arbitrary"),
        name="outproj_ln_router",
    )(x, yh, ya, w, g1, sh2, sc2, lw, lb, wr_t, rbias)


def _moe_layout(counts, max_tiles):
    n_pad = (counts + (SORT_UNIT - 1)) // SORT_UNIT * SORT_UNIT
    total = jnp.sum(n_pad, axis=0)
    region = (total + (EXPERT_TILE - 1)) // EXPERT_TILE * EXPERT_TILE
    base = jnp.cumsum(region) - region
    chunk_row = base[None, :] + jnp.cumsum(n_pad, axis=0) - n_pad
    tile_end = jnp.cumsum(region // EXPERT_TILE)
    n_used = tile_end[-1]
    j = jnp.minimum(jnp.arange(max_tiles, dtype=jnp.int32), n_used - 1)
    tile_expert = jnp.sum((tile_end[None, :] <= j[:, None]).astype(jnp.int32), axis=1)
    units = n_pad // SORT_UNIT
    units_end = jnp.cumsum(units, axis=1)
    u = jnp.arange(LOCAL_UNITS, dtype=jnp.int32)
    owner = (u[None, :, None] >= (units_end - units)[:, None, :]) & (u[None, :, None] < units_end[:, None, :])
    shift = chunk_row - (units_end - units) * SORT_UNIT
    unit_row = jnp.sum(jnp.where(owner, shift[:, None, :], 0), axis=2) + u[None, :] * SORT_UNIT
    tile_rows = jnp.broadcast_to(jnp.sum(n_pad, axis=1, keepdims=True),
                                 (n_pad.shape[0], TABLE_WIDTH - LOCAL_UNITS))
    table = jnp.concatenate([unit_row, tile_rows], axis=1).astype(jnp.int32)[:, None, :]
    tails = jnp.concatenate([base + total, (region - total) // SORT_UNIT]).astype(jnp.int32)
    return table, tails, tile_expert, n_used.astype(jnp.int32)[None]


def _row_cases(rows_used, fn):
    @pl.when(rows_used <= USUAL_ROWS)
    def _():
        fn(USUAL_ROWS)

    @pl.when(rows_used > USUAL_ROWS)
    def _():
        fn(LOCAL_ROWS)


def _unit(ref, row):
    return ref.at[pl.ds(pl.multiple_of(row, SORT_UNIT), SORT_UNIT), :]


def _for_each_unit(table_ref, fn):
    units = table_ref[0, 0, LOCAL_UNITS] // SORT_UNIT

    def per_unit(u, carry):
        fn(u * SORT_UNIT, table_ref[0, 0, u])
        return carry

    lax.fori_loop(0, units, per_unit, 0)
    return units


def _dispatch_kernel(chunks_ref, tails_ref, h_ref, dest_ref, xs_ref, buf_ref, zero_ref, pending_ref, sem):
    i = pl.program_id(0)
    slot = i % 2
    buf = buf_ref.at[slot]

    def drain(s):
        def body(u, carry):
            pltpu.make_async_copy(_unit(buf_ref.at[s], 0), _unit(xs_ref, 0), sem.at[s]).wait()
            return carry

        lax.fori_loop(0, pending_ref[s], body, 0)

    @pl.when(i >= 2)
    def _():
        drain(slot)

    def permute(rows):
        r = lax.broadcasted_iota(jnp.int32, (rows, SORT_TILE), 0).astype(jnp.int16)
        p = jnp.zeros((rows, SORT_TILE), BF16)
        one = jnp.ones((rows, SORT_TILE), BF16)
        for k in range(TOP_K):
            hit = r == dest_ref[0, k:k + 1, :].astype(jnp.int16)
            p = jnp.where(hit, one, p)
        buf[0:rows, :] = _dot(p, h_ref[...]).astype(BF16)

    _row_cases(chunks_ref[0, 0, LOCAL_UNITS], permute)
    pending_ref[slot] = _for_each_unit(
        chunks_ref,
        lambda loc, glob: pltpu.make_async_copy(_unit(buf, loc), _unit(xs_ref, glob), sem.at[slot]).start())

    @pl.when(i == pl.num_programs(0) - 1)
    def _():
        zero_ref[...] = jnp.zeros_like(zero_ref)

        def per_expert(e, total):
            n = tails_ref[N_EXPERTS + e]
            start = tails_ref[e]

            def per_unit(u, carry):
                pltpu.make_async_copy(zero_ref, _unit(xs_ref, start + u * SORT_UNIT), sem.at[2]).start()
                return carry

            lax.fori_loop(0, n, per_unit, 0)
            return total + n

        total = lax.fori_loop(0, N_EXPERTS, per_expert, 0)

        def drain_zero(u, carry):
            pltpu.make_async_copy(zero_ref, _unit(xs_ref, 0), sem.at[2]).wait()
            return carry

        lax.fori_loop(0, total, drain_zero, 0)

        @pl.when(i >= 1)
        def _():
            drain(1 - slot)

        drain(slot)


def _dispatch(chunks, tails, h2, dest, max_rows):
    t, d = h2.shape
    nt = t // SORT_TILE
    return pl.pallas_call(
        _dispatch_kernel,
        grid=(nt,),
        in_specs=[pl.BlockSpec((1, 1, TABLE_WIDTH), lambda i: (i, 0, 0), memory_space=pltpu.SMEM),
                  pl.BlockSpec(memory_space=pltpu.SMEM),
                  pl.BlockSpec((SORT_TILE, d), lambda i: (i, 0)),
                  pl.BlockSpec((1, TOP_K, SORT_TILE), lambda i: (i, 0, 0))],
        out_specs=pl.BlockSpec(memory_space=pl.ANY),
        out_shape=jax.ShapeDtypeStruct((max_rows, d), BF16),
        scratch_shapes=[pltpu.VMEM((2, LOCAL_ROWS, d), BF16), pltpu.VMEM((SORT_UNIT, d), BF16),
                        pltpu.SMEM((2,), jnp.int32), pltpu.SemaphoreType.DMA((3,))],
        compiler_params=_params("arbitrary"),
        name="moe_dispatch",
    )(chunks, tails, h2, dest)


def _expert_kernel(te_ref, nu_ref, xs_ref, wg_ref, wu_ref, wd_ref, ys_ref):
    @pl.when(pl.program_id(0) < nu_ref[0])
    def _():
        x = xs_ref[...]
        a = _silu(_dot(x, wg_ref[0].astype(BF16))) * _dot(x, wu_ref[0].astype(BF16))
        ys_ref[...] = _dot(a.astype(BF16), wd_ref[0].astype(BF16)).astype(BF16)


def _experts(tile_expert, n_used, xs, wg, wu, wd):
    rows, d = xs.shape
    ff = wg.shape[2]
    row = lambda j, te, nu: (jnp.minimum(j, jnp.maximum(nu[0] - 1, 0)), 0)
    exp = lambda j, te, nu: (te[j], 0, 0)
    return pl.pallas_call(
        _expert_kernel,
        grid_spec=pltpu.PrefetchScalarGridSpec(
            num_scalar_prefetch=2,
            grid=(rows // EXPERT_TILE,),
            in_specs=[pl.BlockSpec((EXPERT_TILE, d), row),
                      pl.BlockSpec((1, d, ff), exp), pl.BlockSpec((1, d, ff), exp),
                      pl.BlockSpec((1, ff, d), exp)],
            out_specs=pl.BlockSpec((EXPERT_TILE, d), row)),
        out_shape=jax.ShapeDtypeStruct((rows, d), BF16),
        compiler_params=_params("arbitrary"),
        name="moe_experts",
    )(tile_expert, n_used, xs, wg, wu, wd)


def _combine_kernel(chunks_ref, next_ref, ys_ref, dest_ref, gate_ref, h_ref, x1_ref, g2_ref, lw_ref, lb_ref,
                    swg_ref, swu_ref, swd_ref, o_ref, buf_ref, sem):
    i = pl.program_id(0)
    slot = i % 2

    def fetch(meta_ref, s):
        _for_each_unit(
            meta_ref,
            lambda loc, glob: pltpu.make_async_copy(_unit(ys_ref, glob), _unit(buf_ref.at[s], loc), sem.at[s]).start())

    @pl.when(i == 0)
    def _():
        buf_ref[...] = jnp.zeros_like(buf_ref)
        fetch(chunks_ref, 0)

    @pl.when(i + 1 < pl.num_programs(0))
    def _():
        fetch(next_ref, 1 - slot)

    h = h_ref[...]
    a = _silu(_dot(h, swg_ref[...])) * _dot(h, swu_ref[...])
    shared = _dot(a.astype(BF16), swd_ref[...])
    rows_used = chunks_ref[0, 0, LOCAL_UNITS]

    def wait(u, carry):
        pltpu.make_async_copy(_unit(ys_ref, 0), _unit(buf_ref.at[slot], 0), sem.at[slot]).wait()
        return carry

    lax.fori_loop(0, rows_used // SORT_UNIT, wait, 0)

    def unpermute(rows):
        lane = lax.broadcasted_iota(jnp.int32, (SORT_TILE, rows), 1).astype(jnp.int16)
        p = jnp.zeros((SORT_TILE, rows), BF16)
        for k in range(TOP_K):
            hit = lane == dest_ref[0, :, k:k + 1].astype(jnp.int16)
            p = jnp.where(hit, jnp.broadcast_to(gate_ref[0, :, k:k + 1].astype(BF16), p.shape), p)
        ffn = shared + _dot(p, buf_ref[slot, 0:rows, :])
        u = ALPHA * x1_ref[...] + g2_ref[0] * ffn
        o_ref[...] = _ln_rows(u, LN_EPS) * lw_ref[...] + lb_ref[...]

    _row_cases(rows_used, unpermute)


def _combine(chunks, ys, dest, gate, h2, x1, g2, lw, lb, swg, swu, swd, tiles_per_batch):
    t, d = h2.shape
    ff = swg.shape[1]
    nt = t // SORT_TILE
    tile = lambda i: (i, 0, 0)
    nxt = lambda i: (jnp.minimum(i + 1, nt - 1), 0, 0)
    row = lambda i: (i, 0)
    const = lambda i: (0, 0)
    return pl.pallas_call(
        _combine_kernel,
        grid=(nt,),
        in_specs=[pl.BlockSpec((1, 1, TABLE_WIDTH), tile, memory_space=pltpu.SMEM),
                  pl.BlockSpec((1, 1, TABLE_WIDTH), nxt, memory_space=pltpu.SMEM),
                  pl.BlockSpec(memory_space=pl.ANY),
                  pl.BlockSpec((1, SORT_TILE, LANES), tile), pl.BlockSpec((1, SORT_TILE, LANES), tile),
                  pl.BlockSpec((SORT_TILE, d), row), pl.BlockSpec((SORT_TILE, d), row),
                  pl.BlockSpec((1, 1, d), lambda i: (i // tiles_per_batch, 0, 0)),
                  pl.BlockSpec((1, d), const), pl.BlockSpec((1, d), const),
                  pl.BlockSpec((d, ff), const), pl.BlockSpec((d, ff), const), pl.BlockSpec((ff, d), const)],
        out_specs=pl.BlockSpec((SORT_TILE, d), row),
        out_shape=jax.ShapeDtypeStruct((t, d), F32),
        scratch_shapes=[pltpu.VMEM((2, LOCAL_ROWS, d), BF16), pltpu.SemaphoreType.DMA((2,))],
        compiler_params=_params("arbitrary"),
        name="moe_combine",
    )(chunks, chunks, ys, dest, gate, h2, x1, g2, lw, lb, swg, swu, swd)


def _moe(h2, x1, dest, dest_cols, gate_cols, counts, g2, lw, lb, wg, wu, wd, swg, swu, swd):
    b, n, d = x1.shape
    t = b * n
    nt = t // SORT_TILE
    max_rows = t * TOP_K + nt * N_EXPERTS * (SORT_UNIT - 1) + N_EXPERTS * (EXPERT_TILE - SORT_UNIT)
    max_rows = -(-max_rows // EXPERT_TILE) * EXPERT_TILE
    chunks, tails, tile_expert, n_used = _moe_layout(
        counts[:, 0, :N_EXPERTS].astype(jnp.int32), max_rows // EXPERT_TILE)
    h2 = h2.reshape(t, d)
    xs = _dispatch(chunks, tails, h2, dest, max_rows)
    ys = _experts(tile_expert, n_used, xs, wg, wu, wd)
    out = _combine(chunks, ys, dest_cols, gate_cols, h2, x1.reshape(t, d), g2, lw, lb, swg, swu, swd,
                   n // SORT_TILE)
    return out.reshape(b, n, d)


def _rotate_half_columns(w):
    d, cols = w.shape
    blk = w.reshape(d, cols // (2 * ROT_PAIRS), 2, ROT_PAIRS)
    return jnp.stack([-blk[:, :, 1], blk[:, :, 0]], axis=2).reshape(d, cols)


def _swap_kv_heads(w):
    return jnp.concatenate([w[:, HEAD_DIM:], w[:, :HEAD_DIM]], axis=1)


def _split_w_in(w_in):
    bounds = np.cumsum([HG_WIDTH] * 5 + [ATT_WIDTH, KV_WIDTH])
    return jnp.split(w_in, [int(v) for v in bounds], axis=1)


def _latent_weight(w_in):
    zq, zff, zfb, zi, zg, aq, ak, av = _split_w_in(w_in)
    aks = _swap_kv_heads(ak)
    cols = [zq, zff, zfb, zi, zg, aq, _rotate_half_columns(aq), ak, _rotate_half_columns(ak),
            aks, _rotate_half_columns(aks), av, _swap_kv_heads(av)]
    return jnp.concatenate(cols, axis=1).astype(BF16)


def _context_weight(w_in):
    _, zff, zfb, zi, _, _, ak, av = _split_w_in(w_in)
    return jnp.concatenate([zff, zfb, zi, ak, _swap_kv_heads(ak), av, _swap_kv_heads(av)],
                           axis=1).astype(BF16)


def _rope_tables(n):
    pos = jnp.arange(n)
    freqs = ROPE_BASE ** (-jnp.arange(ROT_PAIRS, dtype=F32) / ROT_PAIRS)
    ang_row = (pos // GRID_W).astype(F32)[:, None] * freqs
    ang_col = (pos % GRID_W).astype(F32)[:, None] * freqs
    ang = jnp.concatenate([ang_row, ang_row, ang_col, ang_col], axis=1)
    ang = jnp.concatenate([ang] * (LANES // HEAD_DIM), axis=1)
    return jnp.cos(ang), jnp.sin(ang)


def kernel(x, c, ctx, c_ctx, w_ada, b_ada, w_in, hg_lb_fwd, hg_lb_bwd, hg_norm_w, attn_sink, w_out, ln1_w, ln1_b, router_w, router_bias, exp_w_gate, exp_w_up, exp_w_down, shared_w_gate, shared_w_up, shared_w_down, ln2_w, ln2_b):
    b, n, d = x.shape
    layer = 0
    rows = -(-(b + 1) // 8) * 8
    cc = jnp.zeros((rows, d), F32).at[:b].set(c).at[b].set(c_ctx)
    mod = _ada(cc, w_ada[layer], b_ada[layer][None, :])
    sh1, sc1, g1, sh2, sc2, g2 = [m[:, None, :] for m in jnp.split(mod[:b], 6, axis=1)]
    csh1, csc1 = mod[b:b + 1, :d], mod[b:b + 1, d:2 * d]

    lb_f = jnp.cumsum(jax.nn.softmax(hg_lb_fwd.astype(F32), axis=0), axis=0)[layer][None, :]
    lb_b = jnp.cumsum(jax.nn.softmax(hg_lb_bwd.astype(F32), axis=0), axis=0)[layer][None, :]
    cos, sin = _rope_tables(n)

    xk, xks, xv, xvs, s0f, s0b = _ctx(ctx, csh1, csc1, _context_weight(w_in[layer]), lb_f, lb_b)
    tm = min(n, 512)
    (q, kf, gf, kb, gb, v, zg, aq, ak, aks, av, avs) = _inproj(
        x, sh1, sc1, _latent_weight(w_in[layer]), lb_f, lb_b, cos, sin, tm)
    yh = _hgrn(q, kf, gf, kb, gb, v, zg, s0f, s0b, hg_norm_w[layer][None, :])
    ya = _attn(attn_sink[layer], aq, ak, aks, av, avs, xk, xks, xv, xvs)

    x1, h2, dest, dest_cols, gate_cols, counts = _outproj(
        x, yh, ya, w_out[layer].astype(BF16), g1, sh2, sc2, ln1_w[layer][None, :], ln1_b[layer][None, :],
        router_w[layer].T, router_bias[layer][:, None], tm)
    return _moe(h2, x1, dest, dest_cols, gate_cols, counts, g2, ln2_w[layer][None, :], ln2_b[layer][None, :],
                exp_w_gate[layer], exp_w_up[layer], exp_w_down[layer], shared_w_gate[layer].astype(BF16),
                shared_w_up[layer].astype(BF16), shared_w_down[layer].astype(BF16))
```

```python
import jax
import jax.numpy as jnp
import numpy as np
from jax import lax
from jax.experimental import pallas as pl
from jax.experimental.pallas import tpu as pltpu

F32 = jnp.float32
BF16 = jnp.bfloat16
HIGHEST = lax.Precision.HIGHEST

DEPTH = 1
GRID_W = 64
HG_WIDTH = 512
HG_HEADS = 4
HG_DIM = 128
HG_CHUNK = 64
HG_SUB = 16
LOG_F_MIN = -4.0
HEAD_DIM = 64
Q_HEADS = 8
KV_HEADS = 2
ATT_WIDTH = Q_HEADS * HEAD_DIM
KV_WIDTH = KV_HEADS * HEAD_DIM
BAND = 128
ROPE_BASE = 10000.0
ROT_PAIRS = HEAD_DIM // 4
N_EXPERTS = 64
TOP_K = 8
N_GROUPS = 8
GROUP_SIZE = N_EXPERTS // N_GROUPS
TOPK_GROUPS = 4
ROUTED_SCALE = 2.5
LN_EPS = 1e-5
NORM_EPS = 1e-6
ALPHA = (2.0 * DEPTH) ** 0.25

LANES = 128
BF16_SUBLANES = 16
MXU_DIM = 256

SORT_TILE = MXU_DIM
SORT_UNIT = BF16_SUBLANES
EXPERT_TILE = 1024
_LOCAL_WORST = SORT_TILE * TOP_K + N_EXPERTS * (SORT_UNIT - 1)
LOCAL_ROWS = -(-_LOCAL_WORST // MXU_DIM) * MXU_DIM
USUAL_ROWS = SORT_TILE * TOP_K + 2 * MXU_DIM
LOCAL_UNITS = LOCAL_ROWS // SORT_UNIT
COMMON_UNITS = (2, 3)
PIECE_LISTS = tuple((size, N_EXPERTS) for size in COMMON_UNITS) + ((1, LOCAL_UNITS),)
TABLE_HEADER = 8
TABLE_ROWS_USED = len(PIECE_LISTS)
TABLE_WIDTH = -(-(TABLE_HEADER + 2 * sum(width for _, width in PIECE_LISTS)) // LANES) * LANES
WAIT_UNITS = 8
V7X_VMEM_LIMIT_BYTES = 56 * 1024 * 1024


def _params(*sem):
    return pltpu.CompilerParams(dimension_semantics=sem, vmem_limit_bytes=V7X_VMEM_LIMIT_BYTES)


def _ln_rows(x, eps):
    mu = jnp.mean(x, axis=-1, keepdims=True)
    xc = x - mu
    return xc * lax.rsqrt(jnp.mean(xc * xc, axis=-1, keepdims=True) + eps)


def _silu(x):
    return x * jax.nn.sigmoid(x)


def _dot(a, b):
    return jnp.dot(a, b, preferred_element_type=F32)


def _dot_nt(a, b, precision=None):
    return lax.dot_general(a, b, (((1,), (1,)), ((), ())), precision=precision,
                           preferred_element_type=F32)


def _dot_tn(a, b):
    return lax.dot_general(a, b, (((0,), (0,)), ((), ())), preferred_element_type=F32)


def _forget_gate(z, lb):
    f = lb + (1.0 - lb) * jax.nn.sigmoid(z)
    return 1.0 - f, jnp.maximum(jnp.log(f), LOG_F_MIN)


def _chunk_scan(g, reverse):
    n = g.shape[0]
    pos = lax.broadcasted_iota(jnp.int32, g.shape, 0) % HG_CHUNK
    step = 1
    while step < HG_CHUNK:
        if reverse:
            g = g + jnp.where(pos < HG_CHUNK - step, pltpu.roll(g, n - step, axis=0), 0.0)
        else:
            g = g + jnp.where(pos >= step, pltpu.roll(g, step, axis=0), 0.0)
        step *= 2
    return g


def _ada_kernel(c_ref, w_ref, b_ref, o_ref):
    c = c_ref[...]
    o_ref[...] = jnp.dot(_silu(c), w_ref[...], precision=HIGHEST,
                         preferred_element_type=F32) + b_ref[...]


def _ada(cc, w, b):
    rows, d = cc.shape
    cols = w.shape[1]
    tn = 512
    return pl.pallas_call(
        _ada_kernel,
        grid=(cols // tn,),
        in_specs=[pl.BlockSpec((rows, d), lambda j: (0, 0)),
                  pl.BlockSpec((d, tn), lambda j: (0, j)),
                  pl.BlockSpec((1, tn), lambda j: (0, j))],
        out_specs=pl.BlockSpec((rows, tn), lambda j: (0, j)),
        out_shape=jax.ShapeDtypeStruct((rows, cols), F32),
        compiler_params=_params("arbitrary"),
        name="adaln",
    )(cc, w, b)


_C_Q, _C_FF, _C_FB, _C_I, _C_G = 0, 512, 1024, 1536, 2048
_C_AQ, _C_AQR = 2560, 3072
_C_AK, _C_AKR, _C_AKS, _C_AKSR = 3584, 3712, 3840, 3968
_C_AV, _C_AVS = 4096, 4224
_C_TOTAL = 4352


def _inproj_kernel(x_ref, sh_ref, sc_ref, w_ref, lbf_ref, lbb_ref, cos_ref, sin_ref,
                   q_ref, kf_ref, gf_ref, kb_ref, gb_ref, v_ref, zg_ref,
                   aq_ref, ak_ref, aks_ref, av_ref, avs_ref):
    h = (_ln_rows(x_ref[0], NORM_EPS) * (1.0 + sc_ref[0]) + sh_ref[0]).astype(BF16)

    def proj(lo, n):
        return _dot(h, w_ref[:, lo:lo + n])

    q_ref[0] = proj(_C_Q, HG_WIDTH).astype(BF16)
    k, g = _forget_gate(proj(_C_FF, HG_WIDTH), lbf_ref[...])
    kf_ref[0] = k.astype(BF16)
    gf_ref[0] = _chunk_scan(g, False)
    k, g = _forget_gate(proj(_C_FB, HG_WIDTH), lbb_ref[...])
    kb_ref[0] = k.astype(BF16)
    gb_ref[0] = _chunk_scan(g, True)
    v_ref[0] = proj(_C_I, HG_WIDTH).astype(BF16)
    zg_ref[0] = proj(_C_G, HG_WIDTH).astype(BF16)
    cos = cos_ref[...]
    sin = sin_ref[...]
    cos4 = jnp.concatenate([cos] * (ATT_WIDTH // LANES), axis=1)
    sin4 = jnp.concatenate([sin] * (ATT_WIDTH // LANES), axis=1)
    scale = HEAD_DIM ** -0.5
    aq_ref[0] = ((proj(_C_AQ, ATT_WIDTH) * cos4 + proj(_C_AQR, ATT_WIDTH) * sin4) * scale).astype(BF16)
    ak_ref[0] = (proj(_C_AK, KV_WIDTH) * cos + proj(_C_AKR, KV_WIDTH) * sin).astype(BF16)
    aks_ref[0] = (proj(_C_AKS, KV_WIDTH) * cos + proj(_C_AKSR, KV_WIDTH) * sin).astype(BF16)
    av_ref[0] = proj(_C_AV, KV_WIDTH).astype(BF16)
    avs_ref[0] = proj(_C_AVS, KV_WIDTH).astype(BF16)


def _inproj(x, sh, sc, w, lbf, lbb, cos, sin, tm):
    b, n, d = x.shape
    row = lambda bi, i: (bi, i, 0)
    per_b = lambda bi, i: (bi, 0, 0)
    const = lambda bi, i: (0, 0)
    tab = lambda bi, i: (i, 0)

    def out(width, dtype):
        return jax.ShapeDtypeStruct((b, n, width), dtype), pl.BlockSpec((1, tm, width), row)

    outs = [out(HG_WIDTH, BF16), out(HG_WIDTH, BF16), out(HG_WIDTH, F32), out(HG_WIDTH, BF16),
            out(HG_WIDTH, F32), out(HG_WIDTH, BF16), out(HG_WIDTH, BF16),
            out(ATT_WIDTH, BF16), out(KV_WIDTH, BF16), out(KV_WIDTH, BF16),
            out(KV_WIDTH, BF16), out(KV_WIDTH, BF16)]
    return pl.pallas_call(
        _inproj_kernel,
        grid=(b, n // tm),
        in_specs=[pl.BlockSpec((1, tm, d), row),
                  pl.BlockSpec((1, 1, d), per_b), pl.BlockSpec((1, 1, d), per_b),
                  pl.BlockSpec(w.shape, const),
                  pl.BlockSpec((1, HG_WIDTH), const), pl.BlockSpec((1, HG_WIDTH), const),
                  pl.BlockSpec((tm, LANES), tab), pl.BlockSpec((tm, LANES), tab)],
        out_specs=[o[1] for o in outs],
        out_shape=[o[0] for o in outs],
        compiler_params=_params("arbitrary", "arbitrary"),
        name="latent_inproj",
    )(x, sh, sc, w, lbf, lbb, cos, sin)


_X_FF, _X_FB, _X_I, _X_AK, _X_AKS, _X_AV, _X_AVS, _X_TOTAL = 0, 512, 1024, 1536, 1664, 1792, 1920, 2048


def _ctx_kernel(c_ref, sh_ref, sc_ref, w_ref, lbf_ref, lbb_ref,
                k_ref, ks_ref, v_ref, vs_ref, sf_ref, sb_ref):
    h = (_ln_rows(c_ref[0], NORM_EPS) * (1.0 + sc_ref[...]) + sh_ref[...]).astype(BF16)

    def proj(lo, n):
        return _dot(h, w_ref[:, lo:lo + n])

    k_ref[0] = proj(_X_AK, KV_WIDTH).astype(BF16)
    ks_ref[0] = proj(_X_AKS, KV_WIDTH).astype(BF16)
    v_ref[0] = proj(_X_AV, KV_WIDTH).astype(BF16)
    vs_ref[0] = proj(_X_AVS, KV_WIDTH).astype(BF16)

    kf, gf = _forget_gate(proj(_X_FF, HG_WIDTH), lbf_ref[...])
    kb, gb = _forget_gate(proj(_X_FB, HG_WIDTH), lbb_ref[...])
    vi = proj(_X_I, HG_WIDTH).astype(BF16)
    n = h.shape[0]
    r = lax.broadcasted_iota(jnp.int32, (n, n), 0)
    c = lax.broadcasted_iota(jnp.int32, (n, n), 1)
    bf = jnp.dot((c <= r).astype(F32), gf, precision=HIGHEST, preferred_element_type=F32)
    bb = jnp.dot((c >= r).astype(F32), gb, precision=HIGHEST, preferred_element_type=F32)
    kdf = (kf * jnp.exp(bf[n - 1:n] - bf)).astype(BF16)
    kdb = (kb * jnp.exp(bb[0:1] - bb)).astype(BF16)
    for hd in range(HG_HEADS):
        sl = slice(hd * HG_DIM, (hd + 1) * HG_DIM)
        sf_ref[0, hd] = _dot_tn(vi[:, sl], kdf[:, sl])
        sb_ref[0, hd] = _dot_tn(vi[:, sl], kdb[:, sl])


def _ctx(ctx, sh, sc, w, lbf, lbb):
    b, n, d = ctx.shape
    per_b = lambda bi: (bi, 0, 0)
    const = lambda bi: (0, 0)
    kv = (jax.ShapeDtypeStruct((b, n, KV_WIDTH), BF16), pl.BlockSpec((1, n, KV_WIDTH), per_b))
    st = (jax.ShapeDtypeStruct((b, HG_HEADS, HG_DIM, HG_DIM), F32),
          pl.BlockSpec((1, HG_HEADS, HG_DIM, HG_DIM), lambda bi: (bi, 0, 0, 0)))
    outs = [kv, kv, kv, kv, st, st]
    return pl.pallas_call(
        _ctx_kernel,
        grid=(b,),
        in_specs=[pl.BlockSpec((1, n, d), per_b),
                  pl.BlockSpec((1, d), const), pl.BlockSpec((1, d), const),
                  pl.BlockSpec(w.shape, const),
                  pl.BlockSpec((1, HG_WIDTH), const), pl.BlockSpec((1, HG_WIDTH), const)],
        out_specs=[o[1] for o in outs],
        out_shape=[o[0] for o in outs],
        compiler_params=_params("arbitrary"),
        name="context_side",
    )(ctx, sh, sc, w, lbf, lbb)


def _hgrn_chunk(q, k, v, b, st, reverse):
    cs, us = HG_CHUNK, HG_SUB
    ns = cs // us
    last = 0 if reverse else cs - 1
    b_last = b[last:last + 1]
    qf = q.astype(F32)
    kf = k.astype(F32)
    q_ref_rows, k_blocks = [], []
    for s in range(ns):
        if reverse:
            keys = slice(cs - us * (s + 1), cs)
            ref = b[cs - us * s:cs - us * s + 1] if s > 0 else jnp.zeros_like(b_last)
        else:
            keys = slice(0, us * (s + 1))
            ref = b[us * s - 1:us * s] if s > 0 else jnp.zeros_like(b_last)
        q_ref_rows.append(jnp.broadcast_to(ref, (us, HG_DIM)))
        kh = (kf[keys] * jnp.exp(ref - b[keys])).astype(BF16)
        pad = jnp.zeros((cs - us * (s + 1), HG_DIM), BF16)
        if pad.shape[0]:
            kh = jnp.concatenate([pad, kh] if reverse else [kh, pad], axis=0)
        k_blocks.append(kh)
    q_ref = jnp.concatenate(q_ref_rows[::-1] if reverse else q_ref_rows, axis=0)
    qh = qf * jnp.exp(b - q_ref)
    sub = lax.broadcasted_iota(jnp.int32, (cs, HG_DIM), 0) // us
    if reverse:
        sub = ns - 1 - sub
    q_cat = jnp.concatenate([jnp.where(sub == s, qh, 0.0).astype(BF16) for s in range(ns)], axis=1)
    att = _dot_nt(q_cat, jnp.concatenate(k_blocks, axis=1))
    ri = lax.broadcasted_iota(jnp.int32, (cs, cs), 0)
    ci = lax.broadcasted_iota(jnp.int32, (cs, cs), 1)
    att = jnp.where((ci >= ri) if reverse else (ci <= ri), att, 0.0)
    o = _dot(att.astype(BF16), v) + _dot_nt((qf * jnp.exp(b)).astype(BF16), st.astype(BF16))
    kdec = (kf * jnp.exp(b_last - b)).astype(BF16)
    st_new = st * jnp.exp(b_last) + _dot_tn(v, kdec)
    return o, st_new


def _hgrn_kernel(q_ref, kf_ref, gf_ref, kb_ref, gb_ref, v_ref, zg_ref, s0f_ref, s0b_ref, nw_ref,
                 y_ref, of_ref, ob_ref):
    n = q_ref.shape[1]
    cs = HG_CHUNK
    nc = n // cs

    def body(i, carry):
        sf, sb = carry
        fwd = pl.ds(pl.multiple_of(i * cs, cs), cs)
        bwd = pl.ds(pl.multiple_of((nc - 1 - i) * cs, cs), cs)
        o, sf = _hgrn_chunk(q_ref[0, fwd, :], kf_ref[0, fwd, :], v_ref[0, fwd, :], gf_ref[0, fwd, :],
                            sf, False)
        of_ref[fwd, :] = o
        o, sb = _hgrn_chunk(q_ref[0, bwd, :], kb_ref[0, bwd, :], v_ref[0, bwd, :], gb_ref[0, bwd, :],
                            sb, True)
        ob_ref[bwd, :] = o
        return sf, sb

    lax.fori_loop(0, nc, body, (s0f_ref[0, 0], s0b_ref[0, 0]), unroll=4)

    rb = min(n, 512)

    def readout(j, carry):
        sl = pl.ds(pl.multiple_of(j * rb, rb), rb)
        o = of_ref[sl, :] + ob_ref[sl, :]
        o = o * lax.rsqrt(jnp.mean(o * o, axis=-1, keepdims=True) + NORM_EPS) * nw_ref[...]
        y_ref[0, sl, :] = (o * _silu(zg_ref[0, sl, :].astype(F32))).astype(BF16)
        return carry

    lax.fori_loop(0, n // rb, readout, 0)


def _hgrn(q, kf, gf, kb, gb, v, zg, s0f, s0b, norm_w):
    b, n, _ = q.shape
    head = lambda bi, hi: (bi, 0, hi)
    st = lambda bi, hi: (bi, hi, 0, 0)
    seq = pl.BlockSpec((1, n, HG_DIM), head)
    state = pl.BlockSpec((1, 1, HG_DIM, HG_DIM), st)
    return pl.pallas_call(
        _hgrn_kernel,
        grid=(b, HG_HEADS),
        in_specs=[seq, seq, seq, seq, seq, seq, seq, state, state,
                  pl.BlockSpec((1, HG_DIM), lambda bi, hi: (0, hi))],
        out_specs=seq,
        out_shape=jax.ShapeDtypeStruct((b, n, HG_WIDTH), BF16),
        scratch_shapes=[pltpu.VMEM((n, HG_DIM), F32), pltpu.VMEM((n, HG_DIM), F32)],
        compiler_params=_params("arbitrary", "arbitrary"),
        name="hgrn2",
    )(q, kf, gf, kb, gb, v, zg, s0f, s0b, norm_w)


def _attn_kernel(sink_ref, q_ref, kp_ref, kc_ref, kn_ref, ksp_ref, ksc_ref, ksn_ref,
                 vp_ref, vc_ref, vn_ref, vsp_ref, vsc_ref, vsn_ref,
                 xk_ref, xks_ref, xv_ref, xvs_ref, y_ref):
    i = pl.program_id(1)
    nb = pl.num_programs(1)
    low = lax.broadcasted_iota(jnp.int32, (1, LANES), 1) < HEAD_DIM
    ctx_len = xk_ref.shape[1]
    rows = 2 * BAND
    ri = lax.broadcasted_iota(jnp.int32, (rows, BAND), 0) % BAND
    ci = lax.broadcasted_iota(jnp.int32, (rows, BAND), 1)
    ok_prev = (ci >= ri) & (i > 0)
    ok_next = (ci <= ri) & (i < nb - 1)
    p0, c0, n0 = ctx_len, ctx_len + BAND, ctx_len + 2 * BAND
    top = lax.broadcasted_iota(jnp.int32, (rows, 1), 0) < BAND

    def keys_of(refs, keep_low):
        x = jnp.concatenate([r[0] for r in refs], axis=0)
        return jnp.where(low if keep_low else ~low, x, jnp.zeros_like(x))

    group = Q_HEADS // KV_HEADS
    k_plain, k_swap = (xk_ref, kp_ref, kc_ref, kn_ref), (xks_ref, ksp_ref, ksc_ref, ksn_ref)
    v_plain, v_swap = (xv_ref, vp_ref, vc_ref, vn_ref), (xvs_ref, vsp_ref, vsc_ref, vsn_ref)
    for kvh in range(KV_HEADS):
        plain_low = kvh == 0
        tile = kvh * (group // 2)
        q = jnp.concatenate([q_ref[0, :, tile * LANES:(tile + 1) * LANES],
                             q_ref[0, :, (tile + 1) * LANES:(tile + 2) * LANES]], axis=0)
        acc = jnp.zeros((rows, LANES), F32)
        for sub in range(2):
            in_low = sub == 0
            use_plain = plain_low == in_low
            k = keys_of(k_plain if use_plain else k_swap, in_low)
            v = keys_of(v_plain if use_plain else v_swap, in_low)
            sink = jnp.where(top, sink_ref[group * kvh + sub], sink_ref[group * kvh + 2 + sub])
            s = _dot_nt(q, k)
            s = jnp.concatenate([s[:, :p0], jnp.where(ok_prev, s[:, p0:c0], -jnp.inf), s[:, c0:n0],
                                 jnp.where(ok_next, s[:, n0:], -jnp.inf)], axis=1)
            m = jnp.maximum(jnp.max(s, axis=1, keepdims=True), sink)
            e = jnp.exp(s - m)
            denom = jnp.sum(e, axis=1, keepdims=True) + jnp.exp(sink - m)
            acc = acc + _dot(e.astype(BF16), v) / denom
        y_ref[0, :, tile * LANES:(tile + 1) * LANES] = acc[:BAND].astype(BF16)
        y_ref[0, :, (tile + 1) * LANES:(tile + 2) * LANES] = acc[BAND:].astype(BF16)


def _attn(sink, aq, ak, aks, av, avs, xk, xks, xv, xvs):
    b, n, _ = aq.shape
    nb = n // BAND
    cur = lambda bi, i: (bi, i, 0)
    prev = lambda bi, i: (bi, jnp.maximum(i - 1, 0), 0)
    nxt = lambda bi, i: (bi, jnp.minimum(i + 1, nb - 1), 0)
    per_b = lambda bi, i: (bi, 0, 0)
    kv = lambda f: pl.BlockSpec((1, BAND, KV_WIDTH), f)
    cx = pl.BlockSpec((1, xk.shape[1], KV_WIDTH), per_b)
    return pl.pallas_call(
        _attn_kernel,
        grid=(b, nb),
        in_specs=[pl.BlockSpec(memory_space=pltpu.SMEM),
                  pl.BlockSpec((1, BAND, ATT_WIDTH), cur),
                  kv(prev), kv(cur), kv(nxt), kv(prev), kv(cur), kv(nxt),
                  kv(prev), kv(cur), kv(nxt), kv(prev), kv(cur), kv(nxt),
                  cx, cx, cx, cx],
        out_specs=pl.BlockSpec((1, BAND, ATT_WIDTH), cur),
        out_shape=jax.ShapeDtypeStruct((b, n, ATT_WIDTH), BF16),
        compiler_params=_params("arbitrary", "arbitrary"),
        name="window_attn",
    )(sink, aq, ak, ak, ak, aks, aks, aks, av, av, av, avs, avs, avs, xk, xks, xv, xvs)


def _route(hf, wr_t, bias):
    tm = hf.shape[0]
    ne = wr_t.shape[0]
    h_hi = hf.astype(BF16)
    h_lo = (hf - h_hi.astype(F32)).astype(BF16)
    w_hi = wr_t.astype(BF16)
    w_lo = (wr_t - w_hi.astype(F32)).astype(BF16)
    first = _dot_nt(jnp.concatenate([w_hi, w_lo], axis=0), h_hi)
    scores = jax.nn.sigmoid(first[:ne] + first[ne:] + _dot_nt(w_hi, h_lo))
    sel = scores + bias
    grp = sel.reshape(N_GROUPS, GROUP_SIZE, tm)
    j = lax.broadcasted_iota(jnp.int32, grp.shape, 1)
    m1 = jnp.max(grp, axis=1, keepdims=True)
    first = jnp.min(jnp.where(grp == m1, j, GROUP_SIZE), axis=1, keepdims=True)
    m2 = jnp.max(jnp.where(j == first, -jnp.inf, grp), axis=1, keepdims=True)
    gs = (m1 + m2).reshape(N_GROUPS, tm)
    gi = lax.broadcasted_iota(jnp.int32, gs.shape, 0)
    rank = jnp.zeros(gs.shape, jnp.int32)
    for g in range(N_GROUPS):
        other = gs[g:g + 1]
        rank = rank + ((other > gs) | ((other == gs) & (g < gi))).astype(jnp.int32)
    gsel = rank < TOPK_GROUPS
    emask = jnp.broadcast_to(gsel[:, None, :], grp.shape).reshape(N_EXPERTS, tm)
    cand = jnp.where(emask, sel, -jnp.inf)
    ei = lax.broadcasted_iota(jnp.int32, cand.shape, 0)
    chosen = jnp.zeros(cand.shape, jnp.bool_)
    for _ in range(TOP_K):
        best = jnp.max(cand, axis=0, keepdims=True)
        first = jnp.min(jnp.where(cand == best, ei, N_EXPERTS), axis=0, keepdims=True)
        hit = ei == first
        chosen = chosen | hit
        cand = jnp.where(hit, -jnp.inf, cand)
    w = jnp.where(chosen, scores, 0.0)
    return w / jnp.sum(w, axis=0, keepdims=True) * ROUTED_SCALE, jnp.where(chosen, 1.0, 0.0)


def _sort_rows(chosen, gates_t):
    ne, ts = chosen.shape
    sel = chosen.astype(BF16)
    chosen = chosen > 0.5
    r = lax.broadcasted_iota(jnp.int32, (ts, ts), 0)
    c = lax.broadcasted_iota(jnp.int32, (ts, ts), 1)
    seen = _dot(sel, jnp.where(r <= c, 1.0, 0.0).astype(BF16))
    total = _dot(sel, jnp.ones((ts, ts), BF16))
    padded = jnp.floor((total + (SORT_UNIT - 1)) * (1.0 / SORT_UNIT)) * SORT_UNIT
    er = lax.broadcasted_iota(jnp.int32, (ne, ne), 0)
    ec = lax.broadcasted_iota(jnp.int32, (ne, ne), 1)
    before = jnp.where(ec < er, 1.0, 0.0).astype(BF16)
    start = _dot(before, padded.astype(BF16))
    choice = _dot(before, sel)
    row = start + seen - 1.0
    dest, gate = [], []
    for k in range(TOP_K):
        mk = chosen & (choice == k)
        dest.append(jnp.sum(jnp.where(mk, row, 0.0), axis=0, keepdims=True))
        gate.append(jnp.sum(jnp.where(mk, gates_t, 0.0), axis=0, keepdims=True))
    sel_pad = jnp.concatenate([sel, jnp.zeros((LANES - ne, ts), BF16)], axis=0)
    counts = _dot_nt(jnp.ones((8, ts), BF16), sel_pad)
    dest = jnp.concatenate(dest, axis=0)
    pad = jnp.zeros((LANES - TOP_K, ts), F32)
    dest_cols = jnp.concatenate([dest, pad], axis=0).T
    gate_cols = jnp.concatenate(gate + [pad], axis=0).T
    return dest.astype(jnp.int32), dest_cols.astype(jnp.int32), gate_cols, counts


def _outproj_kernel(x_ref, yh_ref, ya_ref, w_ref, g1_ref, sh_ref, sc_ref, lw_ref, lb_ref,
                    wr_ref, rb_ref, x1_ref, h2_ref, dest_ref, destc_ref, gatec_ref, cnt_ref):
    y = _dot(yh_ref[0], w_ref[:HG_WIDTH, :]) + _dot(ya_ref[0], w_ref[HG_WIDTH:, :])
    x1 = _ln_rows(ALPHA * x_ref[0] + g1_ref[0] * y, LN_EPS) * lw_ref[...] + lb_ref[...]
    x1_ref[0] = x1
    hf = _ln_rows(x1, NORM_EPS) * (1.0 + sc_ref[0]) + sh_ref[0]
    h2_ref[0] = hf.astype(BF16)
    gates_t, chosen = _route(hf, wr_ref[...], rb_ref[...])
    for s in range(hf.shape[0] // SORT_TILE):
        sl = slice(s * SORT_TILE, (s + 1) * SORT_TILE)
        dest_ref[s], destc_ref[s], gatec_ref[s], cnt_ref[s] = _sort_rows(chosen[:, sl], gates_t[:, sl])


def _outproj(x, yh, ya, w, g1, sh2, sc2, lw, lb, wr_t, rbias, tm):
    b, n, d = x.shape
    row = lambda bi, i: (bi, i, 0)
    per_b = lambda bi, i: (bi, 0, 0)
    const = lambda bi, i: (0, 0)
    nt = b * n // SORT_TILE
    per_step = tm // SORT_TILE
    tiles = lambda bi, i: (bi * (n // tm) + i, 0, 0)
    return pl.pallas_call(
        _outproj_kernel,
        grid=(b, n // tm),
        in_specs=[pl.BlockSpec((1, tm, d), row),
                  pl.BlockSpec((1, tm, HG_WIDTH), row), pl.BlockSpec((1, tm, ATT_WIDTH), row),
                  pl.BlockSpec(w.shape, const),
                  pl.BlockSpec((1, 1, d), per_b), pl.BlockSpec((1, 1, d), per_b),
                  pl.BlockSpec((1, 1, d), per_b),
                  pl.BlockSpec((1, d), const), pl.BlockSpec((1, d), const),
                  pl.BlockSpec(wr_t.shape, const), pl.BlockSpec(rbias.shape, const)],
        out_specs=[pl.BlockSpec((1, tm, d), row), pl.BlockSpec((1, tm, d), row),
                   pl.BlockSpec((per_step, TOP_K, SORT_TILE), tiles),
                   pl.BlockSpec((per_step, SORT_TILE, LANES), tiles),
                   pl.BlockSpec((per_step, SORT_TILE, LANES), tiles),
                   pl.BlockSpec((per_step, 8, LANES), tiles)],
        out_shape=[jax.ShapeDtypeStruct((b, n, d), F32), jax.ShapeDtypeStruct((b, n, d), BF16),
                   jax.ShapeDtypeStruct((nt, TOP_K, SORT_TILE), jnp.int32),
                   jax.ShapeDtypeStruct((nt, SORT_TILE, LANES), jnp.int32),
                   jax.ShapeDtypeStruct((nt, SORT_TILE, LANES), F32),
                   jax.ShapeDtypeStruct((nt, 8, LANES), F32)],
        compiler_params=_params("arbitrary", "arbitrary"),
        name="outproj_ln_router",
    )(x, yh, ya, w, g1, sh2, sc2, lw, lb, wr_t, rbias)


def _moe_layout(counts, max_tiles):
    n_pad = (counts + (SORT_UNIT - 1)) // SORT_UNIT * SORT_UNIT
    total = jnp.sum(n_pad, axis=0)
    region = (total + (EXPERT_TILE - 1)) // EXPERT_TILE * EXPERT_TILE
    base = jnp.cumsum(region) - region
    chunk_row = base[None, :] + jnp.cumsum(n_pad, axis=0) - n_pad
    tile_end = jnp.cumsum(region // EXPERT_TILE)
    n_used = tile_end[-1]
    j = jnp.minimum(jnp.arange(max_tiles, dtype=jnp.int32), n_used - 1)
    tile_expert = jnp.sum((tile_end[None, :] <= j[:, None]).astype(jnp.int32), axis=1)
    units = n_pad // SORT_UNIT
    local_row = jnp.cumsum(n_pad, axis=1) - n_pad
    rare = jnp.ones(units.shape, jnp.bool_)
    counts_out, lists = [], []
    for size, width in PIECE_LISTS[:-1]:
        mask = units == size
        rare = rare & ~mask
        slot = jnp.cumsum(mask, axis=1) - 1
        pick = mask[:, None, :] & (slot[:, None, :] == jnp.arange(width)[None, :, None])
        counts_out.append(jnp.sum(mask, axis=1))
        lists += [jnp.sum(jnp.where(pick, v[:, None, :], 0), axis=2) for v in (local_row, chunk_row)]
    single = jnp.where(rare, units, 0)
    single_end = jnp.cumsum(single, axis=1)
    u = jnp.arange(LOCAL_UNITS)[None, :, None]
    first = (single_end - single)[:, None, :]
    pick = (u >= first) & (u < single_end[:, None, :])
    counts_out.append(single_end[:, -1])
    lists += [jnp.sum(jnp.where(pick, v[:, None, :] + (u - first) * SORT_UNIT, 0), axis=2)
              for v in (local_row, chunk_row)]
    header = jnp.stack(counts_out + [jnp.sum(n_pad, axis=1)], axis=1)
    header = jnp.pad(header, ((0, 0), (0, TABLE_HEADER - header.shape[1])))
    table = jnp.concatenate([header] + lists, axis=1)
    table = jnp.pad(table, ((0, 0), (0, TABLE_WIDTH - table.shape[1]))).astype(jnp.int32)[:, None, :]
    tails = jnp.concatenate([base + total, (region - total) // SORT_UNIT]).astype(jnp.int32)
    return table, tails, tile_expert, n_used.astype(jnp.int32)[None]


def _row_cases(rows_used, fn):
    @pl.when(rows_used <= USUAL_ROWS)
    def _():
        fn(USUAL_ROWS)

    @pl.when(rows_used > USUAL_ROWS)
    def _():
        fn(LOCAL_ROWS)


def _unit(ref, row, units=1):
    return ref.at[pl.ds(pl.multiple_of(row, SORT_UNIT), units * SORT_UNIT), :]


def _for_each_piece(table_ref, fn):
    offset = TABLE_HEADER
    for c, (size, width) in enumerate(PIECE_LISTS):
        def body(j, carry, offset=offset, size=size, width=width):
            fn(table_ref[0, 0, offset + j], table_ref[0, 0, offset + width + j], size)
            return carry

        lax.fori_loop(0, table_ref[0, 0, c], body, 0)
        offset += 2 * width


def _await_units(units, wait_fn):
    def many(u, carry):
        wait_fn(WAIT_UNITS)
        return carry

    def single(u, carry):
        wait_fn(1)
        return carry

    lax.fori_loop(0, units // WAIT_UNITS, many, 0)
    lax.fori_loop(0, units % WAIT_UNITS, single, 0)


def _dispatch_kernel(chunks_ref, tails_ref, h_ref, dest_ref, xs_ref, buf_ref, zero_ref, pending_ref, sem):
    i = pl.program_id(0)
    slot = i % 2
    buf = buf_ref.at[slot]

    def drain(s):
        _await_units(pending_ref[s], lambda n: pltpu.make_async_copy(
            _unit(buf_ref.at[s], 0, n), _unit(xs_ref, 0, n), sem.at[s]).wait())

    @pl.when(i >= 2)
    def _():
        drain(slot)

    def permute(rows):
        r = lax.broadcasted_iota(jnp.int32, (rows, SORT_TILE), 0).astype(jnp.int16)
        p = jnp.zeros((rows, SORT_TILE), BF16)
        one = jnp.ones((rows, SORT_TILE), BF16)
        for k in range(TOP_K):
            hit = r == dest_ref[0, k:k + 1, :].astype(jnp.int16)
            p = jnp.where(hit, one, p)
        buf[0:rows, :] = _dot(p, h_ref[...]).astype(BF16)

    rows_used = chunks_ref[0, 0, TABLE_ROWS_USED]
    _row_cases(rows_used, permute)
    _for_each_piece(chunks_ref, lambda loc, glob, n: pltpu.make_async_copy(
        _unit(buf, loc, n), _unit(xs_ref, glob, n), sem.at[slot]).start())
    pending_ref[slot] = rows_used // SORT_UNIT

    @pl.when(i == pl.num_programs(0) - 1)
    def _():
        zero_ref[...] = jnp.zeros_like(zero_ref)

        def per_expert(e, total):
            n = tails_ref[N_EXPERTS + e]
            start = tails_ref[e]

            def per_unit(u, carry):
                pltpu.make_async_copy(zero_ref, _unit(xs_ref, start + u * SORT_UNIT), sem.at[2]).start()
                return carry

            lax.fori_loop(0, n, per_unit, 0)
            return total + n

        total = lax.fori_loop(0, N_EXPERTS, per_expert, 0)

        def drain_zero(u, carry):
            pltpu.make_async_copy(zero_ref, _unit(xs_ref, 0), sem.at[2]).wait()
            return carry

        lax.fori_loop(0, total, drain_zero, 0)

        @pl.when(i >= 1)
        def _():
            drain(1 - slot)

        drain(slot)


def _dispatch(chunks, tails, h2, dest, max_rows):
    t, d = h2.shape
    nt = t // SORT_TILE
    return pl.pallas_call(
        _dispatch_kernel,
        grid=(nt,),
        in_specs=[pl.BlockSpec((1, 1, TABLE_WIDTH), lambda i: (i, 0, 0), memory_space=pltpu.SMEM),
                  pl.BlockSpec(memory_space=pltpu.SMEM),
                  pl.BlockSpec((SORT_TILE, d), lambda i: (i, 0)),
                  pl.BlockSpec((1, TOP_K, SORT_TILE), lambda i: (i, 0, 0))],
        out_specs=pl.BlockSpec(memory_space=pl.ANY),
        out_shape=jax.ShapeDtypeStruct((max_rows, d), BF16),
        scratch_shapes=[pltpu.VMEM((2, LOCAL_ROWS, d), BF16), pltpu.VMEM((SORT_UNIT, d), BF16),
                        pltpu.SMEM((2,), jnp.int32), pltpu.SemaphoreType.DMA((3,))],
        compiler_params=_params("arbitrary"),
        name="moe_dispatch",
    )(chunks, tails, h2, dest)


def _expert_kernel(te_ref, nu_ref, xs_ref, wg_ref, wu_ref, wd_ref, ys_ref):
    @pl.when(pl.program_id(0) < nu_ref[0])
    def _():
        x = xs_ref[...]
        a = _silu(_dot(x, wg_ref[0].astype(BF16))) * _dot(x, wu_ref[0].astype(BF16))
        ys_ref[...] = _dot(a.astype(BF16), wd_ref[0].astype(BF16)).astype(BF16)


def _experts(tile_expert, n_used, xs, wg, wu, wd):
    rows, d = xs.shape
    ff = wg.shape[2]
    row = lambda j, te, nu: (jnp.minimum(j, jnp.maximum(nu[0] - 1, 0)), 0)
    exp = lambda j, te, nu: (te[j], 0, 0)
    return pl.pallas_call(
        _expert_kernel,
        grid_spec=pltpu.PrefetchScalarGridSpec(
            num_scalar_prefetch=2,
            grid=(rows // EXPERT_TILE,),
            in_specs=[pl.BlockSpec((EXPERT_TILE, d), row),
                      pl.BlockSpec((1, d, ff), exp), pl.BlockSpec((1, d, ff), exp),
                      pl.BlockSpec((1, ff, d), exp)],
            out_specs=pl.BlockSpec((EXPERT_TILE, d), row)),
        out_shape=jax.ShapeDtypeStruct((rows, d), BF16),
        compiler_params=_params("arbitrary"),
        name="moe_experts",
    )(tile_expert, n_used, xs, wg, wu, wd)


def _combine_kernel(chunks_ref, next_ref, ys_ref, dest_ref, gate_ref, h_ref, x1_ref, g2_ref, lw_ref, lb_ref,
                    swg_ref, swu_ref, swd_ref, o_ref, buf_ref, sem):
    i = pl.program_id(0)
    slot = i % 2

    def fetch(meta_ref, s):
        _for_each_piece(meta_ref, lambda loc, glob, n: pltpu.make_async_copy(
            _unit(ys_ref, glob, n), _unit(buf_ref.at[s], loc, n), sem.at[s]).start())

    @pl.when(i == 0)
    def _():
        buf_ref[...] = jnp.zeros_like(buf_ref)
        fetch(chunks_ref, 0)

    @pl.when(i + 1 < pl.num_programs(0))
    def _():
        fetch(next_ref, 1 - slot)

    h = h_ref[...]
    a = _silu(_dot(h, swg_ref[...])) * _dot(h, swu_ref[...])
    shared = _dot(a.astype(BF16), swd_ref[...])
    rows_used = chunks_ref[0, 0, TABLE_ROWS_USED]
    _await_units(rows_used // SORT_UNIT, lambda n: pltpu.make_async_copy(
        _unit(ys_ref, 0, n), _unit(buf_ref.at[slot], 0, n), sem.at[slot]).wait())

    def unpermute(rows):
        lane = lax.broadcasted_iota(jnp.int32, (SORT_TILE, rows), 1).astype(jnp.int16)
        p = jnp.zeros((SORT_TILE, rows), BF16)
        for k in range(TOP_K):
            hit = lane == dest_ref[0, :, k:k + 1].astype(jnp.int16)
            p = jnp.where(hit, jnp.broadcast_to(gate_ref[0, :, k:k + 1].astype(BF16), p.shape), p)
        ffn = shared + _dot(p, buf_ref[slot, 0:rows, :])
        u = ALPHA * x1_ref[...] + g2_ref[0] * ffn
        o_ref[...] = _ln_rows(u, LN_EPS) * lw_ref[...] + lb_ref[...]

    _row_cases(rows_used, unpermute)


def _combine(chunks, ys, dest, gate, h2, x1, g2, lw, lb, swg, swu, swd, tiles_per_batch):
    t, d = h2.shape
    ff = swg.shape[1]
    nt = t // SORT_TILE
    tile = lambda i: (i, 0, 0)
    nxt = lambda i: (jnp.minimum(i + 1, nt - 1), 0, 0)
    row = lambda i: (i, 0)
    const = lambda i: (0, 0)
    return pl.pallas_call(
        _combine_kernel,
        grid=(nt,),
        in_specs=[pl.BlockSpec((1, 1, TABLE_WIDTH), tile, memory_space=pltpu.SMEM),
                  pl.BlockSpec((1, 1, TABLE_WIDTH), nxt, memory_space=pltpu.SMEM),
                  pl.BlockSpec(memory_space=pl.ANY),
                  pl.BlockSpec((1, SORT_TILE, LANES), tile), pl.BlockSpec((1, SORT_TILE, LANES), tile),
                  pl.BlockSpec((SORT_TILE, d), row), pl.BlockSpec((SORT_TILE, d), row),
                  pl.BlockSpec((1, 1, d), lambda i: (i // tiles_per_batch, 0, 0)),
                  pl.BlockSpec((1, d), const), pl.BlockSpec((1, d), const),
                  pl.BlockSpec((d, ff), const), pl.BlockSpec((d, ff), const), pl.BlockSpec((ff, d), const)],
        out_specs=pl.BlockSpec((SORT_TILE, d), row),
        out_shape=jax.ShapeDtypeStruct((t, d), F32),
        scratch_shapes=[pltpu.VMEM((2, LOCAL_ROWS, d), BF16), pltpu.SemaphoreType.DMA((2,))],
        compiler_params=_params("arbitrary"),
        name="moe_combine",
    )(chunks, chunks, ys, dest, gate, h2, x1, g2, lw, lb, swg, swu, swd)


def _moe(h2, x1, dest, dest_cols, gate_cols, counts, g2, lw, lb, wg, wu, wd, swg, swu, swd):
    b, n, d = x1.shape
    t = b * n
    nt = t // SORT_TILE
    max_rows = t * TOP_K + nt * N_EXPERTS * (SORT_UNIT - 1) + N_EXPERTS * (EXPERT_TILE - SORT_UNIT)
    max_rows = -(-max_rows // EXPERT_TILE) * EXPERT_TILE
    chunks, tails, tile_expert, n_used = _moe_layout(
        counts[:, 0, :N_EXPERTS].astype(jnp.int32), max_rows // EXPERT_TILE)
    h2 = h2.reshape(t, d)
    xs = _dispatch(chunks, tails, h2, dest, max_rows)
    ys = _experts(tile_expert, n_used, xs, wg, wu, wd)
    out = _combine(chunks, ys, dest_cols, gate_cols, h2, x1.reshape(t, d), g2, lw, lb, swg, swu, swd,
                   n // SORT_TILE)
    return out.reshape(b, n, d)


def _rotate_half_columns(w):
    d, cols = w.shape
    blk = w.reshape(d, cols // (2 * ROT_PAIRS), 2, ROT_PAIRS)
    return jnp.stack([-blk[:, :, 1], blk[:, :, 0]], axis=2).reshape(d, cols)


def _swap_kv_heads(w):
    return jnp.concatenate([w[:, HEAD_DIM:], w[:, :HEAD_DIM]], axis=1)


def _split_w_in(w_in):
    bounds = np.cumsum([HG_WIDTH] * 5 + [ATT_WIDTH, KV_WIDTH])
    return jnp.split(w_in, [int(v) for v in bounds], axis=1)


def _latent_weight(w_in):
    zq, zff, zfb, zi, zg, aq, ak, av = _split_w_in(w_in)
    aks = _swap_kv_heads(ak)
    cols = [zq, zff, zfb, zi, zg, aq, _rotate_half_columns(aq), ak, _rotate_half_columns(ak),
            aks, _rotate_half_columns(aks), av, _swap_kv_heads(av)]
    return jnp.concatenate(cols, axis=1).astype(BF16)


def _context_weight(w_in):
    _, zff, zfb, zi, _, _, ak, av = _split_w_in(w_in)
    return jnp.concatenate([zff, zfb, zi, ak, _swap_kv_heads(ak), av, _swap_kv_heads(av)],
                           axis=1).astype(BF16)


def _rope_tables(n):
    pos = jnp.arange(n)
    freqs = ROPE_BASE ** (-jnp.arange(ROT_PAIRS, dtype=F32) / ROT_PAIRS)
    ang_row = (pos // GRID_W).astype(F32)[:, None] * freqs
    ang_col = (pos % GRID_W).astype(F32)[:, None] * freqs
    ang = jnp.concatenate([ang_row, ang_row, ang_col, ang_col], axis=1)
    ang = jnp.concatenate([ang] * (LANES // HEAD_DIM), axis=1)
    return jnp.cos(ang), jnp.sin(ang)


def kernel(x, c, ctx, c_ctx, w_ada, b_ada, w_in, hg_lb_fwd, hg_lb_bwd, hg_norm_w, attn_sink, w_out, ln1_w, ln1_b, router_w, router_bias, exp_w_gate, exp_w_up, exp_w_down, shared_w_gate, shared_w_up, shared_w_down, ln2_w, ln2_b):
    b, n, d = x.shape
    layer = 0
    rows = -(-(b + 1) // 8) * 8
    cc = jnp.zeros((rows, d), F32).at[:b].set(c).at[b].set(c_ctx)
    mod = _ada(cc, w_ada[layer], b_ada[layer][None, :])
    sh1, sc1, g1, sh2, sc2, g2 = [m[:, None, :] for m in jnp.split(mod[:b], 6, axis=1)]
    csh1, csc1 = mod[b:b + 1, :d], mod[b:b + 1, d:2 * d]

    lb_f = jnp.cumsum(jax.nn.softmax(hg_lb_fwd.astype(F32), axis=0), axis=0)[layer][None, :]
    lb_b = jnp.cumsum(jax.nn.softmax(hg_lb_bwd.astype(F32), axis=0), axis=0)[layer][None, :]
    cos, sin = _rope_tables(n)

    xk, xks, xv, xvs, s0f, s0b = _ctx(ctx, csh1, csc1, _context_weight(w_in[layer]), lb_f, lb_b)
    tm = min(n, 512)
    (q, kf, gf, kb, gb, v, zg, aq, ak, aks, av, avs) = _inproj(
        x, sh1, sc1, _latent_weight(w_in[layer]), lb_f, lb_b, cos, sin, tm)
    yh = _hgrn(q, kf, gf, kb, gb, v, zg, s0f, s0b, hg_norm_w[layer][None, :])
    ya = _attn(attn_sink[layer], aq, ak, aks, av, avs, xk, xks, xv, xvs)

    x1, h2, dest, dest_cols, gate_cols, counts = _outproj(
        x, yh, ya, w_out[layer].astype(BF16), g1, sh2, sc2, ln1_w[layer][None, :], ln1_b[layer][None, :],
        router_w[layer].T, router_bias[layer][:, None], tm)
    return _moe(h2, x1, dest, dest_cols, gate_cols, counts, g2, ln2_w[layer][None, :], ln2_b[layer][None, :],
                exp_w_gate[layer], exp_w_up[layer], exp_w_down[layer], shared_w_gate[layer].astype(BF16),
                shared_w_up[layer].astype(BF16), shared_w_down[layer].astype(BF16))
```

```python
import jax
import jax.numpy as jnp
import numpy as np
from jax import lax
from jax.experimental import pallas as pl
from jax.experimental.pallas import tpu as pltpu

F32 = jnp.float32
BF16 = jnp.bfloat16
HIGHEST = lax.Precision.HIGHEST

DEPTH = 1
GRID_W = 64
HG_WIDTH = 512
HG_HEADS = 4
HG_DIM = 128
HG_CHUNK = 64
HG_SUB = 16
HG_BATCH = 4
LOG_F_MIN = -4.0
HEAD_DIM = 64
Q_HEADS = 8
KV_HEADS = 2
ATT_WIDTH = Q_HEADS * HEAD_DIM
KV_WIDTH = KV_HEADS * HEAD_DIM
BAND = 128
ROPE_BASE = 10000.0
ROT_PAIRS = HEAD_DIM // 4
N_EXPERTS = 64
TOP_K = 8
N_GROUPS = 8
GROUP_SIZE = N_EXPERTS // N_GROUPS
TOPK_GROUPS = 4
ROUTED_SCALE = 2.5
LN_EPS = 1e-5
NORM_EPS = 1e-6
ALPHA = (2.0 * DEPTH) ** 0.25

LANES = 128
BF16_SUBLANES = 16
MXU_DIM = 256

SORT_TILE = MXU_DIM
SORT_UNIT = BF16_SUBLANES
EXPERT_TILE = 1024
_LOCAL_WORST = SORT_TILE * TOP_K + N_EXPERTS * (SORT_UNIT - 1)
LOCAL_ROWS = -(-_LOCAL_WORST // MXU_DIM) * MXU_DIM
USUAL_ROWS = SORT_TILE * TOP_K + 2 * MXU_DIM
LOCAL_UNITS = LOCAL_ROWS // SORT_UNIT
COMMON_UNITS = (2, 3)
PIECE_LISTS = tuple((size, N_EXPERTS) for size in COMMON_UNITS) + ((1, LOCAL_UNITS),)
TABLE_HEADER = 8
TABLE_ROWS_USED = len(PIECE_LISTS)
TABLE_WIDTH = -(-(TABLE_HEADER + 2 * sum(width for _, width in PIECE_LISTS)) // LANES) * LANES
WAIT_UNITS = 8
V7X_VMEM_LIMIT_BYTES = 56 * 1024 * 1024


def _params(*sem):
    return pltpu.CompilerParams(dimension_semantics=sem, vmem_limit_bytes=V7X_VMEM_LIMIT_BYTES)


def _ln_rows(x, eps):
    mu = jnp.mean(x, axis=-1, keepdims=True)
    xc = x - mu
    return xc * lax.rsqrt(jnp.mean(xc * xc, axis=-1, keepdims=True) + eps)


def _silu(x):
    return x * jax.nn.sigmoid(x)


def _dot(a, b):
    return jnp.dot(a, b, preferred_element_type=F32)


def _dot_nt(a, b, precision=None):
    return lax.dot_general(a, b, (((1,), (1,)), ((), ())), precision=precision,
                           preferred_element_type=F32)


def _dot_tn(a, b):
    return lax.dot_general(a, b, (((0,), (0,)), ((), ())), preferred_element_type=F32)


def _forget_gate(z, lb):
    f = lb + (1.0 - lb) * jax.nn.sigmoid(z)
    return 1.0 - f, jnp.maximum(jnp.log(f), LOG_F_MIN)


def _chunk_scan(g, reverse):
    n = g.shape[0]
    pos = lax.broadcasted_iota(jnp.int32, g.shape, 0) % HG_CHUNK
    step = 1
    while step < HG_CHUNK:
        if reverse:
            g = g + jnp.where(pos < HG_CHUNK - step, pltpu.roll(g, n - step, axis=0), 0.0)
        else:
            g = g + jnp.where(pos >= step, pltpu.roll(g, step, axis=0), 0.0)
        step *= 2
    return g


def _ada_kernel(c_ref, w_ref, b_ref, o_ref):
    c = c_ref[...]
    o_ref[...] = jnp.dot(_silu(c), w_ref[...], precision=HIGHEST,
                         preferred_element_type=F32) + b_ref[...]


def _ada(cc, w, b):
    rows, d = cc.shape
    cols = w.shape[1]
    tn = 512
    return pl.pallas_call(
        _ada_kernel,
        grid=(cols // tn,),
        in_specs=[pl.BlockSpec((rows, d), lambda j: (0, 0)),
                  pl.BlockSpec((d, tn), lambda j: (0, j)),
                  pl.BlockSpec((1, tn), lambda j: (0, j))],
        out_specs=pl.BlockSpec((rows, tn), lambda j: (0, j)),
        out_shape=jax.ShapeDtypeStruct((rows, cols), F32),
        compiler_params=_params("arbitrary"),
        name="adaln",
    )(cc, w, b)


_C_Q, _C_FF, _C_FB, _C_I, _C_G = 0, 512, 1024, 1536, 2048
_C_AQ, _C_AK, _C_AV = 2560, 3072, 3200


def _swap_rotary_halves(x):
    first = lax.broadcasted_iota(jnp.int32, (1, LANES), 1) % (2 * ROT_PAIRS) < ROT_PAIRS
    tiles = []
    for t in range(x.shape[1] // LANES):
        xt = x[:, t * LANES:(t + 1) * LANES]
        tiles.append(jnp.where(first, pltpu.roll(xt, LANES - ROT_PAIRS, axis=1),
                               pltpu.roll(xt, ROT_PAIRS, axis=1)))
    return jnp.concatenate(tiles, axis=1)


def _inproj_kernel(x_ref, sh_ref, sc_ref, w_ref, lbf_ref, lbb_ref, cos_ref, sin_ref,
                   q_ref, kf_ref, gf_ref, kb_ref, gb_ref, v_ref, zg_ref,
                   aq_ref, ak_ref, aks_ref, av_ref, avs_ref):
    h = (_ln_rows(x_ref[0], NORM_EPS) * (1.0 + sc_ref[0]) + sh_ref[0]).astype(BF16)

    def proj(lo, n):
        return _dot(h, w_ref[:, lo:lo + n])

    zq, zff, zfb = proj(_C_Q, HG_WIDTH), proj(_C_FF, HG_WIDTH), proj(_C_FB, HG_WIDTH)
    zi, zg = proj(_C_I, HG_WIDTH), proj(_C_G, HG_WIDTH)
    aq, ak, av = proj(_C_AQ, ATT_WIDTH), proj(_C_AK, KV_WIDTH), proj(_C_AV, KV_WIDTH)

    q_ref[0] = zq.astype(BF16)
    k, g = _forget_gate(zff, lbf_ref[...])
    kf_ref[0] = k.astype(BF16)
    gf_ref[0] = _chunk_scan(g, False)
    k, g = _forget_gate(zfb, lbb_ref[...])
    kb_ref[0] = k.astype(BF16)
    gb_ref[0] = _chunk_scan(g, True)
    v_ref[0] = zi.astype(BF16)
    zg_ref[0] = zg.astype(BF16)
    cos = cos_ref[...]
    sin = sin_ref[...]
    cos4 = jnp.concatenate([cos] * (ATT_WIDTH // LANES), axis=1)
    sin4 = jnp.concatenate([sin] * (ATT_WIDTH // LANES), axis=1)
    scale = HEAD_DIM ** -0.5
    aq_ref[0] = ((aq * cos4 + _swap_rotary_halves(aq) * sin4) * scale).astype(BF16)
    ak = ak * cos + _swap_rotary_halves(ak) * sin
    ak_ref[0] = ak.astype(BF16)
    aks_ref[0] = pltpu.roll(ak, HEAD_DIM, axis=1).astype(BF16)
    av_ref[0] = av.astype(BF16)
    avs_ref[0] = pltpu.roll(av, HEAD_DIM, axis=1).astype(BF16)


def _inproj(x, sh, sc, w, lbf, lbb, cos, sin, tm):
    b, n, d = x.shape
    row = lambda bi, i: (bi, i, 0)
    per_b = lambda bi, i: (bi, 0, 0)
    const = lambda bi, i: (0, 0)
    tab = lambda bi, i: (i, 0)

    def out(width, dtype):
        return jax.ShapeDtypeStruct((b, n, width), dtype), pl.BlockSpec((1, tm, width), row)

    outs = [out(HG_WIDTH, BF16), out(HG_WIDTH, BF16), out(HG_WIDTH, F32), out(HG_WIDTH, BF16),
            out(HG_WIDTH, F32), out(HG_WIDTH, BF16), out(HG_WIDTH, BF16),
            out(ATT_WIDTH, BF16), out(KV_WIDTH, BF16), out(KV_WIDTH, BF16),
            out(KV_WIDTH, BF16), out(KV_WIDTH, BF16)]
    return pl.pallas_call(
        _inproj_kernel,
        grid=(b, n // tm),
        in_specs=[pl.BlockSpec((1, tm, d), row),
                  pl.BlockSpec((1, 1, d), per_b), pl.BlockSpec((1, 1, d), per_b),
                  pl.BlockSpec(w.shape, const),
                  pl.BlockSpec((1, HG_WIDTH), const), pl.BlockSpec((1, HG_WIDTH), const),
                  pl.BlockSpec((tm, LANES), tab), pl.BlockSpec((tm, LANES), tab)],
        out_specs=[o[1] for o in outs],
        out_shape=[o[0] for o in outs],
        compiler_params=_params("arbitrary", "arbitrary"),
        name="latent_inproj",
    )(x, sh, sc, w, lbf, lbb, cos, sin)


_X_FF, _X_FB, _X_I, _X_AK, _X_AKS, _X_AV, _X_AVS, _X_TOTAL = 0, 512, 1024, 1536, 1664, 1792, 1920, 2048


def _ctx_kernel(c_ref, sh_ref, sc_ref, w_ref, lbf_ref, lbb_ref,
                k_ref, ks_ref, v_ref, vs_ref, sf_ref, sb_ref):
    h = (_ln_rows(c_ref[0], NORM_EPS) * (1.0 + sc_ref[...]) + sh_ref[...]).astype(BF16)

    def proj(lo, n):
        return _dot(h, w_ref[:, lo:lo + n])

    k_ref[0] = proj(_X_AK, KV_WIDTH).astype(BF16)
    ks_ref[0] = proj(_X_AKS, KV_WIDTH).astype(BF16)
    v_ref[0] = proj(_X_AV, KV_WIDTH).astype(BF16)
    vs_ref[0] = proj(_X_AVS, KV_WIDTH).astype(BF16)

    kf, gf = _forget_gate(proj(_X_FF, HG_WIDTH), lbf_ref[...])
    kb, gb = _forget_gate(proj(_X_FB, HG_WIDTH), lbb_ref[...])
    vi = proj(_X_I, HG_WIDTH).astype(BF16)
    n = h.shape[0]
    r = lax.broadcasted_iota(jnp.int32, (n, n), 0)
    c = lax.broadcasted_iota(jnp.int32, (n, n), 1)
    bf = jnp.dot((c <= r).astype(F32), gf, precision=HIGHEST, preferred_element_type=F32)
    bb = jnp.dot((c >= r).astype(F32), gb, precision=HIGHEST, preferred_element_type=F32)
    kdf = (kf * jnp.exp(bf[n - 1:n] - bf)).astype(BF16)
    kdb = (kb * jnp.exp(bb[0:1] - bb)).astype(BF16)
    for hd in range(HG_HEADS):
        sl = slice(hd * HG_DIM, (hd + 1) * HG_DIM)
        sf_ref[0, hd] = _dot_tn(vi[:, sl], kdf[:, sl])
        sb_ref[0, hd] = _dot_tn(vi[:, sl], kdb[:, sl])


def _ctx(ctx, sh, sc, w, lbf, lbb):
    b, n, d = ctx.shape
    per_b = lambda bi: (bi, 0, 0)
    const = lambda bi: (0, 0)
    kv = (jax.ShapeDtypeStruct((b, n, KV_WIDTH), BF16), pl.BlockSpec((1, n, KV_WIDTH), per_b))
    st = (jax.ShapeDtypeStruct((b, HG_HEADS, HG_DIM, HG_DIM), F32),
          pl.BlockSpec((1, HG_HEADS, HG_DIM, HG_DIM), lambda bi: (bi, 0, 0, 0)))
    outs = [kv, kv, kv, kv, st, st]
    return pl.pallas_call(
        _ctx_kernel,
        grid=(b,),
        in_specs=[pl.BlockSpec((1, n, d), per_b),
                  pl.BlockSpec((1, d), const), pl.BlockSpec((1, d), const),
                  pl.BlockSpec(w.shape, const),
                  pl.BlockSpec((1, HG_WIDTH), const), pl.BlockSpec((1, HG_WIDTH), const)],
        out_specs=[o[1] for o in outs],
        out_shape=[o[0] for o in outs],
        compiler_params=_params("arbitrary"),
        name="context_side",
    )(ctx, sh, sc, w, lbf, lbb)


def _hgrn_chunk_start(q, k, v, b, reverse):
    cs, us = HG_CHUNK, HG_SUB
    ns = cs // us
    last = 0 if reverse else cs - 1
    b_last = b[last:last + 1]
    qf = q.astype(F32)
    kf = k.astype(F32)
    q_ref_rows, k_blocks = [], []
    for s in range(ns):
        if reverse:
            keys = slice(cs - us * (s + 1), cs)
            ref = b[cs - us * s:cs - us * s + 1] if s > 0 else jnp.zeros_like(b_last)
        else:
            keys = slice(0, us * (s + 1))
            ref = b[us * s - 1:us * s] if s > 0 else jnp.zeros_like(b_last)
        q_ref_rows.append(jnp.broadcast_to(ref, (us, HG_DIM)))
        kh = (kf[keys] * jnp.exp(ref - b[keys])).astype(BF16)
        pad = jnp.zeros((cs - us * (s + 1), HG_DIM), BF16)
        if pad.shape[0]:
            kh = jnp.concatenate([pad, kh] if reverse else [kh, pad], axis=0)
        k_blocks.append(kh)
    q_ref = jnp.concatenate(q_ref_rows[::-1] if reverse else q_ref_rows, axis=0)
    qh = qf * jnp.exp(b - q_ref)
    sub = lax.broadcasted_iota(jnp.int32, (cs, HG_DIM), 0) // us
    if reverse:
        sub = ns - 1 - sub
    q_cat = jnp.concatenate([jnp.where(sub == s, qh, 0.0).astype(BF16) for s in range(ns)], axis=1)
    att = _dot_nt(q_cat, jnp.concatenate(k_blocks, axis=1))
    kdec = (kf * jnp.exp(b_last - b)).astype(BF16)
    return att, (qf * jnp.exp(b)).astype(BF16), v, jnp.exp(b_last), _dot_tn(v, kdec)


def _hgrn_chunk_finish(parts, st, reverse):
    att, q_dec, v, decay, update = parts
    ri = lax.broadcasted_iota(jnp.int32, att.shape, 0)
    ci = lax.broadcasted_iota(jnp.int32, att.shape, 1)
    att = jnp.where((ci >= ri) if reverse else (ci <= ri), att, 0.0)
    o = _dot(att.astype(BF16), v) + _dot_nt(q_dec, st.astype(BF16))
    return o, st * decay + update


def _hgrn_kernel(q_ref, kf_ref, gf_ref, kb_ref, gb_ref, v_ref, zg_ref, s0f_ref, s0b_ref, nw_ref,
                 y_ref, of_ref, ob_ref):
    n = q_ref.shape[1]
    cs = HG_CHUNK
    nc = n // cs

    def body(i, carry):
        sf, sb = carry
        started = []
        for j in range(HG_BATCH):
            fwd = pl.ds(pl.multiple_of((i * HG_BATCH + j) * cs, cs), cs)
            bwd = pl.ds(pl.multiple_of((nc - 1 - i * HG_BATCH - j) * cs, cs), cs)
            started.append((
                fwd, _hgrn_chunk_start(q_ref[0, fwd, :], kf_ref[0, fwd, :], v_ref[0, fwd, :],
                                       gf_ref[0, fwd, :], False),
                bwd, _hgrn_chunk_start(q_ref[0, bwd, :], kb_ref[0, bwd, :], v_ref[0, bwd, :],
                                       gb_ref[0, bwd, :], True)))
        for fwd, parts_f, bwd, parts_b in started:
            of_ref[fwd, :], sf = _hgrn_chunk_finish(parts_f, sf, False)
            ob_ref[bwd, :], sb = _hgrn_chunk_finish(parts_b, sb, True)
        return sf, sb

    lax.fori_loop(0, nc // HG_BATCH, body, (s0f_ref[0, 0], s0b_ref[0, 0]))

    rb = min(n, 512)

    def readout(j, carry):
        sl = pl.ds(pl.multiple_of(j * rb, rb), rb)
        o = of_ref[sl, :] + ob_ref[sl, :]
        o = o * lax.rsqrt(jnp.mean(o * o, axis=-1, keepdims=True) + NORM_EPS) * nw_ref[...]
        y_ref[0, sl, :] = (o * _silu(zg_ref[0, sl, :].astype(F32))).astype(BF16)
        return carry

    lax.fori_loop(0, n // rb, readout, 0)


def _hgrn(q, kf, gf, kb, gb, v, zg, s0f, s0b, norm_w):
    b, n, _ = q.shape
    head = lambda bi, hi: (bi, 0, hi)
    st = lambda bi, hi: (bi, hi, 0, 0)
    seq = pl.BlockSpec((1, n, HG_DIM), head)
    state = pl.BlockSpec((1, 1, HG_DIM, HG_DIM), st)
    return pl.pallas_call(
        _hgrn_kernel,
        grid=(b, HG_HEADS),
        in_specs=[seq, seq, seq, seq, seq, seq, seq, state, state,
                  pl.BlockSpec((1, HG_DIM), lambda bi, hi: (0, hi))],
        out_specs=seq,
        out_shape=jax.ShapeDtypeStruct((b, n, HG_WIDTH), BF16),
        scratch_shapes=[pltpu.VMEM((n, HG_DIM), F32), pltpu.VMEM((n, HG_DIM), F32)],
        compiler_params=_params("arbitrary", "arbitrary"),
        name="hgrn2",
    )(q, kf, gf, kb, gb, v, zg, s0f, s0b, norm_w)


def _attn_kernel(sink_ref, q_ref, kp_ref, kc_ref, kn_ref, ksp_ref, ksc_ref, ksn_ref,
                 vp_ref, vc_ref, vn_ref, vsp_ref, vsc_ref, vsn_ref,
                 xk_ref, xks_ref, xv_ref, xvs_ref, y_ref):
    i = pl.program_id(1)
    nb = pl.num_programs(1)
    low = lax.broadcasted_iota(jnp.int32, (1, LANES), 1) < HEAD_DIM
    ctx_len = xk_ref.shape[1]
    rows = 2 * BAND
    ri = lax.broadcasted_iota(jnp.int32, (rows, BAND), 0) % BAND
    ci = lax.broadcasted_iota(jnp.int32, (rows, BAND), 1)
    ok_prev = (ci >= ri) & (i > 0)
    ok_next = (ci <= ri) & (i < nb - 1)
    p0, c0, n0 = ctx_len, ctx_len + BAND, ctx_len + 2 * BAND
    top = lax.broadcasted_iota(jnp.int32, (rows, 1), 0) < BAND

    def keys_of(refs, keep_low):
        x = jnp.concatenate([r[0] for r in refs], axis=0)
        return jnp.where(low if keep_low else ~low, x, jnp.zeros_like(x))

    group = Q_HEADS // KV_HEADS
    k_plain, k_swap = (xk_ref, kp_ref, kc_ref, kn_ref), (xks_ref, ksp_ref, ksc_ref, ksn_ref)
    v_plain, v_swap = (xv_ref, vp_ref, vc_ref, vn_ref), (xvs_ref, vsp_ref, vsc_ref, vsn_ref)
    scores = {}
    for kvh in range(KV_HEADS):
        tile = kvh * (group // 2)
        q = jnp.concatenate([q_ref[0, :, tile * LANES:(tile + 1) * LANES],
                             q_ref[0, :, (tile + 1) * LANES:(tile + 2) * LANES]], axis=0)
        for sub in range(2):
            in_low = sub == 0
            scores[kvh, sub] = _dot_nt(q, keys_of(k_plain if (kvh == 0) == in_low else k_swap, in_low))
    for kvh in range(KV_HEADS):
        tile = kvh * (group // 2)
        acc = jnp.zeros((rows, LANES), F32)
        for sub in range(2):
            in_low = sub == 0
            v = keys_of(v_plain if (kvh == 0) == in_low else v_swap, in_low)
            sink = jnp.where(top, sink_ref[group * kvh + sub], sink_ref[group * kvh + 2 + sub])
            s = scores[kvh, sub]
            s = jnp.concatenate([s[:, :p0], jnp.where(ok_prev, s[:, p0:c0], -jnp.inf), s[:, c0:n0],
                                 jnp.where(ok_next, s[:, n0:], -jnp.inf)], axis=1)
            m = jnp.maximum(jnp.max(s, axis=1, keepdims=True), sink)
            e = jnp.exp(s - m)
            denom = jnp.sum(e, axis=1, keepdims=True) + jnp.exp(sink - m)
            acc = acc + _dot(e.astype(BF16), v) / denom
        y_ref[0, :, tile * LANES:(tile + 1) * LANES] = acc[:BAND].astype(BF16)
        y_ref[0, :, (tile + 1) * LANES:(tile + 2) * LANES] = acc[BAND:].astype(BF16)


def _attn(sink, aq, ak, aks, av, avs, xk, xks, xv, xvs):
    b, n, _ = aq.shape
    nb = n // BAND
    cur = lambda bi, i: (bi, i, 0)
    prev = lambda bi, i: (bi, jnp.maximum(i - 1, 0), 0)
    nxt = lambda bi, i: (bi, jnp.minimum(i + 1, nb - 1), 0)
    per_b = lambda bi, i: (bi, 0, 0)
    kv = lambda f: pl.BlockSpec((1, BAND, KV_WIDTH), f)
    cx = pl.BlockSpec((1, xk.shape[1], KV_WIDTH), per_b)
    return pl.pallas_call(
        _attn_kernel,
        grid=(b, nb),
        in_specs=[pl.BlockSpec(memory_space=pltpu.SMEM),
                  pl.BlockSpec((1, BAND, ATT_WIDTH), cur),
                  kv(prev), kv(cur), kv(nxt), kv(prev), kv(cur), kv(nxt),
                  kv(prev), kv(cur), kv(nxt), kv(prev), kv(cur), kv(nxt),
                  cx, cx, cx, cx],
        out_specs=pl.BlockSpec((1, BAND, ATT_WIDTH), cur),
        out_shape=jax.ShapeDtypeStruct((b, n, ATT_WIDTH), BF16),
        compiler_params=_params("arbitrary", "arbitrary"),
        name="window_attn",
    )(sink, aq, ak, ak, ak, aks, aks, aks, av, av, av, avs, avs, avs, xk, xks, xv, xvs)


def _route(hf, wr_t, bias):
    tm = hf.shape[0]
    ne = wr_t.shape[0]
    h_hi = hf.astype(BF16)
    h_lo = (hf - h_hi.astype(F32)).astype(BF16)
    w_hi = wr_t.astype(BF16)
    w_lo = (wr_t - w_hi.astype(F32)).astype(BF16)
    first = _dot_nt(jnp.concatenate([w_hi, w_lo], axis=0), h_hi)
    scores = jax.nn.sigmoid(first[:ne] + first[ne:] + _dot_nt(w_hi, h_lo))
    sel = scores + bias
    grp = sel.reshape(N_GROUPS, GROUP_SIZE, tm)
    j = lax.broadcasted_iota(jnp.int32, grp.shape, 1)
    m1 = jnp.max(grp, axis=1, keepdims=True)
    first = jnp.min(jnp.where(grp == m1, j, GROUP_SIZE), axis=1, keepdims=True)
    m2 = jnp.max(jnp.where(j == first, -jnp.inf, grp), axis=1, keepdims=True)
    gs = (m1 + m2).reshape(N_GROUPS, tm)
    gi = lax.broadcasted_iota(jnp.int32, gs.shape, 0)
    rank = jnp.zeros(gs.shape, jnp.int32)
    for g in range(N_GROUPS):
        other = gs[g:g + 1]
        rank = rank + ((other > gs) | ((other == gs) & (g < gi))).astype(jnp.int32)
    gsel = rank < TOPK_GROUPS
    emask = jnp.broadcast_to(gsel[:, None, :], grp.shape).reshape(N_EXPERTS, tm)
    cand = jnp.where(emask, sel, -jnp.inf)
    ei = lax.broadcasted_iota(jnp.int32, cand.shape, 0)
    chosen = jnp.zeros(cand.shape, jnp.bool_)
    for _ in range(TOP_K):
        best = jnp.max(cand, axis=0, keepdims=True)
        first = jnp.min(jnp.where(cand == best, ei, N_EXPERTS), axis=0, keepdims=True)
        hit = ei == first
        chosen = chosen | hit
        cand = jnp.where(hit, -jnp.inf, cand)
    w = jnp.where(chosen, scores, 0.0)
    return w / jnp.sum(w, axis=0, keepdims=True) * ROUTED_SCALE, jnp.where(chosen, 1.0, 0.0)


def _sort_rows(chosen, gates_t):
    ne, ts = chosen.shape
    sel = chosen.astype(BF16)
    chosen = chosen > 0.5
    r = lax.broadcasted_iota(jnp.int32, (ts, ts), 0)
    c = lax.broadcasted_iota(jnp.int32, (ts, ts), 1)
    seen = _dot(sel, jnp.where(r <= c, 1.0, 0.0).astype(BF16))
    total = _dot(sel, jnp.ones((ts, ts), BF16))
    padded = jnp.floor((total + (SORT_UNIT - 1)) * (1.0 / SORT_UNIT)) * SORT_UNIT
    er = lax.broadcasted_iota(jnp.int32, (ne, ne), 0)
    ec = lax.broadcasted_iota(jnp.int32, (ne, ne), 1)
    before = jnp.where(ec < er, 1.0, 0.0).astype(BF16)
    start = _dot(before, padded.astype(BF16))
    choice = _dot(before, sel)
    row = start + seen - 1.0
    dest, gate = [], []
    for k in range(TOP_K):
        mk = chosen & (choice == k)
        dest.append(jnp.sum(jnp.where(mk, row, 0.0), axis=0, keepdims=True))
        gate.append(jnp.sum(jnp.where(mk, gates_t, 0.0), axis=0, keepdims=True))
    sel_pad = jnp.concatenate([sel, jnp.zeros((LANES - ne, ts), BF16)], axis=0)
    counts = _dot_nt(jnp.ones((8, ts), BF16), sel_pad)
    dest = jnp.concatenate(dest, axis=0)
    pad = jnp.zeros((LANES - TOP_K, ts), F32)
    dest_cols = jnp.concatenate([dest, pad], axis=0).T
    gate_cols = jnp.concatenate(gate + [pad], axis=0).T
    return dest.astype(jnp.int32), dest_cols.astype(jnp.int32), gate_cols, counts


def _outproj_kernel(x_ref, yh_ref, ya_ref, w_ref, g1_ref, sh_ref, sc_ref, lw_ref, lb_ref,
                    wr_ref, rb_ref, x1_ref, h2_ref, dest_ref, destc_ref, gatec_ref, cnt_ref):
    y = _dot(yh_ref[0], w_ref[:HG_WIDTH, :]) + _dot(ya_ref[0], w_ref[HG_WIDTH:, :])
    x1 = _ln_rows(ALPHA * x_ref[0] + g1_ref[0] * y, LN_EPS) * lw_ref[...] + lb_ref[...]
    x1_ref[0] = x1
    hf = _ln_rows(x1, NORM_EPS) * (1.0 + sc_ref[0]) + sh_ref[0]
    h2_ref[0] = hf.astype(BF16)
    gates_t, chosen = _route(hf, wr_ref[...], rb_ref[...])
    for s in range(hf.shape[0] // SORT_TILE):
        sl = slice(s * SORT_TILE, (s + 1) * SORT_TILE)
        dest_ref[s], destc_ref[s], gatec_ref[s], cnt_ref[s] = _sort_rows(chosen[:, sl], gates_t[:, sl])


def _outproj(x, yh, ya, w, g1, sh2, sc2, lw, lb, wr_t, rbias, tm):
    b, n, d = x.shape
    row = lambda bi, i: (bi, i, 0)
    per_b = lambda bi, i: (bi, 0, 0)
    const = lambda bi, i: (0, 0)
    nt = b * n // SORT_TILE
    per_step = tm // SORT_TILE
    tiles = lambda bi, i: (bi * (n // tm) + i, 0, 0)
    return pl.pallas_call(
        _outproj_kernel,
        grid=(b, n // tm),
        in_specs=[pl.BlockSpec((1, tm, d), row),
                  pl.BlockSpec((1, tm, HG_WIDTH), row), pl.BlockSpec((1, tm, ATT_WIDTH), row),
                  pl.BlockSpec(w.shape, const),
                  pl.BlockSpec((1, 1, d), per_b), pl.BlockSpec((1, 1, d), per_b),
                  pl.BlockSpec((1, 1, d), per_b),
                  pl.BlockSpec((1, d), const), pl.BlockSpec((1, d), const),
                  pl.BlockSpec(wr_t.shape, const), pl.BlockSpec(rbias.shape, const)],
        out_specs=[pl.BlockSpec((1, tm, d), row), pl.BlockSpec((1, tm, d), row),
                   pl.BlockSpec((per_step, TOP_K, SORT_TILE), tiles),
                   pl.BlockSpec((per_step, SORT_TILE, LANES), tiles),
                   pl.BlockSpec((per_step, SORT_TILE, LANES), tiles),
                   pl.BlockSpec((per_step, 8, LANES), tiles)],
        out_shape=[jax.ShapeDtypeStruct((b, n, d), F32), jax.ShapeDtypeStruct((b, n, d), BF16),
                   jax.ShapeDtypeStruct((nt, TOP_K, SORT_TILE), jnp.int32),
                   jax.ShapeDtypeStruct((nt, SORT_TILE, LANES), jnp.int32),
                   jax.ShapeDtypeStruct((nt, SORT_TILE, LANES), F32),
                   jax.ShapeDtypeStruct((nt, 8, LANES), F32)],
        compiler_params=_params("arbitrary", "arbitrary"),
        name="outproj_ln_router",
    )(x, yh, ya, w, g1, sh2, sc2, lw, lb, wr_t, rbias)


def _moe_layout(counts, max_tiles):
    n_pad = (counts + (SORT_UNIT - 1)) // SORT_UNIT * SORT_UNIT
    total = jnp.sum(n_pad, axis=0)
    region = (total + (EXPERT_TILE - 1)) // EXPERT_TILE * EXPERT_TILE
    base = jnp.cumsum(region) - region
    chunk_row = base[None, :] + jnp.cumsum(n_pad, axis=0) - n_pad
    tile_end = jnp.cumsum(region // EXPERT_TILE)
    n_used = tile_end[-1]
    j = jnp.minimum(jnp.arange(max_tiles, dtype=jnp.int32), n_used - 1)
    tile_expert = jnp.sum((tile_end[None, :] <= j[:, None]).astype(jnp.int32), axis=1)
    units = n_pad // SORT_UNIT
    local_row = jnp.cumsum(n_pad, axis=1) - n_pad
    rare = jnp.ones(units.shape, jnp.bool_)
    counts_out, lists = [], []
    for size, width in PIECE_LISTS[:-1]:
        mask = units == size
        rare = rare & ~mask
        slot = jnp.cumsum(mask, axis=1) - 1
        pick = mask[:, None, :] & (slot[:, None, :] == jnp.arange(width)[None, :, None])
        counts_out.append(jnp.sum(mask, axis=1))
        lists += [jnp.sum(jnp.where(pick, v[:, None, :], 0), axis=2) for v in (local_row, chunk_row)]
    single = jnp.where(rare, units, 0)
    single_end = jnp.cumsum(single, axis=1)
    u = jnp.arange(LOCAL_UNITS)[None, :, None]
    first = (single_end - single)[:, None, :]
    pick = (u >= first) & (u < single_end[:, None, :])
    counts_out.append(single_end[:, -1])
    lists += [jnp.sum(jnp.where(pick, v[:, None, :] + (u - first) * SORT_UNIT, 0), axis=2)
              for v in (local_row, chunk_row)]
    header = jnp.stack(counts_out + [jnp.sum(n_pad, axis=1)], axis=1)
    header = jnp.pad(header, ((0, 0), (0, TABLE_HEADER - header.shape[1])))
    table = jnp.concatenate([header] + lists, axis=1)
    table = jnp.pad(table, ((0, 0), (0, TABLE_WIDTH - table.shape[1]))).astype(jnp.int32)[:, None, :]
    tails = jnp.concatenate([base + total, (region - total) // SORT_UNIT]).astype(jnp.int32)
    return table, tails, tile_expert, n_used.astype(jnp.int32)[None]


def _row_cases(rows_used, fn):
    @pl.when(rows_used <= USUAL_ROWS)
    def _():
        fn(USUAL_ROWS)

    @pl.when(rows_used > USUAL_ROWS)
    def _():
        fn(LOCAL_ROWS)


def _unit(ref, row, units=1):
    return ref.at[pl.ds(pl.multiple_of(row, SORT_UNIT), units * SORT_UNIT), :]


def _for_each_piece(table_ref, fn):
    offset = TABLE_HEADER
    for c, (size, width) in enumerate(PIECE_LISTS):
        def body(j, carry, offset=offset, size=size, width=width):
            fn(table_ref[0, 0, offset + j], table_ref[0, 0, offset + width + j], size)
            return carry

        lax.fori_loop(0, table_ref[0, 0, c], body, 0)
        offset += 2 * width


def _await_units(units, wait_fn):
    def many(u, carry):
        wait_fn(WAIT_UNITS)
        return carry

    def single(u, carry):
        wait_fn(1)
        return carry

    lax.fori_loop(0, units // WAIT_UNITS, many, 0)
    lax.fori_loop(0, units % WAIT_UNITS, single, 0)


def _dispatch_kernel(chunks_ref, tails_ref, h_ref, dest_ref, xs_ref, buf_ref, zero_ref, pending_ref, sem):
    i = pl.program_id(0)
    slot = i % 2
    buf = buf_ref.at[slot]

    def drain(s):
        _await_units(pending_ref[s], lambda n: pltpu.make_async_copy(
            _unit(buf_ref.at[s], 0, n), _unit(xs_ref, 0, n), sem.at[s]).wait())

    @pl.when(i >= 2)
    def _():
        drain(slot)

    def permute(rows):
        r = lax.broadcasted_iota(jnp.int32, (rows, SORT_TILE), 0).astype(jnp.int16)
        p = jnp.zeros((rows, SORT_TILE), BF16)
        one = jnp.ones((rows, SORT_TILE), BF16)
        for k in range(TOP_K):
            hit = r == dest_ref[0, k:k + 1, :].astype(jnp.int16)
            p = jnp.where(hit, one, p)
        buf[0:rows, :] = _dot(p, h_ref[...]).astype(BF16)

    rows_used = chunks_ref[0, 0, TABLE_ROWS_USED]
    _row_cases(rows_used, permute)
    _for_each_piece(chunks_ref, lambda loc, glob, n: pltpu.make_async_copy(
        _unit(buf, loc, n), _unit(xs_ref, glob, n), sem.at[slot]).start())
    pending_ref[slot] = rows_used // SORT_UNIT

    @pl.when(i == pl.num_programs(0) - 1)
    def _():
        zero_ref[...] = jnp.zeros_like(zero_ref)

        def per_expert(e, total):
            n = tails_ref[N_EXPERTS + e]
            start = tails_ref[e]

            def per_unit(u, carry):
                pltpu.make_async_copy(zero_ref, _unit(xs_ref, start + u * SORT_UNIT), sem.at[2]).start()
                return carry

            lax.fori_loop(0, n, per_unit, 0)
            return total + n

        total = lax.fori_loop(0, N_EXPERTS, per_expert, 0)

        def drain_zero(u, carry):
            pltpu.make_async_copy(zero_ref, _unit(xs_ref, 0), sem.at[2]).wait()
            return carry

        lax.fori_loop(0, total, drain_zero, 0)

        @pl.when(i >= 1)
        def _():
            drain(1 - slot)

        drain(slot)


def _dispatch(chunks, tails, h2, dest, max_rows):
    t, d = h2.shape
    nt = t // SORT_TILE
    return pl.pallas_call(
        _dispatch_kernel,
        grid=(nt,),
        in_specs=[pl.BlockSpec((1, 1, TABLE_WIDTH), lambda i: (i, 0, 0), memory_space=pltpu.SMEM),
                  pl.BlockSpec(memory_space=pltpu.SMEM),
                  pl.BlockSpec((SORT_TILE, d), lambda i: (i, 0)),
                  pl.BlockSpec((1, TOP_K, SORT_TILE), lambda i: (i, 0, 0))],
        out_specs=pl.BlockSpec(memory_space=pl.ANY),
        out_shape=jax.ShapeDtypeStruct((max_rows, d), BF16),
        scratch_shapes=[pltpu.VMEM((2, LOCAL_ROWS, d), BF16), pltpu.VMEM((SORT_UNIT, d), BF16),
                        pltpu.SMEM((2,), jnp.int32), pltpu.SemaphoreType.DMA((3,))],
        compiler_params=_params("arbitrary"),
        name="moe_dispatch",
    )(chunks, tails, h2, dest)


def _expert_kernel(te_ref, nu_ref, xs_ref, wg_ref, wu_ref, wd_ref, ys_ref):
    @pl.when(pl.program_id(0) < nu_ref[0])
    def _():
        x = xs_ref[...]
        a = _silu(_dot(x, wg_ref[0].astype(BF16))) * _dot(x, wu_ref[0].astype(BF16))
        ys_ref[...] = _dot(a.astype(BF16), wd_ref[0].astype(BF16)).astype(BF16)


def _experts(tile_expert, n_used, xs, wg, wu, wd):
    rows, d = xs.shape
    ff = wg.shape[2]
    row = lambda j, te, nu: (jnp.minimum(j, jnp.maximum(nu[0] - 1, 0)), 0)
    exp = lambda j, te, nu: (te[j], 0, 0)
    return pl.pallas_call(
        _expert_kernel,
        grid_spec=pltpu.PrefetchScalarGridSpec(
            num_scalar_prefetch=2,
            grid=(rows // EXPERT_TILE,),
            in_specs=[pl.BlockSpec((EXPERT_TILE, d), row),
                      pl.BlockSpec((1, d, ff), exp), pl.BlockSpec((1, d, ff), exp),
                      pl.BlockSpec((1, ff, d), exp)],
            out_specs=pl.BlockSpec((EXPERT_TILE, d), row)),
        out_shape=jax.ShapeDtypeStruct((rows, d), BF16),
        compiler_params=_params("arbitrary"),
        name="moe_experts",
    )(tile_expert, n_used, xs, wg, wu, wd)


def _combine_kernel(chunks_ref, next_ref, ys_ref, dest_ref, gate_ref, h_ref, x1_ref, g2_ref, lw_ref, lb_ref,
                    swg_ref, swu_ref, swd_ref, o_ref, buf_ref, sem):
    i = pl.program_id(0)
    slot = i % 2

    def fetch(meta_ref, s):
        _for_each_piece(meta_ref, lambda loc, glob, n: pltpu.make_async_copy(
            _unit(ys_ref, glob, n), _unit(buf_ref.at[s], loc, n), sem.at[s]).start())

    @pl.when(i == 0)
    def _():
        buf_ref[...] = jnp.zeros_like(buf_ref)
        fetch(chunks_ref, 0)

    @pl.when(i + 1 < pl.num_programs(0))
    def _():
        fetch(next_ref, 1 - slot)

    h = h_ref[...]
    a = _silu(_dot(h, swg_ref[...])) * _dot(h, swu_ref[...])
    shared = _dot(a.astype(BF16), swd_ref[...])
    rows_used = chunks_ref[0, 0, TABLE_ROWS_USED]
    _await_units(rows_used // SORT_UNIT, lambda n: pltpu.make_async_copy(
        _unit(ys_ref, 0, n), _unit(buf_ref.at[slot], 0, n), sem.at[slot]).wait())

    def unpermute(rows):
        lane = lax.broadcasted_iota(jnp.int32, (SORT_TILE, rows), 1).astype(jnp.int16)
        p = jnp.zeros((SORT_TILE, rows), BF16)
        for k in range(TOP_K):
            hit = lane == dest_ref[0, :, k:k + 1].astype(jnp.int16)
            p = jnp.where(hit, jnp.broadcast_to(gate_ref[0, :, k:k + 1].astype(BF16), p.shape), p)
        ffn = shared + _dot(p, buf_ref[slot, 0:rows, :])
        u = ALPHA * x1_ref[...] + g2_ref[0] * ffn
        o_ref[...] = _ln_rows(u, LN_EPS) * lw_ref[...] + lb_ref[...]

    _row_cases(rows_used, unpermute)


def _combine(chunks, ys, dest, gate, h2, x1, g2, lw, lb, swg, swu, swd, tiles_per_batch):
    t, d = h2.shape
    ff = swg.shape[1]
    nt = t // SORT_TILE
    tile = lambda i: (i, 0, 0)
    nxt = lambda i: (jnp.minimum(i + 1, nt - 1), 0, 0)
    row = lambda i: (i, 0)
    const = lambda i: (0, 0)
    return pl.pallas_call(
        _combine_kernel,
        grid=(nt,),
        in_specs=[pl.BlockSpec((1, 1, TABLE_WIDTH), tile, memory_space=pltpu.SMEM),
                  pl.BlockSpec((1, 1, TABLE_WIDTH), nxt, memory_space=pltpu.SMEM),
                  pl.BlockSpec(memory_space=pl.ANY),
                  pl.BlockSpec((1, SORT_TILE, LANES), tile), pl.BlockSpec((1, SORT_TILE, LANES), tile),
                  pl.BlockSpec((SORT_TILE, d), row), pl.BlockSpec((SORT_TILE, d), row),
                  pl.BlockSpec((1, 1, d), lambda i: (i // tiles_per_batch, 0, 0)),
                  pl.BlockSpec((1, d), const), pl.BlockSpec((1, d), const),
                  pl.BlockSpec((d, ff), const), pl.BlockSpec((d, ff), const), pl.BlockSpec((ff, d), const)],
        out_specs=pl.BlockSpec((SORT_TILE, d), row),
        out_shape=jax.ShapeDtypeStruct((t, d), F32),
        scratch_shapes=[pltpu.VMEM((2, LOCAL_ROWS, d), BF16), pltpu.SemaphoreType.DMA((2,))],
        compiler_params=_params("arbitrary"),
        name="moe_combine",
    )(chunks, chunks, ys, dest, gate, h2, x1, g2, lw, lb, swg, swu, swd)


def _moe(h2, x1, dest, dest_cols, gate_cols, counts, g2, lw, lb, wg, wu, wd, swg, swu, swd):
    b, n, d = x1.shape
    t = b * n
    nt = t // SORT_TILE
    max_rows = t * TOP_K + nt * N_EXPERTS * (SORT_UNIT - 1) + N_EXPERTS * (EXPERT_TILE - SORT_UNIT)
    max_rows = -(-max_rows // EXPERT_TILE) * EXPERT_TILE
    chunks, tails, tile_expert, n_used = _moe_layout(
        counts[:, 0, :N_EXPERTS].astype(jnp.int32), max_rows // EXPERT_TILE)
    h2 = h2.reshape(t, d)
    xs = _dispatch(chunks, tails, h2, dest, max_rows)
    ys = _experts(tile_expert, n_used, xs, wg, wu, wd)
    out = _combine(chunks, ys, dest_cols, gate_cols, h2, x1.reshape(t, d), g2, lw, lb, swg, swu, swd,
                   n // SORT_TILE)
    return out.reshape(b, n, d)


def _swap_kv_heads(w):
    return jnp.concatenate([w[:, HEAD_DIM:], w[:, :HEAD_DIM]], axis=1)


def _split_w_in(w_in):
    bounds = np.cumsum([HG_WIDTH] * 5 + [ATT_WIDTH, KV_WIDTH])
    return jnp.split(w_in, [int(v) for v in bounds], axis=1)


def _context_weight(w_in):
    _, zff, zfb, zi, _, _, ak, av = _split_w_in(w_in)
    return jnp.concatenate([zff, zfb, zi, ak, _swap_kv_heads(ak), av, _swap_kv_heads(av)],
                           axis=1).astype(BF16)


def _rope_tables(n):
    pos = jnp.arange(n)
    freqs = ROPE_BASE ** (-jnp.arange(ROT_PAIRS, dtype=F32) / ROT_PAIRS)
    ang_row = (pos // GRID_W).astype(F32)[:, None] * freqs
    ang_col = (pos % GRID_W).astype(F32)[:, None] * freqs
    cos = jnp.concatenate([jnp.cos(ang_row)] * 2 + [jnp.cos(ang_col)] * 2, axis=1)
    sin = jnp.concatenate([-jnp.sin(ang_row), jnp.sin(ang_row), -jnp.sin(ang_col), jnp.sin(ang_col)], axis=1)
    reps = LANES // HEAD_DIM
    return jnp.concatenate([cos] * reps, axis=1), jnp.concatenate([sin] * reps, axis=1)


def kernel(x, c, ctx, c_ctx, w_ada, b_ada, w_in, hg_lb_fwd, hg_lb_bwd, hg_norm_w, attn_sink, w_out, ln1_w, ln1_b, router_w, router_bias, exp_w_gate, exp_w_up, exp_w_down, shared_w_gate, shared_w_up, shared_w_down, ln2_w, ln2_b):
    b, n, d = x.shape
    layer = 0
    rows = -(-(b + 1) // 8) * 8
    cc = jnp.zeros((rows, d), F32).at[:b].set(c).at[b].set(c_ctx)
    mod = _ada(cc, w_ada[layer], b_ada[layer][None, :])
    sh1, sc1, g1, sh2, sc2, g2 = [m[:, None, :] for m in jnp.split(mod[:b], 6, axis=1)]
    csh1, csc1 = mod[b:b + 1, :d], mod[b:b + 1, d:2 * d]

    lb_f = jnp.cumsum(jax.nn.softmax(hg_lb_fwd.astype(F32), axis=0), axis=0)[layer][None, :]
    lb_b = jnp.cumsum(jax.nn.softmax(hg_lb_bwd.astype(F32), axis=0), axis=0)[layer][None, :]
    cos, sin = _rope_tables(n)

    xk, xks, xv, xvs, s0f, s0b = _ctx(ctx, csh1, csc1, _context_weight(w_in[layer]), lb_f, lb_b)
    tm = min(n, 512)
    (q, kf, gf, kb, gb, v, zg, aq, ak, aks, av, avs) = _inproj(
        x, sh1, sc1, w_in[layer].astype(BF16), lb_f, lb_b, cos, sin, tm)
    yh = _hgrn(q, kf, gf, kb, gb, v, zg, s0f, s0b, hg_norm_w[layer][None, :])
    ya = _attn(attn_sink[layer], aq, ak, aks, av, avs, xk, xks, xv, xvs)

    x1, h2, dest, dest_cols, gate_cols, counts = _outproj(
        x, yh, ya, w_out[layer].astype(BF16), g1, sh2, sc2, ln1_w[layer][None, :], ln1_b[layer][None, :],
        router_w[layer].T, router_bias[layer][:, None], tm)
    return _moe(h2, x1, dest, dest_cols, gate_cols, counts, g2, ln2_w[layer][None, :], ln2_b[layer][None, :],
                exp_w_gate[layer], exp_w_up[layer], exp_w_down[layer], shared_w_gate[layer].astype(BF16),
                shared_w_up[layer].astype(BF16), shared_w_down[layer].astype(BF16))
```

```python
import jax
import jax.numpy as jnp
import numpy as np
from jax import lax
from jax.experimental import pallas as pl
from jax.experimental.pallas import tpu as pltpu

F32 = jnp.float32
BF16 = jnp.bfloat16
HIGHEST = lax.Precision.HIGHEST

DEPTH = 1
GRID_W = 64
HG_WIDTH = 512
HG_HEADS = 4
HG_DIM = 128
HG_CHUNK = 64
HG_SUB = 16
HG_BATCH = 4
LOG_F_MIN = -4.0
HEAD_DIM = 64
Q_HEADS = 8
KV_HEADS = 2
ATT_WIDTH = Q_HEADS * HEAD_DIM
KV_WIDTH = KV_HEADS * HEAD_DIM
BAND = 128
ROPE_BASE = 10000.0
ROT_PAIRS = HEAD_DIM // 4
N_EXPERTS = 64
TOP_K = 8
N_GROUPS = 8
GROUP_SIZE = N_EXPERTS // N_GROUPS
TOPK_GROUPS = 4
ROUTED_SCALE = 2.5
LN_EPS = 1e-5
NORM_EPS = 1e-6
ALPHA = (2.0 * DEPTH) ** 0.25

LANES = 128
BF16_SUBLANES = 16
MXU_DIM = 256

SORT_TILE = MXU_DIM
SORT_UNIT = BF16_SUBLANES
EXPERT_TILE = 1024
_LOCAL_WORST = SORT_TILE * TOP_K + N_EXPERTS * (SORT_UNIT - 1)
LOCAL_ROWS = -(-_LOCAL_WORST // MXU_DIM) * MXU_DIM
USUAL_ROWS = SORT_TILE * TOP_K + 2 * MXU_DIM
LOCAL_UNITS = LOCAL_ROWS // SORT_UNIT
COMMON_UNITS = (2, 3)
PIECE_LISTS = tuple((size, N_EXPERTS) for size in COMMON_UNITS) + ((1, LOCAL_UNITS),)
TABLE_HEADER = 8
TABLE_ROWS_USED = len(PIECE_LISTS)
TABLE_WIDTH = -(-(TABLE_HEADER + 2 * sum(width for _, width in PIECE_LISTS)) // LANES) * LANES
WAIT_UNITS = 8
V7X_VMEM_LIMIT_BYTES = 56 * 1024 * 1024


def _params(*sem):
    return pltpu.CompilerParams(dimension_semantics=sem, vmem_limit_bytes=V7X_VMEM_LIMIT_BYTES)


def _ln_rows(x, eps):
    mu = jnp.mean(x, axis=-1, keepdims=True)
    xc = x - mu
    return xc * lax.rsqrt(jnp.mean(xc * xc, axis=-1, keepdims=True) + eps)


def _silu(x):
    return x * jax.nn.sigmoid(x)


def _dot(a, b):
    return jnp.dot(a, b, preferred_element_type=F32)


def _dot_nt(a, b, precision=None):
    return lax.dot_general(a, b, (((1,), (1,)), ((), ())), precision=precision,
                           preferred_element_type=F32)


def _dot_tn(a, b):
    return lax.dot_general(a, b, (((0,), (0,)), ((), ())), preferred_element_type=F32)


def _forget_gate(z, lb):
    f = lb + (1.0 - lb) * jax.nn.sigmoid(z)
    return 1.0 - f, jnp.maximum(jnp.log(f), LOG_F_MIN)


def _chunk_scan(g, reverse):
    n = g.shape[0]
    pos = lax.broadcasted_iota(jnp.int32, g.shape, 0) % HG_CHUNK
    step = 1
    while step < HG_CHUNK:
        if reverse:
            g = g + jnp.where(pos < HG_CHUNK - step, pltpu.roll(g, n - step, axis=0), 0.0)
        else:
            g = g + jnp.where(pos >= step, pltpu.roll(g, step, axis=0), 0.0)
        step *= 2
    return g


def _ada_kernel(c_ref, w_ref, b_ref, o_ref):
    c = c_ref[...]
    o_ref[...] = jnp.dot(_silu(c), w_ref[...], precision=HIGHEST,
                         preferred_element_type=F32) + b_ref[...]


def _ada(cc, w, b):
    rows, d = cc.shape
    cols = w.shape[1]
    tn = 512
    return pl.pallas_call(
        _ada_kernel,
        grid=(cols // tn,),
        in_specs=[pl.BlockSpec((rows, d), lambda j: (0, 0)),
                  pl.BlockSpec((d, tn), lambda j: (0, j)),
                  pl.BlockSpec((1, tn), lambda j: (0, j))],
        out_specs=pl.BlockSpec((rows, tn), lambda j: (0, j)),
        out_shape=jax.ShapeDtypeStruct((rows, cols), F32),
        compiler_params=_params("arbitrary"),
        name="adaln",
    )(cc, w, b)


_C_Q, _C_FF, _C_FB, _C_I, _C_G = 0, 512, 1024, 1536, 2048
_C_AQ, _C_AK, _C_AV = 2560, 3072, 3200


def _swap_rotary_halves(x):
    first = lax.broadcasted_iota(jnp.int32, (1, LANES), 1) % (2 * ROT_PAIRS) < ROT_PAIRS
    tiles = []
    for t in range(x.shape[1] // LANES):
        xt = x[:, t * LANES:(t + 1) * LANES]
        tiles.append(jnp.where(first, pltpu.roll(xt, LANES - ROT_PAIRS, axis=1),
                               pltpu.roll(xt, ROT_PAIRS, axis=1)))
    return jnp.concatenate(tiles, axis=1)


def _inproj_kernel(x_ref, sh_ref, sc_ref, w_ref, lbf_ref, lbb_ref, cos_ref, sin_ref,
                   q_ref, kf_ref, gf_ref, kb_ref, gb_ref, v_ref, zg_ref,
                   aq_ref, ak_ref, aks_ref, av_ref, avs_ref):
    h = (_ln_rows(x_ref[0], NORM_EPS) * (1.0 + sc_ref[0]) + sh_ref[0]).astype(BF16)

    def proj(lo, n):
        return _dot(h, w_ref[:, lo:lo + n])

    zq, zff, zfb = proj(_C_Q, HG_WIDTH), proj(_C_FF, HG_WIDTH), proj(_C_FB, HG_WIDTH)
    zi, zg = proj(_C_I, HG_WIDTH), proj(_C_G, HG_WIDTH)
    aq, ak, av = proj(_C_AQ, ATT_WIDTH), proj(_C_AK, KV_WIDTH), proj(_C_AV, KV_WIDTH)

    q_ref[0] = zq.astype(BF16)
    k, g = _forget_gate(zff, lbf_ref[...])
    kf_ref[0] = k.astype(BF16)
    gf_ref[0] = _chunk_scan(g, False)
    k, g = _forget_gate(zfb, lbb_ref[...])
    kb_ref[0] = k.astype(BF16)
    gb_ref[0] = _chunk_scan(g, True)
    v_ref[0] = zi.astype(BF16)
    zg_ref[0] = zg.astype(BF16)
    cos = cos_ref[...]
    sin = sin_ref[...]
    cos4 = jnp.concatenate([cos] * (ATT_WIDTH // LANES), axis=1)
    sin4 = jnp.concatenate([sin] * (ATT_WIDTH // LANES), axis=1)
    scale = HEAD_DIM ** -0.5
    aq_ref[0] = ((aq * cos4 + _swap_rotary_halves(aq) * sin4) * scale).astype(BF16)
    ak = ak * cos + _swap_rotary_halves(ak) * sin
    ak_ref[0] = ak.astype(BF16)
    aks_ref[0] = pltpu.roll(ak, HEAD_DIM, axis=1).astype(BF16)
    av_ref[0] = av.astype(BF16)
    avs_ref[0] = pltpu.roll(av, HEAD_DIM, axis=1).astype(BF16)


def _inproj(x, sh, sc, w, lbf, lbb, cos, sin, tm):
    b, n, d = x.shape
    row = lambda bi, i: (bi, i, 0)
    per_b = lambda bi, i: (bi, 0, 0)
    const = lambda bi, i: (0, 0)
    tab = lambda bi, i: (i, 0)

    def out(width, dtype):
        return jax.ShapeDtypeStruct((b, n, width), dtype), pl.BlockSpec((1, tm, width), row)

    outs = [out(HG_WIDTH, BF16), out(HG_WIDTH, BF16), out(HG_WIDTH, F32), out(HG_WIDTH, BF16),
            out(HG_WIDTH, F32), out(HG_WIDTH, BF16), out(HG_WIDTH, BF16),
            out(ATT_WIDTH, BF16), out(KV_WIDTH, BF16), out(KV_WIDTH, BF16),
            out(KV_WIDTH, BF16), out(KV_WIDTH, BF16)]
    return pl.pallas_call(
        _inproj_kernel,
        grid=(b, n // tm),
        in_specs=[pl.BlockSpec((1, tm, d), row),
                  pl.BlockSpec((1, 1, d), per_b), pl.BlockSpec((1, 1, d), per_b),
                  pl.BlockSpec(w.shape, const),
                  pl.BlockSpec((1, HG_WIDTH), const), pl.BlockSpec((1, HG_WIDTH), const),
                  pl.BlockSpec((tm, LANES), tab), pl.BlockSpec((tm, LANES), tab)],
        out_specs=[o[1] for o in outs],
        out_shape=[o[0] for o in outs],
        compiler_params=_params("arbitrary", "arbitrary"),
        name="latent_inproj",
    )(x, sh, sc, w, lbf, lbb, cos, sin)


_X_FF, _X_FB, _X_I, _X_AK, _X_AKS, _X_AV, _X_AVS, _X_TOTAL = 0, 512, 1024, 1536, 1664, 1792, 1920, 2048


def _ctx_kernel(c_ref, sh_ref, sc_ref, w_ref, lbf_ref, lbb_ref,
                k_ref, ks_ref, v_ref, vs_ref, sf_ref, sb_ref):
    h = (_ln_rows(c_ref[0], NORM_EPS) * (1.0 + sc_ref[...]) + sh_ref[...]).astype(BF16)

    def proj(lo, n):
        return _dot(h, w_ref[:, lo:lo + n])

    k_ref[0] = proj(_X_AK, KV_WIDTH).astype(BF16)
    ks_ref[0] = proj(_X_AKS, KV_WIDTH).astype(BF16)
    v_ref[0] = proj(_X_AV, KV_WIDTH).astype(BF16)
    vs_ref[0] = proj(_X_AVS, KV_WIDTH).astype(BF16)

    kf, gf = _forget_gate(proj(_X_FF, HG_WIDTH), lbf_ref[...])
    kb, gb = _forget_gate(proj(_X_FB, HG_WIDTH), lbb_ref[...])
    vi = proj(_X_I, HG_WIDTH).astype(BF16)
    n = h.shape[0]
    r = lax.broadcasted_iota(jnp.int32, (n, n), 0)
    c = lax.broadcasted_iota(jnp.int32, (n, n), 1)
    bf = jnp.dot((c <= r).astype(F32), gf, precision=HIGHEST, preferred_element_type=F32)
    bb = jnp.dot((c >= r).astype(F32), gb, precision=HIGHEST, preferred_element_type=F32)
    kdf = (kf * jnp.exp(bf[n - 1:n] - bf)).astype(BF16)
    kdb = (kb * jnp.exp(bb[0:1] - bb)).astype(BF16)
    for hd in range(HG_HEADS):
        sl = slice(hd * HG_DIM, (hd + 1) * HG_DIM)
        sf_ref[0, hd] = _dot_tn(vi[:, sl], kdf[:, sl])
        sb_ref[0, hd] = _dot_tn(vi[:, sl], kdb[:, sl])


def _ctx(ctx, sh, sc, w, lbf, lbb):
    b, n, d = ctx.shape
    per_b = lambda bi: (bi, 0, 0)
    const = lambda bi: (0, 0)
    kv = (jax.ShapeDtypeStruct((b, n, KV_WIDTH), BF16), pl.BlockSpec((1, n, KV_WIDTH), per_b))
    st = (jax.ShapeDtypeStruct((b, HG_HEADS, HG_DIM, HG_DIM), F32),
          pl.BlockSpec((1, HG_HEADS, HG_DIM, HG_DIM), lambda bi: (bi, 0, 0, 0)))
    outs = [kv, kv, kv, kv, st, st]
    return pl.pallas_call(
        _ctx_kernel,
        grid=(b,),
        in_specs=[pl.BlockSpec((1, n, d), per_b),
                  pl.BlockSpec((1, d), const), pl.BlockSpec((1, d), const),
                  pl.BlockSpec(w.shape, const),
                  pl.BlockSpec((1, HG_WIDTH), const), pl.BlockSpec((1, HG_WIDTH), const)],
        out_specs=[o[1] for o in outs],
        out_shape=[o[0] for o in outs],
        compiler_params=_params("arbitrary"),
        name="context_side",
    )(ctx, sh, sc, w, lbf, lbb)


def _hgrn_chunk_start(q, k, v, b, reverse):
    cs, us = HG_CHUNK, HG_SUB
    ns = cs // us
    last = 0 if reverse else cs - 1
    b_last = b[last:last + 1]
    qf = q.astype(F32)
    kf = k.astype(F32)
    q_ref_rows, k_blocks = [], []
    for s in range(ns):
        if reverse:
            keys = slice(cs - us * (s + 1), cs)
            ref = b[cs - us * s:cs - us * s + 1] if s > 0 else jnp.zeros_like(b_last)
        else:
            keys = slice(0, us * (s + 1))
            ref = b[us * s - 1:us * s] if s > 0 else jnp.zeros_like(b_last)
        q_ref_rows.append(jnp.broadcast_to(ref, (us, HG_DIM)))
        kh = (kf[keys] * jnp.exp(ref - b[keys])).astype(BF16)
        pad = jnp.zeros((cs - us * (s + 1), HG_DIM), BF16)
        if pad.shape[0]:
            kh = jnp.concatenate([pad, kh] if reverse else [kh, pad], axis=0)
        k_blocks.append(kh)
    q_ref = jnp.concatenate(q_ref_rows[::-1] if reverse else q_ref_rows, axis=0)
    qh = qf * jnp.exp(b - q_ref)
    sub = lax.broadcasted_iota(jnp.int32, (cs, HG_DIM), 0) // us
    if reverse:
        sub = ns - 1 - sub
    q_cat = jnp.concatenate([jnp.where(sub == s, qh, 0.0).astype(BF16) for s in range(ns)], axis=1)
    att = _dot_nt(q_cat, jnp.concatenate(k_blocks, axis=1))
    kdec = (kf * jnp.exp(b_last - b)).astype(BF16)
    return att, (qf * jnp.exp(b)).astype(BF16), v, jnp.exp(b_last), _dot_tn(v, kdec)


def _hgrn_chunk_finish(parts, st, reverse):
    att, q_dec, v, decay, update = parts
    ri = lax.broadcasted_iota(jnp.int32, att.shape, 0)
    ci = lax.broadcasted_iota(jnp.int32, att.shape, 1)
    att = jnp.where((ci >= ri) if reverse else (ci <= ri), att, 0.0)
    o = _dot(att.astype(BF16), v) + _dot_nt(q_dec, st.astype(BF16))
    return o, st * decay + update


def _hgrn_kernel(q_ref, kf_ref, gf_ref, kb_ref, gb_ref, v_ref, zg_ref, s0f_ref, s0b_ref, nw_ref,
                 y_ref, of_ref, ob_ref):
    n = q_ref.shape[1]
    cs = HG_CHUNK
    nc = n // cs

    def body(i, carry):
        sf, sb = carry
        started = []
        for j in range(HG_BATCH):
            fwd = pl.ds(pl.multiple_of((i * HG_BATCH + j) * cs, cs), cs)
            bwd = pl.ds(pl.multiple_of((nc - 1 - i * HG_BATCH - j) * cs, cs), cs)
            started.append((
                fwd, _hgrn_chunk_start(q_ref[0, fwd, :], kf_ref[0, fwd, :], v_ref[0, fwd, :],
                                       gf_ref[0, fwd, :], False),
                bwd, _hgrn_chunk_start(q_ref[0, bwd, :], kb_ref[0, bwd, :], v_ref[0, bwd, :],
                                       gb_ref[0, bwd, :], True)))
        for fwd, parts_f, bwd, parts_b in started:
            of_ref[fwd, :], sf = _hgrn_chunk_finish(parts_f, sf, False)
            ob_ref[bwd, :], sb = _hgrn_chunk_finish(parts_b, sb, True)
        return sf, sb

    lax.fori_loop(0, nc // HG_BATCH, body, (s0f_ref[0, 0], s0b_ref[0, 0]))

    rb = min(n, 512)

    def readout(j, carry):
        sl = pl.ds(pl.multiple_of(j * rb, rb), rb)
        o = of_ref[sl, :] + ob_ref[sl, :]
        o = o * lax.rsqrt(jnp.mean(o * o, axis=-1, keepdims=True) + NORM_EPS) * nw_ref[...]
        y_ref[0, sl, :] = (o * _silu(zg_ref[0, sl, :].astype(F32))).astype(BF16)
        return carry

    lax.fori_loop(0, n // rb, readout, 0)


def _hgrn(q, kf, gf, kb, gb, v, zg, s0f, s0b, norm_w):
    b, n, _ = q.shape
    head = lambda bi, hi: (bi, 0, hi)
    st = lambda bi, hi: (bi, hi, 0, 0)
    seq = pl.BlockSpec((1, n, HG_DIM), head)
    state = pl.BlockSpec((1, 1, HG_DIM, HG_DIM), st)
    return pl.pallas_call(
        _hgrn_kernel,
        grid=(b, HG_HEADS),
        in_specs=[seq, seq, seq, seq, seq, seq, seq, state, state,
                  pl.BlockSpec((1, HG_DIM), lambda bi, hi: (0, hi))],
        out_specs=seq,
        out_shape=jax.ShapeDtypeStruct((b, n, HG_WIDTH), BF16),
        scratch_shapes=[pltpu.VMEM((n, HG_DIM), F32), pltpu.VMEM((n, HG_DIM), F32)],
        compiler_params=_params("arbitrary", "arbitrary"),
        name="hgrn2",
    )(q, kf, gf, kb, gb, v, zg, s0f, s0b, norm_w)


def _attn_kernel(sink_ref, q_ref, kp_ref, kc_ref, kn_ref, ksp_ref, ksc_ref, ksn_ref,
                 vp_ref, vc_ref, vn_ref, vsp_ref, vsc_ref, vsn_ref,
                 xk_ref, xks_ref, xv_ref, xvs_ref, y_ref):
    i = pl.program_id(1)
    nb = pl.num_programs(1)
    low = lax.broadcasted_iota(jnp.int32, (1, LANES), 1) < HEAD_DIM
    ctx_len = xk_ref.shape[1]
    rows = 2 * BAND
    ri = lax.broadcasted_iota(jnp.int32, (rows, BAND), 0) % BAND
    ci = lax.broadcasted_iota(jnp.int32, (rows, BAND), 1)
    ok_prev = (ci >= ri) & (i > 0)
    ok_next = (ci <= ri) & (i < nb - 1)
    p0, c0, n0 = ctx_len, ctx_len + BAND, ctx_len + 2 * BAND
    top = lax.broadcasted_iota(jnp.int32, (rows, 1), 0) < BAND

    def keys_of(refs, keep_low):
        x = jnp.concatenate([r[0] for r in refs], axis=0)
        return jnp.where(low if keep_low else ~low, x, jnp.zeros_like(x))

    group = Q_HEADS // KV_HEADS
    k_plain, k_swap = (xk_ref, kp_ref, kc_ref, kn_ref), (xks_ref, ksp_ref, ksc_ref, ksn_ref)
    v_plain, v_swap = (xv_ref, vp_ref, vc_ref, vn_ref), (xvs_ref, vsp_ref, vsc_ref, vsn_ref)
    scores = {}
    for kvh in range(KV_HEADS):
        tile = kvh * (group // 2)
        q = jnp.concatenate([q_ref[0, :, tile * LANES:(tile + 1) * LANES],
                             q_ref[0, :, (tile + 1) * LANES:(tile + 2) * LANES]], axis=0)
        for sub in range(2):
            in_low = sub == 0
            scores[kvh, sub] = _dot_nt(q, keys_of(k_plain if (kvh == 0) == in_low else k_swap, in_low))
    for kvh in range(KV_HEADS):
        tile = kvh * (group // 2)
        acc = jnp.zeros((rows, LANES), F32)
        for sub in range(2):
            in_low = sub == 0
            v = keys_of(v_plain if (kvh == 0) == in_low else v_swap, in_low)
            sink = jnp.where(top, sink_ref[group * kvh + sub], sink_ref[group * kvh + 2 + sub])
            s = scores[kvh, sub]
            s = jnp.concatenate([s[:, :p0], jnp.where(ok_prev, s[:, p0:c0], -jnp.inf), s[:, c0:n0],
                                 jnp.where(ok_next, s[:, n0:], -jnp.inf)], axis=1)
            m = jnp.maximum(jnp.max(s, axis=1, keepdims=True), sink)
            e = jnp.exp(s - m)
            denom = jnp.sum(e, axis=1, keepdims=True) + jnp.exp(sink - m)
            acc = acc + _dot(e.astype(BF16), v) / denom
        y_ref[0, :, tile * LANES:(tile + 1) * LANES] = acc[:BAND].astype(BF16)
        y_ref[0, :, (tile + 1) * LANES:(tile + 2) * LANES] = acc[BAND:].astype(BF16)


def _attn(sink, aq, ak, aks, av, avs, xk, xks, xv, xvs):
    b, n, _ = aq.shape
    nb = n // BAND
    cur = lambda bi, i: (bi, i, 0)
    prev = lambda bi, i: (bi, jnp.maximum(i - 1, 0), 0)
    nxt = lambda bi, i: (bi, jnp.minimum(i + 1, nb - 1), 0)
    per_b = lambda bi, i: (bi, 0, 0)
    kv = lambda f: pl.BlockSpec((1, BAND, KV_WIDTH), f)
    cx = pl.BlockSpec((1, xk.shape[1], KV_WIDTH), per_b)
    return pl.pallas_call(
        _attn_kernel,
        grid=(b, nb),
        in_specs=[pl.BlockSpec(memory_space=pltpu.SMEM),
                  pl.BlockSpec((1, BAND, ATT_WIDTH), cur),
                  kv(prev), kv(cur), kv(nxt), kv(prev), kv(cur), kv(nxt),
                  kv(prev), kv(cur), kv(nxt), kv(prev), kv(cur), kv(nxt),
                  cx, cx, cx, cx],
        out_specs=pl.BlockSpec((1, BAND, ATT_WIDTH), cur),
        out_shape=jax.ShapeDtypeStruct((b, n, ATT_WIDTH), BF16),
        compiler_params=_params("arbitrary", "arbitrary"),
        name="window_attn",
    )(sink, aq, ak, ak, ak, aks, aks, aks, av, av, av, avs, avs, avs, xk, xks, xv, xvs)


def _route(hf, wr_t, bias):
    tm = hf.shape[0]
    ne = wr_t.shape[0]
    h_hi = hf.astype(BF16)
    h_lo = (hf - h_hi.astype(F32)).astype(BF16)
    w_hi = wr_t.astype(BF16)
    w_lo = (wr_t - w_hi.astype(F32)).astype(BF16)
    first = _dot_nt(jnp.concatenate([w_hi, w_lo], axis=0), h_hi)
    scores = jax.nn.sigmoid(first[:ne] + first[ne:] + _dot_nt(w_hi, h_lo))
    sel = scores + bias
    grp = sel.reshape(N_GROUPS, GROUP_SIZE, tm)
    j = lax.broadcasted_iota(jnp.int32, grp.shape, 1)
    m1 = jnp.max(grp, axis=1, keepdims=True)
    first = jnp.min(jnp.where(grp == m1, j, GROUP_SIZE), axis=1, keepdims=True)
    m2 = jnp.max(jnp.where(j == first, -jnp.inf, grp), axis=1, keepdims=True)
    gs = (m1 + m2).reshape(N_GROUPS, tm)
    gi = lax.broadcasted_iota(jnp.int32, gs.shape, 0)
    rank = jnp.zeros(gs.shape, jnp.int32)
    for g in range(N_GROUPS):
        other = gs[g:g + 1]
        rank = rank + ((other > gs) | ((other == gs) & (g < gi))).astype(jnp.int32)
    gsel = rank < TOPK_GROUPS
    emask = jnp.broadcast_to(gsel[:, None, :], grp.shape).reshape(N_EXPERTS, tm)
    cand = jnp.where(emask, sel, -jnp.inf)
    ei = lax.broadcasted_iota(jnp.int32, cand.shape, 0)
    chosen = jnp.zeros(cand.shape, jnp.bool_)
    for _ in range(TOP_K):
        best = jnp.max(cand, axis=0, keepdims=True)
        first = jnp.min(jnp.where(cand == best, ei, N_EXPERTS), axis=0, keepdims=True)
        hit = ei == first
        chosen = chosen | hit
        cand = jnp.where(hit, -jnp.inf, cand)
    w = jnp.where(chosen, scores, 0.0)
    return w / jnp.sum(w, axis=0, keepdims=True) * ROUTED_SCALE, jnp.where(chosen, 1.0, 0.0)


def _sort_rows(chosen, gates_t):
    ne, ts = chosen.shape
    sel = chosen.astype(BF16)
    chosen = chosen > 0.5
    r = lax.broadcasted_iota(jnp.int32, (ts, ts), 0)
    c = lax.broadcasted_iota(jnp.int32, (ts, ts), 1)
    seen = _dot(sel, jnp.where(r <= c, 1.0, 0.0).astype(BF16))
    total = _dot(sel, jnp.ones((ts, ts), BF16))
    padded = jnp.floor((total + (SORT_UNIT - 1)) * (1.0 / SORT_UNIT)) * SORT_UNIT
    er = lax.broadcasted_iota(jnp.int32, (ne, ne), 0)
    ec = lax.broadcasted_iota(jnp.int32, (ne, ne), 1)
    before = jnp.where(ec < er, 1.0, 0.0).astype(BF16)
    start = _dot(before, padded.astype(BF16))
    choice = _dot(before, sel)
    row = start + seen - 1.0
    dest, gate = [], []
    for k in range(TOP_K):
        mk = chosen & (choice == k)
        dest.append(jnp.sum(jnp.where(mk, row, 0.0), axis=0, keepdims=True))
        gate.append(jnp.sum(jnp.where(mk, gates_t, 0.0), axis=0, keepdims=True))
    sel_pad = jnp.concatenate([sel, jnp.zeros((LANES - ne, ts), BF16)], axis=0)
    counts = _dot_nt(jnp.ones((8, ts), BF16), sel_pad)
    dest = jnp.concatenate(dest, axis=0)
    pad = jnp.zeros((LANES - TOP_K, ts), F32)
    dest_cols = jnp.concatenate([dest, pad], axis=0).T
    gate_cols = jnp.concatenate(gate + [pad], axis=0).T
    return dest.astype(jnp.int32), dest_cols.astype(jnp.int32), gate_cols, counts


def _outproj_kernel(x_ref, yh_ref, ya_ref, w_ref, g1_ref, sh_ref, sc_ref, lw_ref, lb_ref,
                    wr_ref, rb_ref, x1_ref, h2_ref, dest_ref, destc_ref, gatec_ref, cnt_ref):
    y = _dot(yh_ref[0], w_ref[:HG_WIDTH, :]) + _dot(ya_ref[0], w_ref[HG_WIDTH:, :])
    x1 = _ln_rows(ALPHA * x_ref[0] + g1_ref[0] * y, LN_EPS) * lw_ref[...] + lb_ref[...]
    x1_ref[0] = x1
    hf = _ln_rows(x1, NORM_EPS) * (1.0 + sc_ref[0]) + sh_ref[0]
    h2_ref[0] = hf.astype(BF16)
    gates_t, chosen = _route(hf, wr_ref[...], rb_ref[...])
    for s in range(hf.shape[0] // SORT_TILE):
        sl = slice(s * SORT_TILE, (s + 1) * SORT_TILE)
        dest_ref[s], destc_ref[s], gatec_ref[s], cnt_ref[s] = _sort_rows(chosen[:, sl], gates_t[:, sl])


def _outproj(x, yh, ya, w, g1, sh2, sc2, lw, lb, wr_t, rbias, tm):
    b, n, d = x.shape
    row = lambda bi, i: (bi, i, 0)
    per_b = lambda bi, i: (bi, 0, 0)
    const = lambda bi, i: (0, 0)
    nt = b * n // SORT_TILE
    per_step = tm // SORT_TILE
    tiles = lambda bi, i: (bi * (n // tm) + i, 0, 0)
    return pl.pallas_call(
        _outproj_kernel,
        grid=(b, n // tm),
        in_specs=[pl.BlockSpec((1, tm, d), row),
                  pl.BlockSpec((1, tm, HG_WIDTH), row), pl.BlockSpec((1, tm, ATT_WIDTH), row),
                  pl.BlockSpec(w.shape, const),
                  pl.BlockSpec((1, 1, d), per_b), pl.BlockSpec((1, 1, d), per_b),
                  pl.BlockSpec((1, 1, d), per_b),
                  pl.BlockSpec((1, d), const), pl.BlockSpec((1, d), const),
                  pl.BlockSpec(wr_t.shape, const), pl.BlockSpec(rbias.shape, const)],
        out_specs=[pl.BlockSpec((1, tm, d), row), pl.BlockSpec((1, tm, d), row),
                   pl.BlockSpec((per_step, TOP_K, SORT_TILE), tiles),
                   pl.BlockSpec((per_step, SORT_TILE, LANES), tiles),
                   pl.BlockSpec((per_step, SORT_TILE, LANES), tiles),
                   pl.BlockSpec((per_step, 8, LANES), tiles)],
        out_shape=[jax.ShapeDtypeStruct((b, n, d), F32), jax.ShapeDtypeStruct((b, n, d), BF16),
                   jax.ShapeDtypeStruct((nt, TOP_K, SORT_TILE), jnp.int32),
                   jax.ShapeDtypeStruct((nt, SORT_TILE, LANES), jnp.int32),
                   jax.ShapeDtypeStruct((nt, SORT_TILE, LANES), F32),
                   jax.ShapeDtypeStruct((nt, 8, LANES), F32)],
        compiler_params=_params("arbitrary", "arbitrary"),
        name="outproj_ln_router",
    )(x, yh, ya, w, g1, sh2, sc2, lw, lb, wr_t, rbias)


def _moe_layout(counts, max_tiles):
    n_pad = (counts + (SORT_UNIT - 1)) // SORT_UNIT * SORT_UNIT
    total = jnp.sum(n_pad, axis=0)
    region = (total + (EXPERT_TILE - 1)) // EXPERT_TILE * EXPERT_TILE
    base = jnp.cumsum(region) - region
    chunk_row = base[None, :] + jnp.cumsum(n_pad, axis=0) - n_pad
    tile_end = jnp.cumsum(region // EXPERT_TILE)
    n_used = tile_end[-1]
    j = jnp.minimum(jnp.arange(max_tiles, dtype=jnp.int32), n_used - 1)
    tile_expert = jnp.sum((tile_end[None, :] <= j[:, None]).astype(jnp.int32), axis=1)
    units = n_pad // SORT_UNIT
    local_row = jnp.cumsum(n_pad, axis=1) - n_pad
    rare = jnp.ones(units.shape, jnp.bool_)
    counts_out, lists = [], []
    for size, width in PIECE_LISTS[:-1]:
        mask = units == size
        rare = rare & ~mask
        slot = jnp.cumsum(mask, axis=1) - 1
        pick = mask[:, None, :] & (slot[:, None, :] == jnp.arange(width)[None, :, None])
        counts_out.append(jnp.sum(mask, axis=1))
        lists += [jnp.sum(jnp.where(pick, v[:, None, :], 0), axis=2) for v in (local_row, chunk_row)]
    single = jnp.where(rare, units, 0)
    single_end = jnp.cumsum(single, axis=1)
    u = jnp.arange(LOCAL_UNITS)[None, :, None]
    first = (single_end - single)[:, None, :]
    pick = (u >= first) & (u < single_end[:, None, :])
    counts_out.append(single_end[:, -1])
    lists += [jnp.sum(jnp.where(pick, v[:, None, :] + (u - first) * SORT_UNIT, 0), axis=2)
              for v in (local_row, chunk_row)]
    header = jnp.stack(counts_out + [jnp.sum(n_pad, axis=1)], axis=1)
    header = jnp.pad(header, ((0, 0), (0, TABLE_HEADER - header.shape[1])))
    table = jnp.concatenate([header] + lists, axis=1)
    table = jnp.pad(table, ((0, 0), (0, TABLE_WIDTH - table.shape[1]))).astype(jnp.int32)[:, None, :]
    tails = jnp.concatenate([base + total, (region - total) // SORT_UNIT]).astype(jnp.int32)
    return table, tails, tile_expert, n_used.astype(jnp.int32)[None]


def _row_cases(rows_used, fn):
    @pl.when(rows_used <= USUAL_ROWS)
    def _():
        fn(USUAL_ROWS)

    @pl.when(rows_used > USUAL_ROWS)
    def _():
        fn(LOCAL_ROWS)


def _unit(ref, row, units=1):
    return ref.at[pl.ds(pl.multiple_of(row, SORT_UNIT), units * SORT_UNIT), :]


def _for_each_piece(table_ref, fn):
    offset = TABLE_HEADER
    for c, (size, width) in enumerate(PIECE_LISTS):
        def body(j, carry, offset=offset, size=size, width=width):
            fn(table_ref[0, 0, offset + j], table_ref[0, 0, offset + width + j], size)
            return carry

        lax.fori_loop(0, table_ref[0, 0, c], body, 0)
        offset += 2 * width


def _await_units(units, wait_fn):
    def many(u, carry):
        wait_fn(WAIT_UNITS)
        return carry

    def single(u, carry):
        wait_fn(1)
        return carry

    lax.fori_loop(0, units // WAIT_UNITS, many, 0)
    lax.fori_loop(0, units % WAIT_UNITS, single, 0)


def _dispatch_kernel(chunks_ref, tails_ref, h_ref, dest_ref, xs_ref, buf_ref, zero_ref, pending_ref, sem):
    i = pl.program_id(0)
    slot = i % 2
    buf = buf_ref.at[slot]

    def drain(s):
        _await_units(pending_ref[s], lambda n: pltpu.make_async_copy(
            _unit(buf_ref.at[s], 0, n), _unit(xs_ref, 0, n), sem.at[s]).wait())

    @pl.when(i >= 2)
    def _():
        drain(slot)

    def permute(rows):
        r = lax.broadcasted_iota(jnp.int32, (rows, SORT_TILE), 0).astype(jnp.int16)
        p = jnp.zeros((rows, SORT_TILE), BF16)
        one = jnp.ones((rows, SORT_TILE), BF16)
        for k in range(TOP_K):
            hit = r == dest_ref[0, k:k + 1, :].astype(jnp.int16)
            p = jnp.where(hit, one, p)
        buf[0:rows, :] = _dot(p, h_ref[...]).astype(BF16)

    rows_used = chunks_ref[0, 0, TABLE_ROWS_USED]
    _row_cases(rows_used, permute)
    _for_each_piece(chunks_ref, lambda loc, glob, n: pltpu.make_async_copy(
        _unit(buf, loc, n), _unit(xs_ref, glob, n), sem.at[slot]).start())
    pending_ref[slot] = rows_used // SORT_UNIT

    @pl.when(i == pl.num_programs(0) - 1)
    def _():
        zero_ref[...] = jnp.zeros_like(zero_ref)

        def per_expert(e, total):
            n = tails_ref[N_EXPERTS + e]
            start = tails_ref[e]

            def per_unit(u, carry):
                pltpu.make_async_copy(zero_ref, _unit(xs_ref, start + u * SORT_UNIT), sem.at[2]).start()
                return carry

            lax.fori_loop(0, n, per_unit, 0)
            return total + n

        total = lax.fori_loop(0, N_EXPERTS, per_expert, 0)

        def drain_zero(u, carry):
            pltpu.make_async_copy(zero_ref, _unit(xs_ref, 0), sem.at[2]).wait()
            return carry

        lax.fori_loop(0, total, drain_zero, 0)

        @pl.when(i >= 1)
        def _():
            drain(1 - slot)

        drain(slot)


def _dispatch(chunks, tails, h2, dest, max_rows):
    t, d = h2.shape
    nt = t // SORT_TILE
    return pl.pallas_call(
        _dispatch_kernel,
        grid=(nt,),
        in_specs=[pl.BlockSpec((1, 1, TABLE_WIDTH), lambda i: (i, 0, 0), memory_space=pltpu.SMEM),
                  pl.BlockSpec(memory_space=pltpu.SMEM),
                  pl.BlockSpec((SORT_TILE, d), lambda i: (i, 0)),
                  pl.BlockSpec((1, TOP_K, SORT_TILE), lambda i: (i, 0, 0))],
        out_specs=pl.BlockSpec(memory_space=pl.ANY),
        out_shape=jax.ShapeDtypeStruct((max_rows, d), BF16),
        scratch_shapes=[pltpu.VMEM((2, LOCAL_ROWS, d), BF16), pltpu.VMEM((SORT_UNIT, d), BF16),
                        pltpu.SMEM((2,), jnp.int32), pltpu.SemaphoreType.DMA((3,))],
        compiler_params=_params("arbitrary"),
        name="moe_dispatch",
    )(chunks, tails, h2, dest)


def _expert_kernel(te_ref, nu_ref, xs_ref, wg_ref, wu_ref, wd_ref, ys_ref):
    @pl.when(pl.program_id(0) < nu_ref[0])
    def _():
        x = xs_ref[...]
        a = _silu(_dot(x, wg_ref[0].astype(BF16))) * _dot(x, wu_ref[0].astype(BF16))
        ys_ref[...] = _dot(a.astype(BF16), wd_ref[0].astype(BF16)).astype(BF16)


def _experts(tile_expert, n_used, xs, wg, wu, wd):
    rows, d = xs.shape
    ff = wg.shape[2]
    row = lambda j, te, nu: (jnp.minimum(j, jnp.maximum(nu[0] - 1, 0)), 0)
    exp = lambda j, te, nu: (te[j], 0, 0)
    return pl.pallas_call(
        _expert_kernel,
        grid_spec=pltpu.PrefetchScalarGridSpec(
            num_scalar_prefetch=2,
            grid=(rows // EXPERT_TILE,),
            in_specs=[pl.BlockSpec((EXPERT_TILE, d), row),
                      pl.BlockSpec((1, d, ff), exp), pl.BlockSpec((1, d, ff), exp),
                      pl.BlockSpec((1, ff, d), exp)],
            out_specs=pl.BlockSpec((EXPERT_TILE, d), row)),
        out_shape=jax.ShapeDtypeStruct((rows, d), BF16),
        compiler_params=_params("arbitrary"),
        name="moe_experts",
    )(tile_expert, n_used, xs, wg, wu, wd)


def _combine_kernel(chunks_ref, next_ref, ys_ref, dest_ref, gate_ref, h_ref, x1_ref, g2_ref, lw_ref, lb_ref,
                    swg_ref, swu_ref, swd_ref, o_ref, buf_ref, sem):
    i = pl.program_id(0)
    slot = i % 2

    def fetch(meta_ref, s):
        _for_each_piece(meta_ref, lambda loc, glob, n: pltpu.make_async_copy(
            _unit(ys_ref, glob, n), _unit(buf_ref.at[s], loc, n), sem.at[s]).start())

    @pl.when(i == 0)
    def _():
        buf_ref[...] = jnp.zeros_like(buf_ref)
        fetch(chunks_ref, 0)

    @pl.when(i + 1 < pl.num_programs(0))
    def _():
        fetch(next_ref, 1 - slot)

    rows_used = chunks_ref[0, 0, TABLE_ROWS_USED]
    _await_units(rows_used // SORT_UNIT, lambda n: pltpu.make_async_copy(
        _unit(ys_ref, 0, n), _unit(buf_ref.at[slot], 0, n), sem.at[slot]).wait())

    def unpermute(rows):
        h = h_ref[...]
        a = _silu(_dot(h, swg_ref[...])) * _dot(h, swu_ref[...])
        lane = lax.broadcasted_iota(jnp.int32, (SORT_TILE, rows), 1).astype(jnp.int16)
        p = jnp.zeros((SORT_TILE, rows), BF16)
        for k in range(TOP_K):
            hit = lane == dest_ref[0, :, k:k + 1].astype(jnp.int16)
            p = jnp.where(hit, jnp.broadcast_to(gate_ref[0, :, k:k + 1].astype(BF16), p.shape), p)
        ffn = _dot(a.astype(BF16), swd_ref[...]) + _dot(p, buf_ref[slot, 0:rows, :])
        u = ALPHA * x1_ref[...] + g2_ref[0] * ffn
        o_ref[...] = _ln_rows(u, LN_EPS) * lw_ref[...] + lb_ref[...]

    _row_cases(rows_used, unpermute)


def _combine(chunks, ys, dest, gate, h2, x1, g2, lw, lb, swg, swu, swd, tiles_per_batch):
    t, d = h2.shape
    ff = swg.shape[1]
    nt = t // SORT_TILE
    tile = lambda i: (i, 0, 0)
    nxt = lambda i: (jnp.minimum(i + 1, nt - 1), 0, 0)
    row = lambda i: (i, 0)
    const = lambda i: (0, 0)
    return pl.pallas_call(
        _combine_kernel,
        grid=(nt,),
        in_specs=[pl.BlockSpec((1, 1, TABLE_WIDTH), tile, memory_space=pltpu.SMEM),
                  pl.BlockSpec((1, 1, TABLE_WIDTH), nxt, memory_space=pltpu.SMEM),
                  pl.BlockSpec(memory_space=pl.ANY),
                  pl.BlockSpec((1, SORT_TILE, LANES), tile), pl.BlockSpec((1, SORT_TILE, LANES), tile),
                  pl.BlockSpec((SORT_TILE, d), row), pl.BlockSpec((SORT_TILE, d), row),
                  pl.BlockSpec((1, 1, d), lambda i: (i // tiles_per_batch, 0, 0)),
                  pl.BlockSpec((1, d), const), pl.BlockSpec((1, d), const),
                  pl.BlockSpec((d, ff), const), pl.BlockSpec((d, ff), const), pl.BlockSpec((ff, d), const)],
        out_specs=pl.BlockSpec((SORT_TILE, d), row),
        out_shape=jax.ShapeDtypeStruct((t, d), F32),
        scratch_shapes=[pltpu.VMEM((2, LOCAL_ROWS, d), BF16), pltpu.SemaphoreType.DMA((2,))],
        compiler_params=_params("arbitrary"),
        name="moe_combine",
    )(chunks, chunks, ys, dest, gate, h2, x1, g2, lw, lb, swg, swu, swd)


def _moe(h2, x1, dest, dest_cols, gate_cols, counts, g2, lw, lb, wg, wu, wd, swg, swu, swd):
    b, n, d = x1.shape
    t = b * n
    nt = t // SORT_TILE
    max_rows = t * TOP_K + nt * N_EXPERTS * (SORT_UNIT - 1) + N_EXPERTS * (EXPERT_TILE - SORT_UNIT)
    max_rows = -(-max_rows // EXPERT_TILE) * EXPERT_TILE
    chunks, tails, tile_expert, n_used = _moe_layout(
        counts[:, 0, :N_EXPERTS].astype(jnp.int32), max_rows // EXPERT_TILE)
    h2 = h2.reshape(t, d)
    xs = _dispatch(chunks, tails, h2, dest, max_rows)
    ys = _experts(tile_expert, n_used, xs, wg, wu, wd)
    out = _combine(chunks, ys, dest_cols, gate_cols, h2, x1.reshape(t, d), g2, lw, lb, swg, swu, swd,
                   n // SORT_TILE)
    return out.reshape(b, n, d)


def _swap_kv_heads(w):
    return jnp.concatenate([w[:, HEAD_DIM:], w[:, :HEAD_DIM]], axis=1)


def _split_w_in(w_in):
    bounds = np.cumsum([HG_WIDTH] * 5 + [ATT_WIDTH, KV_WIDTH])
    return jnp.split(w_in, [int(v) for v in bounds], axis=1)


def _context_weight(w_in):
    _, zff, zfb, zi, _, _, ak, av = _split_w_in(w_in)
    return jnp.concatenate([zff, zfb, zi, ak, _swap_kv_heads(ak), av, _swap_kv_heads(av)],
                           axis=1).astype(BF16)


def _rope_tables(n):
    pos = jnp.arange(n)
    freqs = ROPE_BASE ** (-jnp.arange(ROT_PAIRS, dtype=F32) / ROT_PAIRS)
    ang_row = (pos // GRID_W).astype(F32)[:, None] * freqs
    ang_col = (pos % GRID_W).astype(F32)[:, None] * freqs
    cos = jnp.concatenate([jnp.cos(ang_row)] * 2 + [jnp.cos(ang_col)] * 2, axis=1)
    sin = jnp.concatenate([-jnp.sin(ang_row), jnp.sin(ang_row), -jnp.sin(ang_col), jnp.sin(ang_col)], axis=1)
    reps = LANES // HEAD_DIM
    return jnp.concatenate([cos] * reps, axis=1), jnp.concatenate([sin] * reps, axis=1)


def kernel(x, c, ctx, c_ctx, w_ada, b_ada, w_in, hg_lb_fwd, hg_lb_bwd, hg_norm_w, attn_sink, w_out, ln1_w, ln1_b, router_w, router_bias, exp_w_gate, exp_w_up, exp_w_down, shared_w_gate, shared_w_up, shared_w_down, ln2_w, ln2_b):
    b, n, d = x.shape
    layer = 0
    rows = -(-(b + 1) // 8) * 8
    cc = jnp.zeros((rows, d), F32).at[:b].set(c).at[b].set(c_ctx)
    mod = _ada(cc, w_ada[layer], b_ada[layer][None, :])
    sh1, sc1, g1, sh2, sc2, g2 = [m[:, None, :] for m in jnp.split(mod[:b], 6, axis=1)]
    csh1, csc1 = mod[b:b + 1, :d], mod[b:b + 1, d:2 * d]

    lb_f = jnp.cumsum(jax.nn.softmax(hg_lb_fwd.astype(F32), axis=0), axis=0)[layer][None, :]
    lb_b = jnp.cumsum(jax.nn.softmax(hg_lb_bwd.astype(F32), axis=0), axis=0)[layer][None, :]
    cos, sin = _rope_tables(n)

    xk, xks, xv, xvs, s0f, s0b = _ctx(ctx, csh1, csc1, _context_weight(w_in[layer]), lb_f, lb_b)
    tm = min(n, 512)
    (q, kf, gf, kb, gb, v, zg, aq, ak, aks, av, avs) = _inproj(
        x, sh1, sc1, w_in[layer].astype(BF16), lb_f, lb_b, cos, sin, tm)
    yh = _hgrn(q, kf, gf, kb, gb, v, zg, s0f, s0b, hg_norm_w[layer][None, :])
    ya = _attn(attn_sink[layer], aq, ak, aks, av, avs, xk, xks, xv, xvs)

    x1, h2, dest, dest_cols, gate_cols, counts = _outproj(
        x, yh, ya, w_out[layer].astype(BF16), g1, sh2, sc2, ln1_w[layer][None, :], ln1_b[layer][None, :],
        router_w[layer].T, router_bias[layer][:, None], tm)
    return _moe(h2, x1, dest, dest_cols, gate_cols, counts, g2, ln2_w[layer][None, :], ln2_b[layer][None, :],
                exp_w_gate[layer], exp_w_up[layer], exp_w_down[layer], shared_w_gate[layer].astype(BF16),
                shared_w_up[layer].astype(BF16), shared_w_down[layer].astype(BF16))
```

```python
import jax
import jax.numpy as jnp
import numpy as np
from jax import lax
from jax.experimental import pallas as pl
from jax.experimental.pallas import tpu as pltpu

F32 = jnp.float32
BF16 = jnp.bfloat16
HIGHEST = lax.Precision.HIGHEST

DEPTH = 1
GRID_W = 64
HG_WIDTH = 512
HG_HEADS = 4
HG_DIM = 128
HG_CHUNK = 64
HG_SUB = 16
HG_BATCH = 4
LOG_F_MIN = -4.0
HEAD_DIM = 64
Q_HEADS = 8
KV_HEADS = 2
ATT_WIDTH = Q_HEADS * HEAD_DIM
KV_WIDTH = KV_HEADS * HEAD_DIM
BAND = 128
ROPE_BASE = 10000.0
ROT_PAIRS = HEAD_DIM // 4
N_EXPERTS = 64
TOP_K = 8
N_GROUPS = 8
GROUP_SIZE = N_EXPERTS // N_GROUPS
TOPK_GROUPS = 4
ROUTED_SCALE = 2.5
LN_EPS = 1e-5
NORM_EPS = 1e-6
ALPHA = (2.0 * DEPTH) ** 0.25

LANES = 128
BF16_SUBLANES = 16
MXU_DIM = 256

SORT_TILE = MXU_DIM
SORT_UNIT = BF16_SUBLANES
EXPERT_TILE = 2048
_LOCAL_WORST = SORT_TILE * TOP_K + N_EXPERTS * (SORT_UNIT - 1)
LOCAL_ROWS = -(-_LOCAL_WORST // MXU_DIM) * MXU_DIM
USUAL_ROWS = SORT_TILE * TOP_K + 2 * MXU_DIM
LOCAL_UNITS = LOCAL_ROWS // SORT_UNIT
COMMON_UNITS = (2, 3)
PIECE_LISTS = tuple((size, N_EXPERTS) for size in COMMON_UNITS) + ((1, LOCAL_UNITS),)
TABLE_HEADER = 8
TABLE_ROWS_USED = len(PIECE_LISTS)
TABLE_WIDTH = -(-(TABLE_HEADER + 2 * sum(width for _, width in PIECE_LISTS)) // LANES) * LANES
WAIT_UNITS = 8
V7X_VMEM_LIMIT_BYTES = 56 * 1024 * 1024


def _params(*sem):
    return pltpu.CompilerParams(dimension_semantics=sem, vmem_limit_bytes=V7X_VMEM_LIMIT_BYTES)


def _ln_rows(x, eps):
    mu = jnp.mean(x, axis=-1, keepdims=True)
    xc = x - mu
    return xc * lax.rsqrt(jnp.mean(xc * xc, axis=-1, keepdims=True) + eps)


def _silu(x):
    return x * jax.nn.sigmoid(x)


def _dot(a, b):
    return jnp.dot(a, b, preferred_element_type=F32)


def _dot_nt(a, b, precision=None):
    return lax.dot_general(a, b, (((1,), (1,)), ((), ())), precision=precision,
                           preferred_element_type=F32)


def _dot_tn(a, b):
    return lax.dot_general(a, b, (((0,), (0,)), ((), ())), preferred_element_type=F32)


def _forget_gate(z, lb):
    f = lb + (1.0 - lb) * jax.nn.sigmoid(z)
    return 1.0 - f, jnp.maximum(jnp.log(f), LOG_F_MIN)


def _chunk_scan(g, reverse):
    n = g.shape[0]
    pos = lax.broadcasted_iota(jnp.int32, g.shape, 0) % HG_CHUNK
    step = 1
    while step < HG_CHUNK:
        if reverse:
            g = g + jnp.where(pos < HG_CHUNK - step, pltpu.roll(g, n - step, axis=0), 0.0)
        else:
            g = g + jnp.where(pos >= step, pltpu.roll(g, step, axis=0), 0.0)
        step *= 2
    return g


def _ada_kernel(c_ref, w_ref, b_ref, o_ref):
    c = c_ref[...]
    o_ref[...] = jnp.dot(_silu(c), w_ref[...], precision=HIGHEST,
                         preferred_element_type=F32) + b_ref[...]


def _ada(cc, w, b):
    rows, d = cc.shape
    cols = w.shape[1]
    tn = 512
    return pl.pallas_call(
        _ada_kernel,
        grid=(cols // tn,),
        in_specs=[pl.BlockSpec((rows, d), lambda j: (0, 0)),
                  pl.BlockSpec((d, tn), lambda j: (0, j)),
                  pl.BlockSpec((1, tn), lambda j: (0, j))],
        out_specs=pl.BlockSpec((rows, tn), lambda j: (0, j)),
        out_shape=jax.ShapeDtypeStruct((rows, cols), F32),
        compiler_params=_params("arbitrary"),
        name="adaln",
    )(cc, w, b)


_C_Q, _C_FF, _C_FB, _C_I, _C_G = 0, 512, 1024, 1536, 2048
_C_AQ, _C_AK, _C_AV = 2560, 3072, 3200


def _swap_rotary_halves(x):
    first = lax.broadcasted_iota(jnp.int32, (1, LANES), 1) % (2 * ROT_PAIRS) < ROT_PAIRS
    tiles = []
    for t in range(x.shape[1] // LANES):
        xt = x[:, t * LANES:(t + 1) * LANES]
        tiles.append(jnp.where(first, pltpu.roll(xt, LANES - ROT_PAIRS, axis=1),
                               pltpu.roll(xt, ROT_PAIRS, axis=1)))
    return jnp.concatenate(tiles, axis=1)


def _inproj_kernel(x_ref, sh_ref, sc_ref, w_ref, lbf_ref, lbb_ref, cos_ref, sin_ref,
                   q_ref, kf_ref, gf_ref, kb_ref, gb_ref, v_ref, zg_ref,
                   aq_ref, ak_ref, aks_ref, av_ref, avs_ref):
    h = (_ln_rows(x_ref[0], NORM_EPS) * (1.0 + sc_ref[0]) + sh_ref[0]).astype(BF16)

    def proj(lo, n):
        return _dot(h, w_ref[:, lo:lo + n])

    zq, zff, zfb = proj(_C_Q, HG_WIDTH), proj(_C_FF, HG_WIDTH), proj(_C_FB, HG_WIDTH)
    zi, zg = proj(_C_I, HG_WIDTH), proj(_C_G, HG_WIDTH)
    aq, ak, av = proj(_C_AQ, ATT_WIDTH), proj(_C_AK, KV_WIDTH), proj(_C_AV, KV_WIDTH)

    q_ref[0] = zq.astype(BF16)
    k, g = _forget_gate(zff, lbf_ref[...])
    kf_ref[0] = k.astype(BF16)
    gf_ref[0] = _chunk_scan(g, False)
    k, g = _forget_gate(zfb, lbb_ref[...])
    kb_ref[0] = k.astype(BF16)
    gb_ref[0] = _chunk_scan(g, True)
    v_ref[0] = zi.astype(BF16)
    zg_ref[0] = zg.astype(BF16)
    cos = cos_ref[...]
    sin = sin_ref[...]
    cos4 = jnp.concatenate([cos] * (ATT_WIDTH // LANES), axis=1)
    sin4 = jnp.concatenate([sin] * (ATT_WIDTH // LANES), axis=1)
    scale = HEAD_DIM ** -0.5
    aq_ref[0] = ((aq * cos4 + _swap_rotary_halves(aq) * sin4) * scale).astype(BF16)
    ak = ak * cos + _swap_rotary_halves(ak) * sin
    ak_ref[0] = ak.astype(BF16)
    aks_ref[0] = pltpu.roll(ak, HEAD_DIM, axis=1).astype(BF16)
    av_ref[0] = av.astype(BF16)
    avs_ref[0] = pltpu.roll(av, HEAD_DIM, axis=1).astype(BF16)


def _inproj(x, sh, sc, w, lbf, lbb, cos, sin, tm):
    b, n, d = x.shape
    row = lambda bi, i: (bi, i, 0)
    per_b = lambda bi, i: (bi, 0, 0)
    const = lambda bi, i: (0, 0)
    tab = lambda bi, i: (i, 0)

    def out(width, dtype):
        return jax.ShapeDtypeStruct((b, n, width), dtype), pl.BlockSpec((1, tm, width), row)

    outs = [out(HG_WIDTH, BF16), out(HG_WIDTH, BF16), out(HG_WIDTH, F32), out(HG_WIDTH, BF16),
            out(HG_WIDTH, F32), out(HG_WIDTH, BF16), out(HG_WIDTH, BF16),
            out(ATT_WIDTH, BF16), out(KV_WIDTH, BF16), out(KV_WIDTH, BF16),
            out(KV_WIDTH, BF16), out(KV_WIDTH, BF16)]
    return pl.pallas_call(
        _inproj_kernel,
        grid=(b, n // tm),
        in_specs=[pl.BlockSpec((1, tm, d), row),
                  pl.BlockSpec((1, 1, d), per_b), pl.BlockSpec((1, 1, d), per_b),
                  pl.BlockSpec(w.shape, const),
                  pl.BlockSpec((1, HG_WIDTH), const), pl.BlockSpec((1, HG_WIDTH), const),
                  pl.BlockSpec((tm, LANES), tab), pl.BlockSpec((tm, LANES), tab)],
        out_specs=[o[1] for o in outs],
        out_shape=[o[0] for o in outs],
        compiler_params=_params("arbitrary", "arbitrary"),
        name="latent_inproj",
    )(x, sh, sc, w, lbf, lbb, cos, sin)


_X_FF, _X_FB, _X_I, _X_AK, _X_AKS, _X_AV, _X_AVS, _X_TOTAL = 0, 512, 1024, 1536, 1664, 1792, 1920, 2048


def _ctx_kernel(c_ref, sh_ref, sc_ref, w_ref, lbf_ref, lbb_ref,
                k_ref, ks_ref, v_ref, vs_ref, sf_ref, sb_ref):
    h = (_ln_rows(c_ref[0], NORM_EPS) * (1.0 + sc_ref[...]) + sh_ref[...]).astype(BF16)

    def proj(lo, n):
        return _dot(h, w_ref[:, lo:lo + n])

    k_ref[0] = proj(_X_AK, KV_WIDTH).astype(BF16)
    ks_ref[0] = proj(_X_AKS, KV_WIDTH).astype(BF16)
    v_ref[0] = proj(_X_AV, KV_WIDTH).astype(BF16)
    vs_ref[0] = proj(_X_AVS, KV_WIDTH).astype(BF16)

    kf, gf = _forget_gate(proj(_X_FF, HG_WIDTH), lbf_ref[...])
    kb, gb = _forget_gate(proj(_X_FB, HG_WIDTH), lbb_ref[...])
    vi = proj(_X_I, HG_WIDTH).astype(BF16)
    n = h.shape[0]
    r = lax.broadcasted_iota(jnp.int32, (n, n), 0)
    c = lax.broadcasted_iota(jnp.int32, (n, n), 1)
    bf = jnp.dot((c <= r).astype(F32), gf, precision=HIGHEST, preferred_element_type=F32)
    bb = jnp.dot((c >= r).astype(F32), gb, precision=HIGHEST, preferred_element_type=F32)
    kdf = (kf * jnp.exp(bf[n - 1:n] - bf)).astype(BF16)
    kdb = (kb * jnp.exp(bb[0:1] - bb)).astype(BF16)
    for hd in range(HG_HEADS):
        sl = slice(hd * HG_DIM, (hd + 1) * HG_DIM)
        sf_ref[0, hd] = _dot_tn(vi[:, sl], kdf[:, sl])
        sb_ref[0, hd] = _dot_tn(vi[:, sl], kdb[:, sl])


def _ctx(ctx, sh, sc, w, lbf, lbb):
    b, n, d = ctx.shape
    per_b = lambda bi: (bi, 0, 0)
    const = lambda bi: (0, 0)
    kv = (jax.ShapeDtypeStruct((b, n, KV_WIDTH), BF16), pl.BlockSpec((1, n, KV_WIDTH), per_b))
    st = (jax.ShapeDtypeStruct((b, HG_HEADS, HG_DIM, HG_DIM), F32),
          pl.BlockSpec((1, HG_HEADS, HG_DIM, HG_DIM), lambda bi: (bi, 0, 0, 0)))
    outs = [kv, kv, kv, kv, st, st]
    return pl.pallas_call(
        _ctx_kernel,
        grid=(b,),
        in_specs=[pl.BlockSpec((1, n, d), per_b),
                  pl.BlockSpec((1, d), const), pl.BlockSpec((1, d), const),
                  pl.BlockSpec(w.shape, const),
                  pl.BlockSpec((1, HG_WIDTH), const), pl.BlockSpec((1, HG_WIDTH), const)],
        out_specs=[o[1] for o in outs],
        out_shape=[o[0] for o in outs],
        compiler_params=_params("arbitrary"),
        name="context_side",
    )(ctx, sh, sc, w, lbf, lbb)


def _hgrn_chunk_start(q, k, v, b, reverse):
    cs, us = HG_CHUNK, HG_SUB
    ns = cs // us
    last = 0 if reverse else cs - 1
    b_last = b[last:last + 1]
    qf = q.astype(F32)
    kf = k.astype(F32)
    q_ref_rows, k_blocks = [], []
    for s in range(ns):
        if reverse:
            keys = slice(cs - us * (s + 1), cs)
            ref = b[cs - us * s:cs - us * s + 1] if s > 0 else jnp.zeros_like(b_last)
        else:
            keys = slice(0, us * (s + 1))
            ref = b[us * s - 1:us * s] if s > 0 else jnp.zeros_like(b_last)
        q_ref_rows.append(jnp.broadcast_to(ref, (us, HG_DIM)))
        kh = (kf[keys] * jnp.exp(ref - b[keys])).astype(BF16)
        pad = jnp.zeros((cs - us * (s + 1), HG_DIM), BF16)
        if pad.shape[0]:
            kh = jnp.concatenate([pad, kh] if reverse else [kh, pad], axis=0)
        k_blocks.append(kh)
    q_ref = jnp.concatenate(q_ref_rows[::-1] if reverse else q_ref_rows, axis=0)
    qh = qf * jnp.exp(b - q_ref)
    sub = lax.broadcasted_iota(jnp.int32, (cs, HG_DIM), 0) // us
    if reverse:
        sub = ns - 1 - sub
    q_cat = jnp.concatenate([jnp.where(sub == s, qh, 0.0).astype(BF16) for s in range(ns)], axis=1)
    att = _dot_nt(q_cat, jnp.concatenate(k_blocks, axis=1))
    kdec = (kf * jnp.exp(b_last - b)).astype(BF16)
    return att, (qf * jnp.exp(b)).astype(BF16), v, jnp.exp(b_last), _dot_tn(v, kdec)


def _hgrn_chunk_finish(parts, st, reverse):
    att, q_dec, v, decay, update = parts
    ri = lax.broadcasted_iota(jnp.int32, att.shape, 0)
    ci = lax.broadcasted_iota(jnp.int32, att.shape, 1)
    att = jnp.where((ci >= ri) if reverse else (ci <= ri), att, 0.0)
    o = _dot(att.astype(BF16), v) + _dot_nt(q_dec, st.astype(BF16))
    return o, st * decay + update


def _hgrn_kernel(q_ref, kf_ref, gf_ref, kb_ref, gb_ref, v_ref, zg_ref, s0f_ref, s0b_ref, nw_ref,
                 y_ref, of_ref, ob_ref):
    n = q_ref.shape[1]
    cs = HG_CHUNK
    nc = n // cs

    def body(i, carry):
        sf, sb = carry
        started = []
        for j in range(HG_BATCH):
            fwd = pl.ds(pl.multiple_of((i * HG_BATCH + j) * cs, cs), cs)
            bwd = pl.ds(pl.multiple_of((nc - 1 - i * HG_BATCH - j) * cs, cs), cs)
            started.append((
                fwd, _hgrn_chunk_start(q_ref[0, fwd, :], kf_ref[0, fwd, :], v_ref[0, fwd, :],
                                       gf_ref[0, fwd, :], False),
                bwd, _hgrn_chunk_start(q_ref[0, bwd, :], kb_ref[0, bwd, :], v_ref[0, bwd, :],
                                       gb_ref[0, bwd, :], True)))
        for fwd, parts_f, bwd, parts_b in started:
            of_ref[fwd, :], sf = _hgrn_chunk_finish(parts_f, sf, False)
            ob_ref[bwd, :], sb = _hgrn_chunk_finish(parts_b, sb, True)
        return sf, sb

    lax.fori_loop(0, nc // HG_BATCH, body, (s0f_ref[0, 0], s0b_ref[0, 0]))

    rb = min(n, 512)

    def readout(j, carry):
        sl = pl.ds(pl.multiple_of(j * rb, rb), rb)
        o = of_ref[sl, :] + ob_ref[sl, :]
        o = o * lax.rsqrt(jnp.mean(o * o, axis=-1, keepdims=True) + NORM_EPS) * nw_ref[...]
        y_ref[0, sl, :] = (o * _silu(zg_ref[0, sl, :].astype(F32))).astype(BF16)
        return carry

    lax.fori_loop(0, n // rb, readout, 0)


def _hgrn(q, kf, gf, kb, gb, v, zg, s0f, s0b, norm_w):
    b, n, _ = q.shape
    head = lambda bi, hi: (bi, 0, hi)
    st = lambda bi, hi: (bi, hi, 0, 0)
    seq = pl.BlockSpec((1, n, HG_DIM), head)
    state = pl.BlockSpec((1, 1, HG_DIM, HG_DIM), st)
    return pl.pallas_call(
        _hgrn_kernel,
        grid=(b, HG_HEADS),
        in_specs=[seq, seq, seq, seq, seq, seq, seq, state, state,
                  pl.BlockSpec((1, HG_DIM), lambda bi, hi: (0, hi))],
        out_specs=seq,
        out_shape=jax.ShapeDtypeStruct((b, n, HG_WIDTH), BF16),
        scratch_shapes=[pltpu.VMEM((n, HG_DIM), F32), pltpu.VMEM((n, HG_DIM), F32)],
        compiler_params=_params("arbitrary", "arbitrary"),
        name="hgrn2",
    )(q, kf, gf, kb, gb, v, zg, s0f, s0b, norm_w)


def _attn_kernel(sink_ref, q_ref, kp_ref, kc_ref, kn_ref, ksp_ref, ksc_ref, ksn_ref,
                 vp_ref, vc_ref, vn_ref, vsp_ref, vsc_ref, vsn_ref,
                 xk_ref, xks_ref, xv_ref, xvs_ref, y_ref):
    i = pl.program_id(1)
    nb = pl.num_programs(1)
    low = lax.broadcasted_iota(jnp.int32, (1, LANES), 1) < HEAD_DIM
    ctx_len = xk_ref.shape[1]
    rows = 2 * BAND
    ri = lax.broadcasted_iota(jnp.int32, (rows, BAND), 0) % BAND
    ci = lax.broadcasted_iota(jnp.int32, (rows, BAND), 1)
    ok_prev = (ci >= ri) & (i > 0)
    ok_next = (ci <= ri) & (i < nb - 1)
    p0, c0, n0 = ctx_len, ctx_len + BAND, ctx_len + 2 * BAND
    top = lax.broadcasted_iota(jnp.int32, (rows, 1), 0) < BAND

    def keys_of(refs, keep_low):
        x = jnp.concatenate([r[0] for r in refs], axis=0)
        return jnp.where(low if keep_low else ~low, x, jnp.zeros_like(x))

    group = Q_HEADS // KV_HEADS
    k_plain, k_swap = (xk_ref, kp_ref, kc_ref, kn_ref), (xks_ref, ksp_ref, ksc_ref, ksn_ref)
    v_plain, v_swap = (xv_ref, vp_ref, vc_ref, vn_ref), (xvs_ref, vsp_ref, vsc_ref, vsn_ref)
    scores = {}
    for kvh in range(KV_HEADS):
        tile = kvh * (group // 2)
        q = jnp.concatenate([q_ref[0, :, tile * LANES:(tile + 1) * LANES],
                             q_ref[0, :, (tile + 1) * LANES:(tile + 2) * LANES]], axis=0)
        for sub in range(2):
            in_low = sub == 0
            scores[kvh, sub] = _dot_nt(q, keys_of(k_plain if (kvh == 0) == in_low else k_swap, in_low))
    for kvh in range(KV_HEADS):
        tile = kvh * (group // 2)
        acc = jnp.zeros((rows, LANES), F32)
        for sub in range(2):
            in_low = sub == 0
            v = keys_of(v_plain if (kvh == 0) == in_low else v_swap, in_low)
            sink = jnp.where(top, sink_ref[group * kvh + sub], sink_ref[group * kvh + 2 + sub])
            s = scores[kvh, sub]
            s = jnp.concatenate([s[:, :p0], jnp.where(ok_prev, s[:, p0:c0], -jnp.inf), s[:, c0:n0],
                                 jnp.where(ok_next, s[:, n0:], -jnp.inf)], axis=1)
            m = jnp.maximum(jnp.max(s, axis=1, keepdims=True), sink)
            e = jnp.exp(s - m)
            denom = jnp.sum(e, axis=1, keepdims=True) + jnp.exp(sink - m)
            acc = acc + _dot(e.astype(BF16), v) / denom
        y_ref[0, :, tile * LANES:(tile + 1) * LANES] = acc[:BAND].astype(BF16)
        y_ref[0, :, (tile + 1) * LANES:(tile + 2) * LANES] = acc[BAND:].astype(BF16)


def _attn(sink, aq, ak, aks, av, avs, xk, xks, xv, xvs):
    b, n, _ = aq.shape
    nb = n // BAND
    cur = lambda bi, i: (bi, i, 0)
    prev = lambda bi, i: (bi, jnp.maximum(i - 1, 0), 0)
    nxt = lambda bi, i: (bi, jnp.minimum(i + 1, nb - 1), 0)
    per_b = lambda bi, i: (bi, 0, 0)
    kv = lambda f: pl.BlockSpec((1, BAND, KV_WIDTH), f)
    cx = pl.BlockSpec((1, xk.shape[1], KV_WIDTH), per_b)
    return pl.pallas_call(
        _attn_kernel,
        grid=(b, nb),
        in_specs=[pl.BlockSpec(memory_space=pltpu.SMEM),
                  pl.BlockSpec((1, BAND, ATT_WIDTH), cur),
                  kv(prev), kv(cur), kv(nxt), kv(prev), kv(cur), kv(nxt),
                  kv(prev), kv(cur), kv(nxt), kv(prev), kv(cur), kv(nxt),
                  cx, cx, cx, cx],
        out_specs=pl.BlockSpec((1, BAND, ATT_WIDTH), cur),
        out_shape=jax.ShapeDtypeStruct((b, n, ATT_WIDTH), BF16),
        compiler_params=_params("arbitrary", "arbitrary"),
        name="window_attn",
    )(sink, aq, ak, ak, ak, aks, aks, aks, av, av, av, avs, avs, avs, xk, xks, xv, xvs)


def _route(hf, wr_t, bias):
    tm = hf.shape[0]
    ne = wr_t.shape[0]
    h_hi = hf.astype(BF16)
    h_lo = (hf - h_hi.astype(F32)).astype(BF16)
    w_hi = wr_t.astype(BF16)
    w_lo = (wr_t - w_hi.astype(F32)).astype(BF16)
    first = _dot_nt(jnp.concatenate([w_hi, w_lo], axis=0), h_hi)
    scores = jax.nn.sigmoid(first[:ne] + first[ne:] + _dot_nt(w_hi, h_lo))
    sel = scores + bias
    grp = sel.reshape(N_GROUPS, GROUP_SIZE, tm)
    j = lax.broadcasted_iota(jnp.int32, grp.shape, 1)
    m1 = jnp.max(grp, axis=1, keepdims=True)
    first = jnp.min(jnp.where(grp == m1, j, GROUP_SIZE), axis=1, keepdims=True)
    m2 = jnp.max(jnp.where(j == first, -jnp.inf, grp), axis=1, keepdims=True)
    gs = (m1 + m2).reshape(N_GROUPS, tm)
    gi = lax.broadcasted_iota(jnp.int32, gs.shape, 0)
    rank = jnp.zeros(gs.shape, jnp.int32)
    for g in range(N_GROUPS):
        other = gs[g:g + 1]
        rank = rank + ((other > gs) | ((other == gs) & (g < gi))).astype(jnp.int32)
    gsel = rank < TOPK_GROUPS
    emask = jnp.broadcast_to(gsel[:, None, :], grp.shape).reshape(N_EXPERTS, tm)
    cand = jnp.where(emask, sel, -jnp.inf)
    ei = lax.broadcasted_iota(jnp.int32, cand.shape, 0)
    chosen = jnp.zeros(cand.shape, jnp.bool_)
    for _ in range(TOP_K):
        best = jnp.max(cand, axis=0, keepdims=True)
        first = jnp.min(jnp.where(cand == best, ei, N_EXPERTS), axis=0, keepdims=True)
        hit = ei == first
        chosen = chosen | hit
        cand = jnp.where(hit, -jnp.inf, cand)
    w = jnp.where(chosen, scores, 0.0)
    return w / jnp.sum(w, axis=0, keepdims=True) * ROUTED_SCALE, jnp.where(chosen, 1.0, 0.0)


def _sort_rows(chosen, gates_t):
    ne, ts = chosen.shape
    sel = chosen.astype(BF16)
    chosen = chosen > 0.5
    r = lax.broadcasted_iota(jnp.int32, (ts, ts), 0)
    c = lax.broadcasted_iota(jnp.int32, (ts, ts), 1)
    seen = _dot(sel, jnp.where(r <= c, 1.0, 0.0).astype(BF16))
    total = _dot(sel, jnp.ones((ts, ts), BF16))
    padded = jnp.floor((total + (SORT_UNIT - 1)) * (1.0 / SORT_UNIT)) * SORT_UNIT
    er = lax.broadcasted_iota(jnp.int32, (ne, ne), 0)
    ec = lax.broadcasted_iota(jnp.int32, (ne, ne), 1)
    before = jnp.where(ec < er, 1.0, 0.0).astype(BF16)
    start = _dot(before, padded.astype(BF16))
    choice = _dot(before, sel)
    row = start + seen - 1.0
    dest, gate = [], []
    for k in range(TOP_K):
        mk = chosen & (choice == k)
        dest.append(jnp.sum(jnp.where(mk, row, 0.0), axis=0, keepdims=True))
        gate.append(jnp.sum(jnp.where(mk, gates_t, 0.0), axis=0, keepdims=True))
    sel_pad = jnp.concatenate([sel, jnp.zeros((LANES - ne, ts), BF16)], axis=0)
    counts = _dot_nt(jnp.ones((8, ts), BF16), sel_pad)
    dest = jnp.concatenate(dest, axis=0)
    pad = jnp.zeros((LANES - TOP_K, ts), F32)
    dest_cols = jnp.concatenate([dest, pad], axis=0).T
    gate_cols = jnp.concatenate(gate + [pad], axis=0).T
    return dest.astype(jnp.int32), dest_cols.astype(jnp.int32), gate_cols, counts


def _outproj_kernel(x_ref, yh_ref, ya_ref, w_ref, g1_ref, sh_ref, sc_ref, lw_ref, lb_ref,
                    wr_ref, rb_ref, x1_ref, h2_ref, dest_ref, destc_ref, gatec_ref, cnt_ref):
    y = _dot(yh_ref[0], w_ref[:HG_WIDTH, :]) + _dot(ya_ref[0], w_ref[HG_WIDTH:, :])
    x1 = _ln_rows(ALPHA * x_ref[0] + g1_ref[0] * y, LN_EPS) * lw_ref[...] + lb_ref[...]
    x1_ref[0] = x1
    hf = _ln_rows(x1, NORM_EPS) * (1.0 + sc_ref[0]) + sh_ref[0]
    h2_ref[0] = hf.astype(BF16)
    gates_t, chosen = _route(hf, wr_ref[...], rb_ref[...])
    for s in range(hf.shape[0] // SORT_TILE):
        sl = slice(s * SORT_TILE, (s + 1) * SORT_TILE)
        dest_ref[s], destc_ref[s], gatec_ref[s], cnt_ref[s] = _sort_rows(chosen[:, sl], gates_t[:, sl])


def _outproj(x, yh, ya, w, g1, sh2, sc2, lw, lb, wr_t, rbias, tm):
    b, n, d = x.shape
    row = lambda bi, i: (bi, i, 0)
    per_b = lambda bi, i: (bi, 0, 0)
    const = lambda bi, i: (0, 0)
    nt = b * n // SORT_TILE
    per_step = tm // SORT_TILE
    tiles = lambda bi, i: (bi * (n // tm) + i, 0, 0)
    return pl.pallas_call(
        _outproj_kernel,
        grid=(b, n // tm),
        in_specs=[pl.BlockSpec((1, tm, d), row),
                  pl.BlockSpec((1, tm, HG_WIDTH), row), pl.BlockSpec((1, tm, ATT_WIDTH), row),
                  pl.BlockSpec(w.shape, const),
                  pl.BlockSpec((1, 1, d), per_b), pl.BlockSpec((1, 1, d), per_b),
                  pl.BlockSpec((1, 1, d), per_b),
                  pl.BlockSpec((1, d), const), pl.BlockSpec((1, d), const),
                  pl.BlockSpec(wr_t.shape, const), pl.BlockSpec(rbias.shape, const)],
        out_specs=[pl.BlockSpec((1, tm, d), row), pl.BlockSpec((1, tm, d), row),
                   pl.BlockSpec((per_step, TOP_K, SORT_TILE), tiles),
                   pl.BlockSpec((per_step, SORT_TILE, LANES), tiles),
                   pl.BlockSpec((per_step, SORT_TILE, LANES), tiles),
                   pl.BlockSpec((per_step, 8, LANES), tiles)],
        out_shape=[jax.ShapeDtypeStruct((b, n, d), F32), jax.ShapeDtypeStruct((b, n, d), BF16),
                   jax.ShapeDtypeStruct((nt, TOP_K, SORT_TILE), jnp.int32),
                   jax.ShapeDtypeStruct((nt, SORT_TILE, LANES), jnp.int32),
                   jax.ShapeDtypeStruct((nt, SORT_TILE, LANES), F32),
                   jax.ShapeDtypeStruct((nt, 8, LANES), F32)],
        compiler_params=_params("arbitrary", "arbitrary"),
        name="outproj_ln_router",
    )(x, yh, ya, w, g1, sh2, sc2, lw, lb, wr_t, rbias)


def _moe_layout(counts, max_tiles):
    n_pad = (counts + (SORT_UNIT - 1)) // SORT_UNIT * SORT_UNIT
    total = jnp.sum(n_pad, axis=0)
    region = (total + (EXPERT_TILE - 1)) // EXPERT_TILE * EXPERT_TILE
    base = jnp.cumsum(region) - region
    chunk_row = base[None, :] + jnp.cumsum(n_pad, axis=0) - n_pad
    tile_end = jnp.cumsum(region // EXPERT_TILE)
    n_used = tile_end[-1]
    j = jnp.minimum(jnp.arange(max_tiles, dtype=jnp.int32), n_used - 1)
    tile_expert = jnp.sum((tile_end[None, :] <= j[:, None]).astype(jnp.int32), axis=1)
    units = n_pad // SORT_UNIT
    local_row = jnp.cumsum(n_pad, axis=1) - n_pad
    rare = jnp.ones(units.shape, jnp.bool_)
    counts_out, lists = [], []
    for size, width in PIECE_LISTS[:-1]:
        mask = units == size
        rare = rare & ~mask
        slot = jnp.cumsum(mask, axis=1) - 1
        pick = mask[:, None, :] & (slot[:, None, :] == jnp.arange(width)[None, :, None])
        counts_out.append(jnp.sum(mask, axis=1))
        lists += [jnp.sum(jnp.where(pick, v[:, None, :], 0), axis=2) for v in (local_row, chunk_row)]
    single = jnp.where(rare, units, 0)
    single_end = jnp.cumsum(single, axis=1)
    u = jnp.arange(LOCAL_UNITS)[None, :, None]
    first = (single_end - single)[:, None, :]
    pick = (u >= first) & (u < single_end[:, None, :])
    counts_out.append(single_end[:, -1])
    lists += [jnp.sum(jnp.where(pick, v[:, None, :] + (u - first) * SORT_UNIT, 0), axis=2)
              for v in (local_row, chunk_row)]
    header = jnp.stack(counts_out + [jnp.sum(n_pad, axis=1)], axis=1)
    header = jnp.pad(header, ((0, 0), (0, TABLE_HEADER - header.shape[1])))
    table = jnp.concatenate([header] + lists, axis=1)
    table = jnp.pad(table, ((0, 0), (0, TABLE_WIDTH - table.shape[1]))).astype(jnp.int32)[:, None, :]
    tails = jnp.concatenate([base + total, (region - total) // SORT_UNIT]).astype(jnp.int32)
    return table, tails, tile_expert, n_used.astype(jnp.int32)[None]


def _row_cases(rows_used, fn):
    @pl.when(rows_used <= USUAL_ROWS)
    def _():
        fn(USUAL_ROWS)

    @pl.when(rows_used > USUAL_ROWS)
    def _():
        fn(LOCAL_ROWS)


def _unit(ref, row, units=1):
    return ref.at[pl.ds(pl.multiple_of(row, SORT_UNIT), units * SORT_UNIT), :]


def _for_each_piece(table_ref, fn):
    offset = TABLE_HEADER
    for c, (size, width) in enumerate(PIECE_LISTS):
        def body(j, carry, offset=offset, size=size, width=width):
            fn(table_ref[0, 0, offset + j], table_ref[0, 0, offset + width + j], size)
            return carry

        lax.fori_loop(0, table_ref[0, 0, c], body, 0)
        offset += 2 * width


def _await_units(units, wait_fn):
    def many(u, carry):
        wait_fn(WAIT_UNITS)
        return carry

    def single(u, carry):
        wait_fn(1)
        return carry

    lax.fori_loop(0, units // WAIT_UNITS, many, 0)
    lax.fori_loop(0, units % WAIT_UNITS, single, 0)


def _dispatch_kernel(chunks_ref, tails_ref, h_ref, dest_ref, xs_ref, buf_ref, zero_ref, pending_ref, sem):
    i = pl.program_id(0)
    slot = i % 2
    buf = buf_ref.at[slot]

    def drain(s):
        _await_units(pending_ref[s], lambda n: pltpu.make_async_copy(
            _unit(buf_ref.at[s], 0, n), _unit(xs_ref, 0, n), sem.at[s]).wait())

    @pl.when(i >= 2)
    def _():
        drain(slot)

    def permute(rows):
        r = lax.broadcasted_iota(jnp.int32, (rows, SORT_TILE), 0).astype(jnp.int16)
        p = jnp.zeros((rows, SORT_TILE), BF16)
        one = jnp.ones((rows, SORT_TILE), BF16)
        for k in range(TOP_K):
            hit = r == dest_ref[0, k:k + 1, :].astype(jnp.int16)
            p = jnp.where(hit, one, p)
        buf[0:rows, :] = _dot(p, h_ref[...]).astype(BF16)

    rows_used = chunks_ref[0, 0, TABLE_ROWS_USED]
    _row_cases(rows_used, permute)
    _for_each_piece(chunks_ref, lambda loc, glob, n: pltpu.make_async_copy(
        _unit(buf, loc, n), _unit(xs_ref, glob, n), sem.at[slot]).start())
    pending_ref[slot] = rows_used // SORT_UNIT

    @pl.when(i == pl.num_programs(0) - 1)
    def _():
        zero_ref[...] = jnp.zeros_like(zero_ref)

        def per_expert(e, total):
            n = tails_ref[N_EXPERTS + e]
            start = tails_ref[e]

            def many(u, carry):
                pltpu.make_async_copy(zero_ref, _unit(xs_ref, start + u * (WAIT_UNITS * SORT_UNIT), WAIT_UNITS),
                                      sem.at[2]).start()
                return carry

            def single(u, carry):
                row = start + (n // WAIT_UNITS * WAIT_UNITS + u) * SORT_UNIT
                pltpu.make_async_copy(_unit(zero_ref, 0), _unit(xs_ref, row), sem.at[2]).start()
                return carry

            lax.fori_loop(0, n // WAIT_UNITS, many, 0)
            lax.fori_loop(0, n % WAIT_UNITS, single, 0)
            return total + n

        total = lax.fori_loop(0, N_EXPERTS, per_expert, 0)
        _await_units(total, lambda n: pltpu.make_async_copy(
            _unit(zero_ref, 0, n), _unit(xs_ref, 0, n), sem.at[2]).wait())

        @pl.when(i >= 1)
        def _():
            drain(1 - slot)

        drain(slot)


def _dispatch(chunks, tails, h2, dest, max_rows):
    t, d = h2.shape
    nt = t // SORT_TILE
    return pl.pallas_call(
        _dispatch_kernel,
        grid=(nt,),
        in_specs=[pl.BlockSpec((1, 1, TABLE_WIDTH), lambda i: (i, 0, 0), memory_space=pltpu.SMEM),
                  pl.BlockSpec(memory_space=pltpu.SMEM),
                  pl.BlockSpec((SORT_TILE, d), lambda i: (i, 0)),
                  pl.BlockSpec((1, TOP_K, SORT_TILE), lambda i: (i, 0, 0))],
        out_specs=pl.BlockSpec(memory_space=pl.ANY),
        out_shape=jax.ShapeDtypeStruct((max_rows, d), BF16),
        scratch_shapes=[pltpu.VMEM((2, LOCAL_ROWS, d), BF16), pltpu.VMEM((WAIT_UNITS * SORT_UNIT, d), BF16),
                        pltpu.SMEM((2,), jnp.int32), pltpu.SemaphoreType.DMA((3,))],
        compiler_params=_params("arbitrary"),
        name="moe_dispatch",
    )(chunks, tails, h2, dest)


def _expert_kernel(te_ref, nu_ref, xs_ref, wg_ref, wu_ref, wd_ref, ys_ref):
    @pl.when(pl.program_id(0) < nu_ref[0])
    def _():
        x = xs_ref[...]
        a = _silu(_dot(x, wg_ref[0].astype(BF16))) * _dot(x, wu_ref[0].astype(BF16))
        ys_ref[...] = _dot(a.astype(BF16), wd_ref[0].astype(BF16)).astype(BF16)


def _experts(tile_expert, n_used, xs, wg, wu, wd):
    rows, d = xs.shape
    ff = wg.shape[2]
    row = lambda j, te, nu: (jnp.minimum(j, jnp.maximum(nu[0] - 1, 0)), 0)
    exp = lambda j, te, nu: (te[j], 0, 0)
    return pl.pallas_call(
        _expert_kernel,
        grid_spec=pltpu.PrefetchScalarGridSpec(
            num_scalar_prefetch=2,
            grid=(rows // EXPERT_TILE,),
            in_specs=[pl.BlockSpec((EXPERT_TILE, d), row),
                      pl.BlockSpec((1, d, ff), exp), pl.BlockSpec((1, d, ff), exp),
                      pl.BlockSpec((1, ff, d), exp)],
            out_specs=pl.BlockSpec((EXPERT_TILE, d), row)),
        out_shape=jax.ShapeDtypeStruct((rows, d), BF16),
        compiler_params=_params("arbitrary"),
        name="moe_experts",
    )(tile_expert, n_used, xs, wg, wu, wd)


def _combine_kernel(chunks_ref, next_ref, ys_ref, dest_ref, gate_ref, h_ref, x1_ref, g2_ref, lw_ref, lb_ref,
                    swg_ref, swu_ref, swd_ref, o_ref, buf_ref, sem):
    i = pl.program_id(0)
    slot = i % 2

    def fetch(meta_ref, s):
        _for_each_piece(meta_ref, lambda loc, glob, n: pltpu.make_async_copy(
            _unit(ys_ref, glob, n), _unit(buf_ref.at[s], loc, n), sem.at[s]).start())

    @pl.when(i == 0)
    def _():
        buf_ref[...] = jnp.zeros_like(buf_ref)
        fetch(chunks_ref, 0)

    @pl.when(i + 1 < pl.num_programs(0))
    def _():
        fetch(next_ref, 1 - slot)

    rows_used = chunks_ref[0, 0, TABLE_ROWS_USED]
    _await_units(rows_used // SORT_UNIT, lambda n: pltpu.make_async_copy(
        _unit(ys_ref, 0, n), _unit(buf_ref.at[slot], 0, n), sem.at[slot]).wait())

    def unpermute(rows):
        h = h_ref[...]
        a = _silu(_dot(h, swg_ref[...])) * _dot(h, swu_ref[...])
        lane = lax.broadcasted_iota(jnp.int32, (SORT_TILE, rows), 1).astype(jnp.int16)
        p = jnp.zeros((SORT_TILE, rows), BF16)
        for k in range(TOP_K):
            hit = lane == dest_ref[0, :, k:k + 1].astype(jnp.int16)
            p = jnp.where(hit, jnp.broadcast_to(gate_ref[0, :, k:k + 1].astype(BF16), p.shape), p)
        ffn = _dot(a.astype(BF16), swd_ref[...]) + _dot(p, buf_ref[slot, 0:rows, :])
        u = ALPHA * x1_ref[...] + g2_ref[0] * ffn
        o_ref[...] = _ln_rows(u, LN_EPS) * lw_ref[...] + lb_ref[...]

    _row_cases(rows_used, unpermute)


def _combine(chunks, ys, dest, gate, h2, x1, g2, lw, lb, swg, swu, swd, tiles_per_batch):
    t, d = h2.shape
    ff = swg.shape[1]
    nt = t // SORT_TILE
    tile = lambda i: (i, 0, 0)
    nxt = lambda i: (jnp.minimum(i + 1, nt - 1), 0, 0)
    row = lambda i: (i, 0)
    const = lambda i: (0, 0)
    return pl.pallas_call(
        _combine_kernel,
        grid=(nt,),
        in_specs=[pl.BlockSpec((1, 1, TABLE_WIDTH), tile, memory_space=pltpu.SMEM),
                  pl.BlockSpec((1, 1, TABLE_WIDTH), nxt, memory_space=pltpu.SMEM),
                  pl.BlockSpec(memory_space=pl.ANY),
                  pl.BlockSpec((1, SORT_TILE, LANES), tile), pl.BlockSpec((1, SORT_TILE, LANES), tile),
                  pl.BlockSpec((SORT_TILE, d), row), pl.BlockSpec((SORT_TILE, d), row),
                  pl.BlockSpec((1, 1, d), lambda i: (i // tiles_per_batch, 0, 0)),
                  pl.BlockSpec((1, d), const), pl.BlockSpec((1, d), const),
                  pl.BlockSpec((d, ff), const), pl.BlockSpec((d, ff), const), pl.BlockSpec((ff, d), const)],
        out_specs=pl.BlockSpec((SORT_TILE, d), row),
        out_shape=jax.ShapeDtypeStruct((t, d), F32),
        scratch_shapes=[pltpu.VMEM((2, LOCAL_ROWS, d), BF16), pltpu.SemaphoreType.DMA((2,))],
        compiler_params=_params("arbitrary"),
        name="moe_combine",
    )(chunks, chunks, ys, dest, gate, h2, x1, g2, lw, lb, swg, swu, swd)


def _moe(h2, x1, dest, dest_cols, gate_cols, counts, g2, lw, lb, wg, wu, wd, swg, swu, swd):
    b, n, d = x1.shape
    t = b * n
    nt = t // SORT_TILE
    max_rows = t * TOP_K + nt * N_EXPERTS * (SORT_UNIT - 1) + N_EXPERTS * (EXPERT_TILE - SORT_UNIT)
    max_rows = -(-max_rows // EXPERT_TILE) * EXPERT_TILE
    chunks, tails, tile_expert, n_used = _moe_layout(
        counts[:, 0, :N_EXPERTS].astype(jnp.int32), max_rows // EXPERT_TILE)
    h2 = h2.reshape(t, d)
    xs = _dispatch(chunks, tails, h2, dest, max_rows)
    ys = _experts(tile_expert, n_used, xs, wg, wu, wd)
    out = _combine(chunks, ys, dest_cols, gate_cols, h2, x1.reshape(t, d), g2, lw, lb, swg, swu, swd,
                   n // SORT_TILE)
    return out.reshape(b, n, d)


def _swap_kv_heads(w):
    return jnp.concatenate([w[:, HEAD_DIM:], w[:, :HEAD_DIM]], axis=1)


def _split_w_in(w_in):
    bounds = np.cumsum([HG_WIDTH] * 5 + [ATT_WIDTH, KV_WIDTH])
    return jnp.split(w_in, [int(v) for v in bounds], axis=1)


def _context_weight(w_in):
    _, zff, zfb, zi, _, _, ak, av = _split_w_in(w_in)
    return jnp.concatenate([zff, zfb, zi, ak, _swap_kv_heads(ak), av, _swap_kv_heads(av)],
                           axis=1).astype(BF16)


def _rope_tables(n):
    pos = np.arange(n)
    freqs = np.float32(ROPE_BASE) ** (-np.arange(ROT_PAIRS, dtype=np.float32) / np.float32(ROT_PAIRS))
    ang_row = (pos // GRID_W).astype(np.float32)[:, None] * freqs
    ang_col = (pos % GRID_W).astype(np.float32)[:, None] * freqs
    cos = np.concatenate([np.cos(ang_row)] * 2 + [np.cos(ang_col)] * 2, axis=1)
    sin = np.concatenate([-np.sin(ang_row), np.sin(ang_row), -np.sin(ang_col), np.sin(ang_col)], axis=1)
    reps = LANES // HEAD_DIM
    return (jnp.asarray(np.tile(cos, (1, reps)), dtype=F32), jnp.asarray(np.tile(sin, (1, reps)), dtype=F32))


def kernel(x, c, ctx, c_ctx, w_ada, b_ada, w_in, hg_lb_fwd, hg_lb_bwd, hg_norm_w, attn_sink, w_out, ln1_w, ln1_b, router_w, router_bias, exp_w_gate, exp_w_up, exp_w_down, shared_w_gate, shared_w_up, shared_w_down, ln2_w, ln2_b):
    b, n, d = x.shape
    layer = 0
    rows = -(-(b + 1) // 8) * 8
    cc = jnp.zeros((rows, d), F32).at[:b].set(c).at[b].set(c_ctx)
    mod = _ada(cc, w_ada[layer], b_ada[layer][None, :])
    sh1, sc1, g1, sh2, sc2, g2 = [m[:, None, :] for m in jnp.split(mod[:b], 6, axis=1)]
    csh1, csc1 = mod[b:b + 1, :d], mod[b:b + 1, d:2 * d]

    lb_f = jnp.cumsum(jax.nn.softmax(hg_lb_fwd.astype(F32), axis=0), axis=0)[layer][None, :]
    lb_b = jnp.cumsum(jax.nn.softmax(hg_lb_bwd.astype(F32), axis=0), axis=0)[layer][None, :]
    cos, sin = _rope_tables(n)

    xk, xks, xv, xvs, s0f, s0b = _ctx(ctx, csh1, csc1, _context_weight(w_in[layer]), lb_f, lb_b)
    tm = min(n, 512)
    (q, kf, gf, kb, gb, v, zg, aq, ak, aks, av, avs) = _inproj(
        x, sh1, sc1, w_in[layer].astype(BF16), lb_f, lb_b, cos, sin, tm)
    yh = _hgrn(q, kf, gf, kb, gb, v, zg, s0f, s0b, hg_norm_w[layer][None, :])
    ya = _attn(attn_sink[layer], aq, ak, aks, av, avs, xk, xks, xv, xvs)

    x1, h2, dest, dest_cols, gate_cols, counts = _outproj(
        x, yh, ya, w_out[layer].astype(BF16), g1, sh2, sc2, ln1_w[layer][None, :], ln1_b[layer][None, :],
        router_w[layer].T, router_bias[layer][:, None], tm)
    return _moe(h2, x1, dest, dest_cols, gate_cols, counts, g2, ln2_w[layer][None, :], ln2_b[layer][None, :],
                exp_w_gate[layer], exp_w_up[layer], exp_w_down[layer], shared_w_gate[layer].astype(BF16),
                shared_w_up[layer].astype(BF16), shared_w_down[layer].astype(BF16))
```

```python
import math

import jax
import jax.numpy as jnp
import numpy as np
from jax import lax
from jax.experimental import pallas as pl
from jax.experimental.pallas import tpu as pltpu

F32 = jnp.float32
BF16 = jnp.bfloat16
HIGHEST = lax.Precision.HIGHEST

DEPTH = 1
GRID_W = 64
HG_WIDTH = 512
HG_HEADS = 4
HG_DIM = 128
HG_CHUNK = 64
HG_SUB = 16
HG_BATCH = 8
LOG_F_MIN = -4.0
HEAD_DIM = 64
Q_HEADS = 8
KV_HEADS = 2
ATT_WIDTH = Q_HEADS * HEAD_DIM
KV_WIDTH = KV_HEADS * HEAD_DIM
BAND = 128
ROPE_BASE = 10000.0
ROT_PAIRS = HEAD_DIM // 4
N_EXPERTS = 64
TOP_K = 8
N_GROUPS = 8
GROUP_SIZE = N_EXPERTS // N_GROUPS
TOPK_GROUPS = 4
ROUTED_SCALE = 2.5
LN_EPS = 1e-5
NORM_EPS = 1e-6
ALPHA = (2.0 * DEPTH) ** 0.25

LANES = 128
BF16_SUBLANES = 16
MXU_DIM = 256

SORT_TILE = MXU_DIM
SORT_UNIT = BF16_SUBLANES
EXPERT_TILE = 2048
_LOCAL_WORST = SORT_TILE * TOP_K + N_EXPERTS * (SORT_UNIT - 1)
LOCAL_ROWS = -(-_LOCAL_WORST // MXU_DIM) * MXU_DIM
USUAL_ROWS = SORT_TILE * TOP_K + 2 * MXU_DIM
LOCAL_UNITS = LOCAL_ROWS // SORT_UNIT
COMMON_UNITS = (2, 3)
PIECE_LISTS = tuple((size, N_EXPERTS) for size in COMMON_UNITS) + ((1, LOCAL_UNITS),)
TABLE_HEADER = 8
TABLE_ROWS_USED = len(PIECE_LISTS)
TABLE_WIDTH = -(-(TABLE_HEADER + 2 * sum(width for _, width in PIECE_LISTS)) // LANES) * LANES
WAIT_UNITS = 8
V7X_VMEM_LIMIT_BYTES = 56 * 1024 * 1024


def _params(*sem):
    return pltpu.CompilerParams(dimension_semantics=sem, vmem_limit_bytes=V7X_VMEM_LIMIT_BYTES)


def _ln_rows(x, eps):
    mu = jnp.mean(x, axis=-1, keepdims=True)
    xc = x - mu
    return xc * lax.rsqrt(jnp.mean(xc * xc, axis=-1, keepdims=True) + eps)


def _silu(x):
    return x * jax.nn.sigmoid(x)


def _dot(a, b):
    return jnp.dot(a, b, preferred_element_type=F32)


def _dot_nt(a, b, precision=None):
    return lax.dot_general(a, b, (((1,), (1,)), ((), ())), precision=precision,
                           preferred_element_type=F32)


def _dot_tn(a, b):
    return lax.dot_general(a, b, (((0,), (0,)), ((), ())), preferred_element_type=F32)


def _forget_gate(z, lb):
    f = lb + (1.0 - lb) * jax.nn.sigmoid(z)
    return 1.0 - f, jnp.maximum(jnp.log(f), LOG_F_MIN)


def _chunk_scan(g, reverse):
    n = g.shape[0]
    pos = lax.broadcasted_iota(jnp.int32, g.shape, 0) % HG_CHUNK
    step = 1
    while step < HG_CHUNK:
        if reverse:
            g = g + jnp.where(pos < HG_CHUNK - step, pltpu.roll(g, n - step, axis=0), 0.0)
        else:
            g = g + jnp.where(pos >= step, pltpu.roll(g, step, axis=0), 0.0)
        step *= 2
    return g


def _ada_kernel(c_ref, w_ref, b_ref, o_ref):
    c = c_ref[...]
    o_ref[...] = jnp.dot(_silu(c), w_ref[...], precision=HIGHEST,
                         preferred_element_type=F32) + b_ref[...]


def _ada(cc, w, b):
    rows, d = cc.shape
    cols = w.shape[1]
    tn = 512
    return pl.pallas_call(
        _ada_kernel,
        grid=(cols // tn,),
        in_specs=[pl.BlockSpec((rows, d), lambda j: (0, 0)),
                  pl.BlockSpec((d, tn), lambda j: (0, j)),
                  pl.BlockSpec((1, tn), lambda j: (0, j))],
        out_specs=pl.BlockSpec((rows, tn), lambda j: (0, j)),
        out_shape=jax.ShapeDtypeStruct((rows, cols), F32),
        compiler_params=_params("arbitrary"),
        name="adaln",
    )(cc, w, b)


INPROJ_PART = 512
_C_Q, _C_FF, _C_FB, _C_I, _C_G = 0, 512, 1024, 1536, 2048
_C_AQ, _C_AK, _C_AV = 2560, 3072, 3200


def _swap_rotary_halves(x):
    first = lax.broadcasted_iota(jnp.int32, (1, LANES), 1) % (2 * ROT_PAIRS) < ROT_PAIRS
    tiles = []
    for t in range(x.shape[1] // LANES):
        xt = x[:, t * LANES:(t + 1) * LANES]
        tiles.append(jnp.where(first, pltpu.roll(xt, LANES - ROT_PAIRS, axis=1),
                               pltpu.roll(xt, ROT_PAIRS, axis=1)))
    return jnp.concatenate(tiles, axis=1)


def _inproj_kernel(x_ref, sh_ref, sc_ref, w_ref, lbf_ref, lbb_ref, cos_ref, sin_ref,
                   q_ref, kf_ref, gf_ref, kb_ref, gb_ref, v_ref, zg_ref,
                   aq_ref, ak_ref, aks_ref, av_ref, avs_ref):
    tm = x_ref.shape[1]
    part = min(tm, INPROJ_PART)
    started = []
    for p in range(tm // part):
        rows = slice(p * part, (p + 1) * part)
        h = (_ln_rows(x_ref[0, rows, :], NORM_EPS) * (1.0 + sc_ref[0]) + sh_ref[0]).astype(BF16)
        widths = ((_C_Q, HG_WIDTH), (_C_FF, HG_WIDTH), (_C_FB, HG_WIDTH), (_C_I, HG_WIDTH), (_C_G, HG_WIDTH),
                  (_C_AQ, ATT_WIDTH), (_C_AK, KV_WIDTH), (_C_AV, KV_WIDTH))
        started.append((rows, [_dot(h, w_ref[:, lo:lo + n]) for lo, n in widths]))

    scale = HEAD_DIM ** -0.5
    for rows, (zq, zff, zfb, zi, zg, aq, ak, av) in started:
        q_ref[0, rows, :] = zq.astype(BF16)
        k, g = _forget_gate(zff, lbf_ref[...])
        kf_ref[0, rows, :] = k.astype(BF16)
        gf_ref[0, rows, :] = _chunk_scan(g, False)
        k, g = _forget_gate(zfb, lbb_ref[...])
        kb_ref[0, rows, :] = k.astype(BF16)
        gb_ref[0, rows, :] = _chunk_scan(g, True)
        v_ref[0, rows, :] = zi.astype(BF16)
        zg_ref[0, rows, :] = zg.astype(BF16)
        cos = cos_ref[rows, :]
        sin = sin_ref[rows, :]
        cos4 = jnp.concatenate([cos] * (ATT_WIDTH // LANES), axis=1)
        sin4 = jnp.concatenate([sin] * (ATT_WIDTH // LANES), axis=1)
        aq_ref[0, rows, :] = ((aq * cos4 + _swap_rotary_halves(aq) * sin4) * scale).astype(BF16)
        ak = ak * cos + _swap_rotary_halves(ak) * sin
        ak_ref[0, rows, :] = ak.astype(BF16)
        aks_ref[0, rows, :] = pltpu.roll(ak, HEAD_DIM, axis=1).astype(BF16)
        av_ref[0, rows, :] = av.astype(BF16)
        avs_ref[0, rows, :] = pltpu.roll(av, HEAD_DIM, axis=1).astype(BF16)


def _inproj(x, sh, sc, w, lbf, lbb, cos, sin, tm):
    b, n, d = x.shape
    row = lambda bi, i: (bi, i, 0)
    per_b = lambda bi, i: (bi, 0, 0)
    const = lambda bi, i: (0, 0)
    tab = lambda bi, i: (i, 0)

    def out(width, dtype):
        return jax.ShapeDtypeStruct((b, n, width), dtype), pl.BlockSpec((1, tm, width), row)

    outs = [out(HG_WIDTH, BF16), out(HG_WIDTH, BF16), out(HG_WIDTH, F32), out(HG_WIDTH, BF16),
            out(HG_WIDTH, F32), out(HG_WIDTH, BF16), out(HG_WIDTH, BF16),
            out(ATT_WIDTH, BF16), out(KV_WIDTH, BF16), out(KV_WIDTH, BF16),
            out(KV_WIDTH, BF16), out(KV_WIDTH, BF16)]
    return pl.pallas_call(
        _inproj_kernel,
        grid=(b, n // tm),
        in_specs=[pl.BlockSpec((1, tm, d), row),
                  pl.BlockSpec((1, 1, d), per_b), pl.BlockSpec((1, 1, d), per_b),
                  pl.BlockSpec(w.shape, const, pipeline_mode=pl.Buffered(1)),
                  pl.BlockSpec((1, HG_WIDTH), const), pl.BlockSpec((1, HG_WIDTH), const),
                  pl.BlockSpec((tm, LANES), tab), pl.BlockSpec((tm, LANES), tab)],
        out_specs=[o[1] for o in outs],
        out_shape=[o[0] for o in outs],
        compiler_params=_params("arbitrary", "arbitrary"),
        name="latent_inproj",
    )(x, sh, sc, w, lbf, lbb, cos, sin)


_X_FF, _X_FB, _X_I, _X_AK, _X_AKS, _X_AV, _X_AVS, _X_TOTAL = 0, 512, 1024, 1536, 1664, 1792, 1920, 2048


def _ctx_kernel(c_ref, sh_ref, sc_ref, w_ref, lbf_ref, lbb_ref,
                k_ref, ks_ref, v_ref, vs_ref, sf_ref, sb_ref):
    h = (_ln_rows(c_ref[0], NORM_EPS) * (1.0 + sc_ref[...]) + sh_ref[...]).astype(BF16)

    def proj(lo, n):
        return _dot(h, w_ref[:, lo:lo + n])

    k_ref[0] = proj(_X_AK, KV_WIDTH).astype(BF16)
    ks_ref[0] = proj(_X_AKS, KV_WIDTH).astype(BF16)
    v_ref[0] = proj(_X_AV, KV_WIDTH).astype(BF16)
    vs_ref[0] = proj(_X_AVS, KV_WIDTH).astype(BF16)

    kf, gf = _forget_gate(proj(_X_FF, HG_WIDTH), lbf_ref[...])
    kb, gb = _forget_gate(proj(_X_FB, HG_WIDTH), lbb_ref[...])
    vi = proj(_X_I, HG_WIDTH).astype(BF16)
    n = h.shape[0]
    r = lax.broadcasted_iota(jnp.int32, (n, n), 0)
    c = lax.broadcasted_iota(jnp.int32, (n, n), 1)
    bf = jnp.dot((c <= r).astype(F32), gf, precision=HIGHEST, preferred_element_type=F32)
    bb = jnp.dot((c >= r).astype(F32), gb, precision=HIGHEST, preferred_element_type=F32)
    kdf = (kf * jnp.exp(bf[n - 1:n] - bf)).astype(BF16)
    kdb = (kb * jnp.exp(bb[0:1] - bb)).astype(BF16)
    for hd in range(HG_HEADS):
        sl = slice(hd * HG_DIM, (hd + 1) * HG_DIM)
        sf_ref[0, hd] = _dot_tn(vi[:, sl], kdf[:, sl])
        sb_ref[0, hd] = _dot_tn(vi[:, sl], kdb[:, sl])


def _ctx(ctx, sh, sc, w, lbf, lbb):
    b, n, d = ctx.shape
    per_b = lambda bi: (bi, 0, 0)
    const = lambda bi: (0, 0)
    kv = (jax.ShapeDtypeStruct((b, n, KV_WIDTH), BF16), pl.BlockSpec((1, n, KV_WIDTH), per_b))
    st = (jax.ShapeDtypeStruct((b, HG_HEADS, HG_DIM, HG_DIM), F32),
          pl.BlockSpec((1, HG_HEADS, HG_DIM, HG_DIM), lambda bi: (bi, 0, 0, 0)))
    outs = [kv, kv, kv, kv, st, st]
    return pl.pallas_call(
        _ctx_kernel,
        grid=(b,),
        in_specs=[pl.BlockSpec((1, n, d), per_b),
                  pl.BlockSpec((1, d), const), pl.BlockSpec((1, d), const),
                  pl.BlockSpec(w.shape, const),
                  pl.BlockSpec((1, HG_WIDTH), const), pl.BlockSpec((1, HG_WIDTH), const)],
        out_specs=[o[1] for o in outs],
        out_shape=[o[0] for o in outs],
        compiler_params=_params("arbitrary"),
        name="context_side",
    )(ctx, sh, sc, w, lbf, lbb)


def _hgrn_chunk_start(q, k, v, b, reverse):
    cs, us = HG_CHUNK, HG_SUB
    ns = cs // us
    last = 0 if reverse else cs - 1
    b_last = b[last:last + 1]
    qf = q.astype(F32)
    kf = k.astype(F32)
    q_ref_rows, k_blocks = [], []
    for s in range(ns):
        if reverse:
            keys = slice(cs - us * (s + 1), cs)
            ref = b[cs - us * s:cs - us * s + 1] if s > 0 else jnp.zeros_like(b_last)
        else:
            keys = slice(0, us * (s + 1))
            ref = b[us * s - 1:us * s] if s > 0 else jnp.zeros_like(b_last)
        q_ref_rows.append(jnp.broadcast_to(ref, (us, HG_DIM)))
        kh = (kf[keys] * jnp.exp(ref - b[keys])).astype(BF16)
        pad = jnp.zeros((cs - us * (s + 1), HG_DIM), BF16)
        if pad.shape[0]:
            kh = jnp.concatenate([pad, kh] if reverse else [kh, pad], axis=0)
        k_blocks.append(kh)
    q_ref = jnp.concatenate(q_ref_rows[::-1] if reverse else q_ref_rows, axis=0)
    qh = qf * jnp.exp(b - q_ref)
    sub = lax.broadcasted_iota(jnp.int32, (cs, HG_DIM), 0) // us
    if reverse:
        sub = ns - 1 - sub
    q_cat = jnp.concatenate([jnp.where(sub == s, qh, 0.0).astype(BF16) for s in range(ns)], axis=1)
    att = _dot_nt(q_cat, jnp.concatenate(k_blocks, axis=1))
    kdec = (kf * jnp.exp(b_last - b)).astype(BF16)
    return att, (qf * jnp.exp(b)).astype(BF16), v, jnp.exp(b_last), _dot_tn(v, kdec)


def _hgrn_chunk_finish(parts, st, reverse):
    att, q_dec, v, decay, update = parts
    ri = lax.broadcasted_iota(jnp.int32, att.shape, 0)
    ci = lax.broadcasted_iota(jnp.int32, att.shape, 1)
    att = jnp.where((ci >= ri) if reverse else (ci <= ri), att, 0.0)
    o = _dot(att.astype(BF16), v) + _dot_nt(q_dec, st.astype(BF16))
    return o, st * decay + update


def _hgrn_kernel(q_ref, kf_ref, gf_ref, kb_ref, gb_ref, v_ref, zg_ref, s0f_ref, s0b_ref, nw_ref,
                 y_ref, of_ref, ob_ref):
    n = q_ref.shape[1]
    cs = HG_CHUNK
    nc = n // cs
    batch = math.gcd(nc, HG_BATCH)

    def body(i, carry):
        sf, sb = carry
        started = []
        for j in range(batch):
            fwd = pl.ds(pl.multiple_of((i * batch + j) * cs, cs), cs)
            bwd = pl.ds(pl.multiple_of((nc - 1 - i * batch - j) * cs, cs), cs)
            started.append((
                fwd, _hgrn_chunk_start(q_ref[0, fwd, :], kf_ref[0, fwd, :], v_ref[0, fwd, :],
                                       gf_ref[0, fwd, :], False),
                bwd, _hgrn_chunk_start(q_ref[0, bwd, :], kb_ref[0, bwd, :], v_ref[0, bwd, :],
                                       gb_ref[0, bwd, :], True)))
        for fwd, parts_f, bwd, parts_b in started:
            of_ref[fwd, :], sf = _hgrn_chunk_finish(parts_f, sf, False)
            ob_ref[bwd, :], sb = _hgrn_chunk_finish(parts_b, sb, True)
        return sf, sb

    lax.fori_loop(0, nc // batch, body, (s0f_ref[0, 0], s0b_ref[0, 0]))

    rb = min(n, 512)

    def readout(j, carry):
        sl = pl.ds(pl.multiple_of(j * rb, rb), rb)
        o = of_ref[sl, :] + ob_ref[sl, :]
        o = o * lax.rsqrt(jnp.mean(o * o, axis=-1, keepdims=True) + NORM_EPS) * nw_ref[...]
        y_ref[0, sl, :] = (o * _silu(zg_ref[0, sl, :].astype(F32))).astype(BF16)
        return carry

    lax.fori_loop(0, n // rb, readout, 0)


def _hgrn(q, kf, gf, kb, gb, v, zg, s0f, s0b, norm_w):
    b, n, _ = q.shape
    head = lambda bi, hi: (bi, 0, hi)
    st = lambda bi, hi: (bi, hi, 0, 0)
    seq = pl.BlockSpec((1, n, HG_DIM), head)
    state = pl.BlockSpec((1, 1, HG_DIM, HG_DIM), st)
    return pl.pallas_call(
        _hgrn_kernel,
        grid=(b, HG_HEADS),
        in_specs=[seq, seq, seq, seq, seq, seq, seq, state, state,
                  pl.BlockSpec((1, HG_DIM), lambda bi, hi: (0, hi))],
        out_specs=seq,
        out_shape=jax.ShapeDtypeStruct((b, n, HG_WIDTH), BF16),
        scratch_shapes=[pltpu.VMEM((n, HG_DIM), F32), pltpu.VMEM((n, HG_DIM), F32)],
        compiler_params=_params("arbitrary", "arbitrary"),
        name="hgrn2",
    )(q, kf, gf, kb, gb, v, zg, s0f, s0b, norm_w)


def _attn_kernel(sink_ref, q_ref, kp_ref, kc_ref, kn_ref, ksp_ref, ksc_ref, ksn_ref,
                 vp_ref, vc_ref, vn_ref, vsp_ref, vsc_ref, vsn_ref,
                 xk_ref, xks_ref, xv_ref, xvs_ref, y_ref):
    i = pl.program_id(1)
    nb = pl.num_programs(1)
    low = lax.broadcasted_iota(jnp.int32, (1, LANES), 1) < HEAD_DIM
    ctx_len = xk_ref.shape[1]
    rows = 2 * BAND
    ri = lax.broadcasted_iota(jnp.int32, (rows, BAND), 0) % BAND
    ci = lax.broadcasted_iota(jnp.int32, (rows, BAND), 1)
    ok_prev = (ci >= ri) & (i > 0)
    ok_next = (ci <= ri) & (i < nb - 1)
    p0, c0, n0 = ctx_len, ctx_len + BAND, ctx_len + 2 * BAND
    top = lax.broadcasted_iota(jnp.int32, (rows, 1), 0) < BAND

    def keys_of(refs, keep_low):
        x = jnp.concatenate([r[0] for r in refs], axis=0)
        return jnp.where(low if keep_low else ~low, x, jnp.zeros_like(x))

    group = Q_HEADS // KV_HEADS
    k_plain, k_swap = (xk_ref, kp_ref, kc_ref, kn_ref), (xks_ref, ksp_ref, ksc_ref, ksn_ref)
    v_plain, v_swap = (xv_ref, vp_ref, vc_ref, vn_ref), (xvs_ref, vsp_ref, vsc_ref, vsn_ref)
    scores = {}
    for kvh in range(KV_HEADS):
        tile = kvh * (group // 2)
        q = jnp.concatenate([q_ref[0, :, tile * LANES:(tile + 1) * LANES],
                             q_ref[0, :, (tile + 1) * LANES:(tile + 2) * LANES]], axis=0)
        for sub in range(2):
            in_low = sub == 0
            scores[kvh, sub] = _dot_nt(q, keys_of(k_plain if (kvh == 0) == in_low else k_swap, in_low))
    for kvh in range(KV_HEADS):
        tile = kvh * (group // 2)
        acc = jnp.zeros((rows, LANES), F32)
        for sub in range(2):
            in_low = sub == 0
            v = keys_of(v_plain if (kvh == 0) == in_low else v_swap, in_low)
            sink = jnp.where(top, sink_ref[group * kvh + sub], sink_ref[group * kvh + 2 + sub])
            s = scores[kvh, sub]
            s = jnp.concatenate([s[:, :p0], jnp.where(ok_prev, s[:, p0:c0], -jnp.inf), s[:, c0:n0],
                                 jnp.where(ok_next, s[:, n0:], -jnp.inf)], axis=1)
            m = jnp.maximum(jnp.max(s, axis=1, keepdims=True), sink)
            e = jnp.exp(s - m)
            denom = jnp.sum(e, axis=1, keepdims=True) + jnp.exp(sink - m)
            acc = acc + _dot(e.astype(BF16), v) / denom
        y_ref[0, :, tile * LANES:(tile + 1) * LANES] = acc[:BAND].astype(BF16)
        y_ref[0, :, (tile + 1) * LANES:(tile + 2) * LANES] = acc[BAND:].astype(BF16)


def _attn(sink, aq, ak, aks, av, avs, xk, xks, xv, xvs):
    b, n, _ = aq.shape
    nb = n // BAND
    cur = lambda bi, i: (bi, i, 0)
    prev = lambda bi, i: (bi, jnp.maximum(i - 1, 0), 0)
    nxt = lambda bi, i: (bi, jnp.minimum(i + 1, nb - 1), 0)
    per_b = lambda bi, i: (bi, 0, 0)
    kv = lambda f: pl.BlockSpec((1, BAND, KV_WIDTH), f)
    cx = pl.BlockSpec((1, xk.shape[1], KV_WIDTH), per_b)
    return pl.pallas_call(
        _attn_kernel,
        grid=(b, nb),
        in_specs=[pl.BlockSpec(memory_space=pltpu.SMEM),
                  pl.BlockSpec((1, BAND, ATT_WIDTH), cur),
                  kv(prev), kv(cur), kv(nxt), kv(prev), kv(cur), kv(nxt),
                  kv(prev), kv(cur), kv(nxt), kv(prev), kv(cur), kv(nxt),
                  cx, cx, cx, cx],
        out_specs=pl.BlockSpec((1, BAND, ATT_WIDTH), cur),
        out_shape=jax.ShapeDtypeStruct((b, n, ATT_WIDTH), BF16),
        compiler_params=_params("arbitrary", "arbitrary"),
        name="window_attn",
    )(sink, aq, ak, ak, ak, aks, aks, aks, av, av, av, avs, avs, avs, xk, xks, xv, xvs)


def _route(hf, wr_t, bias):
    tm = hf.shape[0]
    ne = wr_t.shape[0]
    h_hi = hf.astype(BF16)
    h_lo = (hf - h_hi.astype(F32)).astype(BF16)
    w_hi = wr_t.astype(BF16)
    w_lo = (wr_t - w_hi.astype(F32)).astype(BF16)
    first = _dot_nt(jnp.concatenate([w_hi, w_lo], axis=0), h_hi)
    scores = jax.nn.sigmoid(first[:ne] + first[ne:] + _dot_nt(w_hi, h_lo))
    sel = scores + bias
    grp = sel.reshape(N_GROUPS, GROUP_SIZE, tm)
    j = lax.broadcasted_iota(jnp.int32, grp.shape, 1)
    m1 = jnp.max(grp, axis=1, keepdims=True)
    first = jnp.min(jnp.where(grp == m1, j, GROUP_SIZE), axis=1, keepdims=True)
    m2 = jnp.max(jnp.where(j == first, -jnp.inf, grp), axis=1, keepdims=True)
    gs = (m1 + m2).reshape(N_GROUPS, tm)
    gi = lax.broadcasted_iota(jnp.int32, gs.shape, 0)
    rank = jnp.zeros(gs.shape, jnp.int32)
    for g in range(N_GROUPS):
        other = gs[g:g + 1]
        rank = rank + ((other > gs) | ((other == gs) & (g < gi))).astype(jnp.int32)
    gsel = rank < TOPK_GROUPS
    emask = jnp.broadcast_to(gsel[:, None, :], grp.shape).reshape(N_EXPERTS, tm)
    cand = jnp.where(emask, sel, -jnp.inf)
    ei = lax.broadcasted_iota(jnp.int32, cand.shape, 0)
    chosen = jnp.zeros(cand.shape, jnp.bool_)
    for _ in range(TOP_K):
        best = jnp.max(cand, axis=0, keepdims=True)
        first = jnp.min(jnp.where(cand == best, ei, N_EXPERTS), axis=0, keepdims=True)
        hit = ei == first
        chosen = chosen | hit
        cand = jnp.where(hit, -jnp.inf, cand)
    w = jnp.where(chosen, scores, 0.0)
    return w / jnp.sum(w, axis=0, keepdims=True) * ROUTED_SCALE, jnp.where(chosen, 1.0, 0.0)


def _sort_rows(chosen, gates_t):
    ne, ts = chosen.shape
    sel = chosen.astype(BF16)
    chosen = chosen > 0.5
    r = lax.broadcasted_iota(jnp.int32, (ts, ts), 0)
    c = lax.broadcasted_iota(jnp.int32, (ts, ts), 1)
    seen = _dot(sel, jnp.where(r <= c, 1.0, 0.0).astype(BF16))
    total = _dot(sel, jnp.ones((ts, ts), BF16))
    padded = jnp.floor((total + (SORT_UNIT - 1)) * (1.0 / SORT_UNIT)) * SORT_UNIT
    er = lax.broadcasted_iota(jnp.int32, (ne, ne), 0)
    ec = lax.broadcasted_iota(jnp.int32, (ne, ne), 1)
    before = jnp.where(ec < er, 1.0, 0.0).astype(BF16)
    start = _dot(before, padded.astype(BF16))
    choice = _dot(before, sel)
    row = start + seen - 1.0
    dest, gate = [], []
    for k in range(TOP_K):
        mk = chosen & (choice == k)
        dest.append(jnp.sum(jnp.where(mk, row, 0.0), axis=0, keepdims=True))
        gate.append(jnp.sum(jnp.where(mk, gates_t, 0.0), axis=0, keepdims=True))
    sel_pad = jnp.concatenate([sel, jnp.zeros((LANES - ne, ts), BF16)], axis=0)
    counts = _dot_nt(jnp.ones((8, ts), BF16), sel_pad)
    dest = jnp.concatenate(dest, axis=0)
    pad = jnp.zeros((LANES - TOP_K, ts), F32)
    dest_cols = jnp.concatenate([dest, pad], axis=0).T
    gate_cols = jnp.concatenate(gate + [pad], axis=0).T
    return dest.astype(jnp.int32), dest_cols.astype(jnp.int32), gate_cols, counts


def _outproj_kernel(x_ref, yh_ref, ya_ref, w_ref, g1_ref, sh_ref, sc_ref, lw_ref, lb_ref,
                    wr_ref, rb_ref, x1_ref, h2_ref, dest_ref, destc_ref, gatec_ref, cnt_ref):
    y = _dot(yh_ref[0], w_ref[:HG_WIDTH, :]) + _dot(ya_ref[0], w_ref[HG_WIDTH:, :])
    x1 = _ln_rows(ALPHA * x_ref[0] + g1_ref[0] * y, LN_EPS) * lw_ref[...] + lb_ref[...]
    x1_ref[0] = x1
    hf = _ln_rows(x1, NORM_EPS) * (1.0 + sc_ref[0]) + sh_ref[0]
    h2_ref[0] = hf.astype(BF16)
    gates_t, chosen = _route(hf, wr_ref[...], rb_ref[...])
    for s in range(hf.shape[0] // SORT_TILE):
        sl = slice(s * SORT_TILE, (s + 1) * SORT_TILE)
        dest_ref[s], destc_ref[s], gatec_ref[s], cnt_ref[s] = _sort_rows(chosen[:, sl], gates_t[:, sl])


def _outproj(x, yh, ya, w, g1, sh2, sc2, lw, lb, wr_t, rbias, tm):
    b, n, d = x.shape
    row = lambda bi, i: (bi, i, 0)
    per_b = lambda bi, i: (bi, 0, 0)
    const = lambda bi, i: (0, 0)
    nt = b * n // SORT_TILE
    per_step = tm // SORT_TILE
    tiles = lambda bi, i: (bi * (n // tm) + i, 0, 0)
    return pl.pallas_call(
        _outproj_kernel,
        grid=(b, n // tm),
        in_specs=[pl.BlockSpec((1, tm, d), row),
                  pl.BlockSpec((1, tm, HG_WIDTH), row), pl.BlockSpec((1, tm, ATT_WIDTH), row),
                  pl.BlockSpec(w.shape, const),
                  pl.BlockSpec((1, 1, d), per_b), pl.BlockSpec((1, 1, d), per_b),
                  pl.BlockSpec((1, 1, d), per_b),
                  pl.BlockSpec((1, d), const), pl.BlockSpec((1, d), const),
                  pl.BlockSpec(wr_t.shape, const), pl.BlockSpec(rbias.shape, const)],
        out_specs=[pl.BlockSpec((1, tm, d), row), pl.BlockSpec((1, tm, d), row),
                   pl.BlockSpec((per_step, TOP_K, SORT_TILE), tiles),
                   pl.BlockSpec((per_step, SORT_TILE, LANES), tiles),
                   pl.BlockSpec((per_step, SORT_TILE, LANES), tiles),
                   pl.BlockSpec((per_step, 8, LANES), tiles)],
        out_shape=[jax.ShapeDtypeStruct((b, n, d), F32), jax.ShapeDtypeStruct((b, n, d), BF16),
                   jax.ShapeDtypeStruct((nt, TOP_K, SORT_TILE), jnp.int32),
                   jax.ShapeDtypeStruct((nt, SORT_TILE, LANES), jnp.int32),
                   jax.ShapeDtypeStruct((nt, SORT_TILE, LANES), F32),
                   jax.ShapeDtypeStruct((nt, 8, LANES), F32)],
        compiler_params=_params("arbitrary", "arbitrary"),
        name="outproj_ln_router",
    )(x, yh, ya, w, g1, sh2, sc2, lw, lb, wr_t, rbias)


def _moe_layout(counts, max_tiles):
    n_pad = (counts + (SORT_UNIT - 1)) // SORT_UNIT * SORT_UNIT
    total = jnp.sum(n_pad, axis=0)
    region = (total + (EXPERT_TILE - 1)) // EXPERT_TILE * EXPERT_TILE
    base = jnp.cumsum(region) - region
    chunk_row = base[None, :] + jnp.cumsum(n_pad, axis=0) - n_pad
    tile_end = jnp.cumsum(region // EXPERT_TILE)
    n_used = tile_end[-1]
    j = jnp.minimum(jnp.arange(max_tiles, dtype=jnp.int32), n_used - 1)
    tile_expert = jnp.sum((tile_end[None, :] <= j[:, None]).astype(jnp.int32), axis=1)
    units = n_pad // SORT_UNIT
    local_row = jnp.cumsum(n_pad, axis=1) - n_pad
    rare = jnp.ones(units.shape, jnp.bool_)
    counts_out, lists = [], []
    for size, width in PIECE_LISTS[:-1]:
        mask = units == size
        rare = rare & ~mask
        slot = jnp.cumsum(mask, axis=1) - 1
        pick = mask[:, None, :] & (slot[:, None, :] == jnp.arange(width)[None, :, None])
        counts_out.append(jnp.sum(mask, axis=1))
        lists += [jnp.sum(jnp.where(pick, v[:, None, :], 0), axis=2) for v in (local_row, chunk_row)]
    single = jnp.where(rare, units, 0)
    single_end = jnp.cumsum(single, axis=1)
    u = jnp.arange(LOCAL_UNITS)[None, :, None]
    first = (single_end - single)[:, None, :]
    pick = (u >= first) & (u < single_end[:, None, :])
    counts_out.append(single_end[:, -1])
    lists += [jnp.sum(jnp.where(pick, v[:, None, :] + (u - first) * SORT_UNIT, 0), axis=2)
              for v in (local_row, chunk_row)]
    header = jnp.stack(counts_out + [jnp.sum(n_pad, axis=1)], axis=1)
    header = jnp.pad(header, ((0, 0), (0, TABLE_HEADER - header.shape[1])))
    table = jnp.concatenate([header] + lists, axis=1)
    table = jnp.pad(table, ((0, 0), (0, TABLE_WIDTH - table.shape[1]))).astype(jnp.int32)[:, None, :]
    tails = jnp.concatenate([base + total, (region - total) // SORT_UNIT]).astype(jnp.int32)
    return table, tails, tile_expert, n_used.astype(jnp.int32)[None]


def _row_cases(rows_used, fn):
    @pl.when(rows_used <= USUAL_ROWS)
    def _():
        fn(USUAL_ROWS)

    @pl.when(rows_used > USUAL_ROWS)
    def _():
        fn(LOCAL_ROWS)


def _unit(ref, row, units=1):
    return ref.at[pl.ds(pl.multiple_of(row, SORT_UNIT), units * SORT_UNIT), :]


def _for_each_piece(table_ref, fn):
    offset = TABLE_HEADER
    for c, (size, width) in enumerate(PIECE_LISTS):
        def body(j, carry, offset=offset, size=size, width=width):
            fn(table_ref[0, 0, offset + j], table_ref[0, 0, offset + width + j], size)
            return carry

        lax.fori_loop(0, table_ref[0, 0, c], body, 0)
        offset += 2 * width


def _await_units(units, wait_fn):
    def many(u, carry):
        wait_fn(WAIT_UNITS)
        return carry

    def single(u, carry):
        wait_fn(1)
        return carry

    lax.fori_loop(0, units // WAIT_UNITS, many, 0)
    lax.fori_loop(0, units % WAIT_UNITS, single, 0)


def _dispatch_kernel(chunks_ref, tails_ref, h_ref, dest_ref, xs_ref, buf_ref, zero_ref, pending_ref, sem):
    i = pl.program_id(0)
    slot = i % 2
    buf = buf_ref.at[slot]

    def drain(s):
        _await_units(pending_ref[s], lambda n: pltpu.make_async_copy(
            _unit(buf_ref.at[s], 0, n), _unit(xs_ref, 0, n), sem.at[s]).wait())

    @pl.when(i >= 2)
    def _():
        drain(slot)

    def permute(rows):
        r = lax.broadcasted_iota(jnp.int32, (rows, SORT_TILE), 0).astype(jnp.int16)
        p = jnp.zeros((rows, SORT_TILE), BF16)
        one = jnp.ones((rows, SORT_TILE), BF16)
        for k in range(TOP_K):
            hit = r == dest_ref[0, k:k + 1, :].astype(jnp.int16)
            p = jnp.where(hit, one, p)
        buf[0:rows, :] = _dot(p, h_ref[...]).astype(BF16)

    rows_used = chunks_ref[0, 0, TABLE_ROWS_USED]
    _row_cases(rows_used, permute)
    _for_each_piece(chunks_ref, lambda loc, glob, n: pltpu.make_async_copy(
        _unit(buf, loc, n), _unit(xs_ref, glob, n), sem.at[slot]).start())
    pending_ref[slot] = rows_used // SORT_UNIT

    @pl.when(i == pl.num_programs(0) - 1)
    def _():
        zero_ref[...] = jnp.zeros_like(zero_ref)

        def per_expert(e, total):
            n = tails_ref[N_EXPERTS + e]
            start = tails_ref[e]

            def many(u, carry):
                pltpu.make_async_copy(zero_ref, _unit(xs_ref, start + u * (WAIT_UNITS * SORT_UNIT), WAIT_UNITS),
                                      sem.at[2]).start()
                return carry

            def single(u, carry):
                row = start + (n // WAIT_UNITS * WAIT_UNITS + u) * SORT_UNIT
                pltpu.make_async_copy(_unit(zero_ref, 0), _unit(xs_ref, row), sem.at[2]).start()
                return carry

            lax.fori_loop(0, n // WAIT_UNITS, many, 0)
            lax.fori_loop(0, n % WAIT_UNITS, single, 0)
            return total + n

        total = lax.fori_loop(0, N_EXPERTS, per_expert, 0)
        _await_units(total, lambda n: pltpu.make_async_copy(
            _unit(zero_ref, 0, n), _unit(xs_ref, 0, n), sem.at[2]).wait())

        @pl.when(i >= 1)
        def _():
            drain(1 - slot)

        drain(slot)


def _dispatch(chunks, tails, h2, dest, max_rows):
    t, d = h2.shape
    nt = t // SORT_TILE
    return pl.pallas_call(
        _dispatch_kernel,
        grid=(nt,),
        in_specs=[pl.BlockSpec((1, 1, TABLE_WIDTH), lambda i: (i, 0, 0), memory_space=pltpu.SMEM),
                  pl.BlockSpec(memory_space=pltpu.SMEM),
                  pl.BlockSpec((SORT_TILE, d), lambda i: (i, 0)),
                  pl.BlockSpec((1, TOP_K, SORT_TILE), lambda i: (i, 0, 0))],
        out_specs=pl.BlockSpec(memory_space=pl.ANY),
        out_shape=jax.ShapeDtypeStruct((max_rows, d), BF16),
        scratch_shapes=[pltpu.VMEM((2, LOCAL_ROWS, d), BF16), pltpu.VMEM((WAIT_UNITS * SORT_UNIT, d), BF16),
                        pltpu.SMEM((2,), jnp.int32), pltpu.SemaphoreType.DMA((3,))],
        compiler_params=_params("arbitrary"),
        name="moe_dispatch",
    )(chunks, tails, h2, dest)


def _expert_kernel(te_ref, nu_ref, xs_ref, wg_ref, wu_ref, wd_ref, ys_ref):
    @pl.when(pl.program_id(0) < nu_ref[0])
    def _():
        x = xs_ref[...]
        a = _silu(_dot(x, wg_ref[0].astype(BF16))) * _dot(x, wu_ref[0].astype(BF16))
        ys_ref[...] = _dot(a.astype(BF16), wd_ref[0].astype(BF16)).astype(BF16)


def _experts(tile_expert, n_used, xs, wg, wu, wd):
    rows, d = xs.shape
    ff = wg.shape[2]
    row = lambda j, te, nu: (jnp.minimum(j, jnp.maximum(nu[0] - 1, 0)), 0)
    exp = lambda j, te, nu: (te[j], 0, 0)
    return pl.pallas_call(
        _expert_kernel,
        grid_spec=pltpu.PrefetchScalarGridSpec(
            num_scalar_prefetch=2,
            grid=(rows // EXPERT_TILE,),
            in_specs=[pl.BlockSpec((EXPERT_TILE, d), row),
                      pl.BlockSpec((1, d, ff), exp), pl.BlockSpec((1, d, ff), exp),
                      pl.BlockSpec((1, ff, d), exp)],
            out_specs=pl.BlockSpec((EXPERT_TILE, d), row)),
        out_shape=jax.ShapeDtypeStruct((rows, d), BF16),
        compiler_params=_params("arbitrary"),
        name="moe_experts",
    )(tile_expert, n_used, xs, wg, wu, wd)


def _combine_kernel(chunks_ref, next_ref, ys_ref, dest_ref, gate_ref, h_ref, x1_ref, g2_ref, lw_ref, lb_ref,
                    swg_ref, swu_ref, swd_ref, o_ref, buf_ref, sem):
    i = pl.program_id(0)
    slot = i % 2

    def fetch(meta_ref, s):
        _for_each_piece(meta_ref, lambda loc, glob, n: pltpu.make_async_copy(
            _unit(ys_ref, glob, n), _unit(buf_ref.at[s], loc, n), sem.at[s]).start())

    @pl.when(i == 0)
    def _():
        buf_ref[...] = jnp.zeros_like(buf_ref)
        fetch(chunks_ref, 0)

    @pl.when(i + 1 < pl.num_programs(0))
    def _():
        fetch(next_ref, 1 - slot)

    rows_used = chunks_ref[0, 0, TABLE_ROWS_USED]
    _await_units(rows_used // SORT_UNIT, lambda n: pltpu.make_async_copy(
        _unit(ys_ref, 0, n), _unit(buf_ref.at[slot], 0, n), sem.at[slot]).wait())

    def unpermute(rows):
        h = h_ref[...]
        a = _silu(_dot(h, swg_ref[...])) * _dot(h, swu_ref[...])
        lane = lax.broadcasted_iota(jnp.int32, (SORT_TILE, rows), 1).astype(jnp.int16)
        p = jnp.zeros((SORT_TILE, rows), BF16)
        for k in range(TOP_K):
            hit = lane == dest_ref[0, :, k:k + 1].astype(jnp.int16)
            p = jnp.where(hit, jnp.broadcast_to(gate_ref[0, :, k:k + 1].astype(BF16), p.shape), p)
        ffn = _dot(a.astype(BF16), swd_ref[...]) + _dot(p, buf_ref[slot, 0:rows, :])
        u = ALPHA * x1_ref[...] + g2_ref[0] * ffn
        o_ref[...] = _ln_rows(u, LN_EPS) * lw_ref[...] + lb_ref[...]

    _row_cases(rows_used, unpermute)


def _combine(chunks, ys, dest, gate, h2, x1, g2, lw, lb, swg, swu, swd, tiles_per_batch):
    t, d = h2.shape
    ff = swg.shape[1]
    nt = t // SORT_TILE
    tile = lambda i: (i, 0, 0)
    nxt = lambda i: (jnp.minimum(i + 1, nt - 1), 0, 0)
    row = lambda i: (i, 0)
    const = lambda i: (0, 0)
    return pl.pallas_call(
        _combine_kernel,
        grid=(nt,),
        in_specs=[pl.BlockSpec((1, 1, TABLE_WIDTH), tile, memory_space=pltpu.SMEM),
                  pl.BlockSpec((1, 1, TABLE_WIDTH), nxt, memory_space=pltpu.SMEM),
                  pl.BlockSpec(memory_space=pl.ANY),
                  pl.BlockSpec((1, SORT_TILE, LANES), tile), pl.BlockSpec((1, SORT_TILE, LANES), tile),
                  pl.BlockSpec((SORT_TILE, d), row), pl.BlockSpec((SORT_TILE, d), row),
                  pl.BlockSpec((1, 1, d), lambda i: (i // tiles_per_batch, 0, 0)),
                  pl.BlockSpec((1, d), const), pl.BlockSpec((1, d), const),
                  pl.BlockSpec((d, ff), const), pl.BlockSpec((d, ff), const), pl.BlockSpec((ff, d), const)],
        out_specs=pl.BlockSpec((SORT_TILE, d), row),
        out_shape=jax.ShapeDtypeStruct((t, d), F32),
        scratch_shapes=[pltpu.VMEM((2, LOCAL_ROWS, d), BF16), pltpu.SemaphoreType.DMA((2,))],
        compiler_params=_params("arbitrary"),
        name="moe_combine",
    )(chunks, chunks, ys, dest, gate, h2, x1, g2, lw, lb, swg, swu, swd)


def _moe(h2, x1, dest, dest_cols, gate_cols, counts, g2, lw, lb, wg, wu, wd, swg, swu, swd):
    b, n, d = x1.shape
    t = b * n
    nt = t // SORT_TILE
    max_rows = t * TOP_K + nt * N_EXPERTS * (SORT_UNIT - 1) + N_EXPERTS * (EXPERT_TILE - SORT_UNIT)
    max_rows = -(-max_rows // EXPERT_TILE) * EXPERT_TILE
    chunks, tails, tile_expert, n_used = _moe_layout(
        counts[:, 0, :N_EXPERTS].astype(jnp.int32), max_rows // EXPERT_TILE)
    h2 = h2.reshape(t, d)
    xs = _dispatch(chunks, tails, h2, dest, max_rows)
    ys = _experts(tile_expert, n_used, xs, wg, wu, wd)
    out = _combine(chunks, ys, dest_cols, gate_cols, h2, x1.reshape(t, d), g2, lw, lb, swg, swu, swd,
                   n // SORT_TILE)
    return out.reshape(b, n, d)


def _swap_kv_heads(w):
    return jnp.concatenate([w[:, HEAD_DIM:], w[:, :HEAD_DIM]], axis=1)


def _split_w_in(w_in):
    bounds = np.cumsum([HG_WIDTH] * 5 + [ATT_WIDTH, KV_WIDTH])
    return jnp.split(w_in, [int(v) for v in bounds], axis=1)


def _context_weight(w_in):
    _, zff, zfb, zi, _, _, ak, av = _split_w_in(w_in)
    return jnp.concatenate([zff, zfb, zi, ak, _swap_kv_heads(ak), av, _swap_kv_heads(av)],
                           axis=1).astype(BF16)


def _rope_tables(n):
    pos = np.arange(n)
    freqs = np.float32(ROPE_BASE) ** (-np.arange(ROT_PAIRS, dtype=np.float32) / np.float32(ROT_PAIRS))
    ang_row = (pos // GRID_W).astype(np.float32)[:, None] * freqs
    ang_col = (pos % GRID_W).astype(np.float32)[:, None] * freqs
    cos = np.concatenate([np.cos(ang_row)] * 2 + [np.cos(ang_col)] * 2, axis=1)
    sin = np.concatenate([-np.sin(ang_row), np.sin(ang_row), -np.sin(ang_col), np.sin(ang_col)], axis=1)
    reps = LANES // HEAD_DIM
    return (jnp.asarray(np.tile(cos, (1, reps)), dtype=F32), jnp.asarray(np.tile(sin, (1, reps)), dtype=F32))


def kernel(x, c, ctx, c_ctx, w_ada, b_ada, w_in, hg_lb_fwd, hg_lb_bwd, hg_norm_w, attn_sink, w_out, ln1_w, ln1_b, router_w, router_bias, exp_w_gate, exp_w_up, exp_w_down, shared_w_gate, shared_w_up, shared_w_down, ln2_w, ln2_b):
    b, n, d = x.shape
    layer = 0
    rows = -(-(b + 1) // 8) * 8
    cc = jnp.zeros((rows, d), F32).at[:b].set(c).at[b].set(c_ctx)
    mod = _ada(cc, w_ada[layer], b_ada[layer][None, :])
    sh1, sc1, g1, sh2, sc2, g2 = [m[:, None, :] for m in jnp.split(mod[:b], 6, axis=1)]
    csh1, csc1 = mod[b:b + 1, :d], mod[b:b + 1, d:2 * d]

    lb_f = jnp.cumsum(jax.nn.softmax(hg_lb_fwd.astype(F32), axis=0), axis=0)[layer][None, :]
    lb_b = jnp.cumsum(jax.nn.softmax(hg_lb_bwd.astype(F32), axis=0), axis=0)[layer][None, :]
    cos, sin = _rope_tables(n)

    xk, xks, xv, xvs, s0f, s0b = _ctx(ctx, csh1, csc1, _context_weight(w_in[layer]), lb_f, lb_b)
    tm = min(n, 512)
    (q, kf, gf, kb, gb, v, zg, aq, ak, aks, av, avs) = _inproj(
        x, sh1, sc1, w_in[layer].astype(BF16), lb_f, lb_b, cos, sin, min(n, 2 * INPROJ_PART))
    yh = _hgrn(q, kf, gf, kb, gb, v, zg, s0f, s0b, hg_norm_w[layer][None, :])
    ya = _attn(attn_sink[layer], aq, ak, aks, av, avs, xk, xks, xv, xvs)

    x1, h2, dest, dest_cols, gate_cols, counts = _outproj(
        x, yh, ya, w_out[layer].astype(BF16), g1, sh2, sc2, ln1_w[layer][None, :], ln1_b[layer][None, :],
        router_w[layer].T, router_bias[layer][:, None], tm)
    return _moe(h2, x1, dest, dest_cols, gate_cols, counts, g2, ln2_w[layer][None, :], ln2_b[layer][None, :],
                exp_w_gate[layer], exp_w_up[layer], exp_w_down[layer], shared_w_gate[layer].astype(BF16),
                shared_w_up[layer].astype(BF16), shared_w_down[layer].astype(BF16))
```

```python
import math

import jax
import jax.numpy as jnp
import numpy as np
from jax import lax
from jax.experimental import pallas as pl
from jax.experimental.pallas import tpu as pltpu

F32 = jnp.float32
BF16 = jnp.bfloat16
HIGHEST = lax.Precision.HIGHEST

DEPTH = 1
GRID_W = 64
HG_WIDTH = 512
HG_HEADS = 4
HG_DIM = 128
HG_CHUNK = 64
HG_SUB = 16
HG_BATCH = 8
LOG_F_MIN = -4.0
HEAD_DIM = 64
Q_HEADS = 8
KV_HEADS = 2
ATT_WIDTH = Q_HEADS * HEAD_DIM
KV_WIDTH = KV_HEADS * HEAD_DIM
BAND = 128
ROPE_BASE = 10000.0
ROT_PAIRS = HEAD_DIM // 4
N_EXPERTS = 64
TOP_K = 8
N_GROUPS = 8
GROUP_SIZE = N_EXPERTS // N_GROUPS
TOPK_GROUPS = 4
ROUTED_SCALE = 2.5
LN_EPS = 1e-5
NORM_EPS = 1e-6
ALPHA = (2.0 * DEPTH) ** 0.25

LANES = 128
BF16_SUBLANES = 16
MXU_DIM = 256

SORT_TILE = MXU_DIM
SORT_UNIT = BF16_SUBLANES
EXPERT_TILE = 2048
_LOCAL_WORST = SORT_TILE * TOP_K + N_EXPERTS * (SORT_UNIT - 1)
LOCAL_ROWS = -(-_LOCAL_WORST // MXU_DIM) * MXU_DIM
USUAL_ROWS = SORT_TILE * TOP_K + 2 * MXU_DIM
LOCAL_UNITS = LOCAL_ROWS // SORT_UNIT
COMMON_UNITS = (2, 3)
PIECE_LISTS = tuple((size, N_EXPERTS) for size in COMMON_UNITS) + ((1, LOCAL_UNITS),)
TABLE_HEADER = 8
TABLE_ROWS_USED = len(PIECE_LISTS)
TABLE_WIDTH = -(-(TABLE_HEADER + 2 * sum(width for _, width in PIECE_LISTS)) // LANES) * LANES
WAIT_UNITS = 8
V7X_VMEM_LIMIT_BYTES = 56 * 1024 * 1024


def _params(*sem):
    return pltpu.CompilerParams(dimension_semantics=sem, vmem_limit_bytes=V7X_VMEM_LIMIT_BYTES)


def _ln_rows(x, eps):
    mu = jnp.mean(x, axis=-1, keepdims=True)
    xc = x - mu
    return xc * lax.rsqrt(jnp.mean(xc * xc, axis=-1, keepdims=True) + eps)


def _silu(x):
    return x * jax.nn.sigmoid(x)


def _dot(a, b):
    return jnp.dot(a, b, preferred_element_type=F32)


def _dot_nt(a, b, precision=None):
    return lax.dot_general(a, b, (((1,), (1,)), ((), ())), precision=precision,
                           preferred_element_type=F32)


def _dot_tn(a, b):
    return lax.dot_general(a, b, (((0,), (0,)), ((), ())), preferred_element_type=F32)


def _forget_gate(z, lb):
    f = lb + (1.0 - lb) * jax.nn.sigmoid(z)
    return 1.0 - f, jnp.maximum(jnp.log(f), LOG_F_MIN)


def _chunk_scan(g, reverse):
    n = g.shape[0]
    pos = lax.broadcasted_iota(jnp.int32, g.shape, 0) % HG_CHUNK
    step = 1
    while step < HG_CHUNK:
        if reverse:
            g = g + jnp.where(pos < HG_CHUNK - step, pltpu.roll(g, n - step, axis=0), 0.0)
        else:
            g = g + jnp.where(pos >= step, pltpu.roll(g, step, axis=0), 0.0)
        step *= 2
    return g


def _ada_kernel(c_ref, w_ref, b_ref, o_ref):
    c = c_ref[...]
    o_ref[...] = jnp.dot(_silu(c), w_ref[...], precision=HIGHEST,
                         preferred_element_type=F32) + b_ref[...]


def _ada(cc, w, b):
    rows, d = cc.shape
    cols = w.shape[1]
    tn = 512
    return pl.pallas_call(
        _ada_kernel,
        grid=(cols // tn,),
        in_specs=[pl.BlockSpec((rows, d), lambda j: (0, 0)),
                  pl.BlockSpec((d, tn), lambda j: (0, j)),
                  pl.BlockSpec((1, tn), lambda j: (0, j))],
        out_specs=pl.BlockSpec((rows, tn), lambda j: (0, j)),
        out_shape=jax.ShapeDtypeStruct((rows, cols), F32),
        compiler_params=_params("arbitrary"),
        name="adaln",
    )(cc, w, b)


INPROJ_PART = 512
_C_Q, _C_FF, _C_FB, _C_I, _C_G = 0, 512, 1024, 1536, 2048
_C_AQ, _C_AK, _C_AV = 2560, 3072, 3200


def _swap_rotary_halves(x):
    first = lax.broadcasted_iota(jnp.int32, (1, LANES), 1) % (2 * ROT_PAIRS) < ROT_PAIRS
    tiles = []
    for t in range(x.shape[1] // LANES):
        xt = x[:, t * LANES:(t + 1) * LANES]
        tiles.append(jnp.where(first, pltpu.roll(xt, LANES - ROT_PAIRS, axis=1),
                               pltpu.roll(xt, ROT_PAIRS, axis=1)))
    return jnp.concatenate(tiles, axis=1)


def _inproj_kernel(x_ref, sh_ref, sc_ref, w_ref, lbf_ref, lbb_ref, cos_ref, sin_ref,
                   q_ref, kf_ref, gf_ref, kb_ref, gb_ref, v_ref, zg_ref,
                   aq_ref, ak_ref, aks_ref, av_ref, avs_ref):
    tm = x_ref.shape[1]
    part = min(tm, INPROJ_PART)
    started = []
    for p in range(tm // part):
        rows = slice(p * part, (p + 1) * part)
        h = (_ln_rows(x_ref[0, rows, :], NORM_EPS) * (1.0 + sc_ref[0]) + sh_ref[0]).astype(BF16)
        widths = ((_C_Q, HG_WIDTH), (_C_FF, HG_WIDTH), (_C_FB, HG_WIDTH), (_C_I, HG_WIDTH), (_C_G, HG_WIDTH),
                  (_C_AQ, ATT_WIDTH), (_C_AK, KV_WIDTH), (_C_AV, KV_WIDTH))
        started.append((rows, [_dot(h, w_ref[:, lo:lo + n]) for lo, n in widths]))

    scale = HEAD_DIM ** -0.5
    for rows, (zq, zff, zfb, zi, zg, aq, ak, av) in started:
        q_ref[0, rows, :] = zq.astype(BF16)
        k, g = _forget_gate(zff, lbf_ref[...])
        kf_ref[0, rows, :] = k.astype(BF16)
        gf_ref[0, rows, :] = _chunk_scan(g, False)
        k, g = _forget_gate(zfb, lbb_ref[...])
        kb_ref[0, rows, :] = k.astype(BF16)
        gb_ref[0, rows, :] = _chunk_scan(g, True)
        v_ref[0, rows, :] = zi.astype(BF16)
        zg_ref[0, rows, :] = zg.astype(BF16)
        cos = cos_ref[rows, :]
        sin = sin_ref[rows, :]
        cos4 = jnp.concatenate([cos] * (ATT_WIDTH // LANES), axis=1)
        sin4 = jnp.concatenate([sin] * (ATT_WIDTH // LANES), axis=1)
        aq_ref[0, rows, :] = ((aq * cos4 + _swap_rotary_halves(aq) * sin4) * scale).astype(BF16)
        ak = ak * cos + _swap_rotary_halves(ak) * sin
        ak_ref[0, rows, :] = ak.astype(BF16)
        aks_ref[0, rows, :] = pltpu.roll(ak, HEAD_DIM, axis=1).astype(BF16)
        av_ref[0, rows, :] = av.astype(BF16)
        avs_ref[0, rows, :] = pltpu.roll(av, HEAD_DIM, axis=1).astype(BF16)


def _inproj(x, sh, sc, w, lbf, lbb, cos, sin, tm):
    b, n, d = x.shape
    row = lambda bi, i: (bi, i, 0)
    per_b = lambda bi, i: (bi, 0, 0)
    const = lambda bi, i: (0, 0)
    tab = lambda bi, i: (i, 0)

    def out(width, dtype):
        return jax.ShapeDtypeStruct((b, n, width), dtype), pl.BlockSpec((1, tm, width), row)

    outs = [out(HG_WIDTH, BF16), out(HG_WIDTH, BF16), out(HG_WIDTH, F32), out(HG_WIDTH, BF16),
            out(HG_WIDTH, F32), out(HG_WIDTH, BF16), out(HG_WIDTH, BF16),
            out(ATT_WIDTH, BF16), out(KV_WIDTH, BF16), out(KV_WIDTH, BF16),
            out(KV_WIDTH, BF16), out(KV_WIDTH, BF16)]
    return pl.pallas_call(
        _inproj_kernel,
        grid=(b, n // tm),
        in_specs=[pl.BlockSpec((1, tm, d), row),
                  pl.BlockSpec((1, 1, d), per_b), pl.BlockSpec((1, 1, d), per_b),
                  pl.BlockSpec(w.shape, const, pipeline_mode=pl.Buffered(1)),
                  pl.BlockSpec((1, HG_WIDTH), const), pl.BlockSpec((1, HG_WIDTH), const),
                  pl.BlockSpec((tm, LANES), tab), pl.BlockSpec((tm, LANES), tab)],
        out_specs=[o[1] for o in outs],
        out_shape=[o[0] for o in outs],
        compiler_params=_params("arbitrary", "arbitrary"),
        name="latent_inproj",
    )(x, sh, sc, w, lbf, lbb, cos, sin)


_X_FF, _X_FB, _X_I, _X_AK, _X_AKS, _X_AV, _X_AVS, _X_TOTAL = 0, 512, 1024, 1536, 1664, 1792, 1920, 2048


def _ctx_kernel(c_ref, sh_ref, sc_ref, w_ref, lbf_ref, lbb_ref,
                k_ref, ks_ref, v_ref, vs_ref, sf_ref, sb_ref):
    h = (_ln_rows(c_ref[0], NORM_EPS) * (1.0 + sc_ref[...]) + sh_ref[...]).astype(BF16)

    def proj(lo, n):
        return _dot(h, w_ref[:, lo:lo + n])

    k_ref[0] = proj(_X_AK, KV_WIDTH).astype(BF16)
    ks_ref[0] = proj(_X_AKS, KV_WIDTH).astype(BF16)
    v_ref[0] = proj(_X_AV, KV_WIDTH).astype(BF16)
    vs_ref[0] = proj(_X_AVS, KV_WIDTH).astype(BF16)

    kf, gf = _forget_gate(proj(_X_FF, HG_WIDTH), lbf_ref[...])
    kb, gb = _forget_gate(proj(_X_FB, HG_WIDTH), lbb_ref[...])
    vi = proj(_X_I, HG_WIDTH).astype(BF16)
    n = h.shape[0]
    r = lax.broadcasted_iota(jnp.int32, (n, n), 0)
    c = lax.broadcasted_iota(jnp.int32, (n, n), 1)
    bf = jnp.dot((c <= r).astype(F32), gf, precision=HIGHEST, preferred_element_type=F32)
    bb = jnp.dot((c >= r).astype(F32), gb, precision=HIGHEST, preferred_element_type=F32)
    kdf = (kf * jnp.exp(bf[n - 1:n] - bf)).astype(BF16)
    kdb = (kb * jnp.exp(bb[0:1] - bb)).astype(BF16)
    for hd in range(HG_HEADS):
        sl = slice(hd * HG_DIM, (hd + 1) * HG_DIM)
        sf_ref[0, hd] = _dot_tn(vi[:, sl], kdf[:, sl])
        sb_ref[0, hd] = _dot_tn(vi[:, sl], kdb[:, sl])


def _ctx(ctx, sh, sc, w, lbf, lbb):
    b, n, d = ctx.shape
    per_b = lambda bi: (bi, 0, 0)
    const = lambda bi: (0, 0)
    kv = (jax.ShapeDtypeStruct((b, n, KV_WIDTH), BF16), pl.BlockSpec((1, n, KV_WIDTH), per_b))
    st = (jax.ShapeDtypeStruct((b, HG_HEADS, HG_DIM, HG_DIM), F32),
          pl.BlockSpec((1, HG_HEADS, HG_DIM, HG_DIM), lambda bi: (bi, 0, 0, 0)))
    outs = [kv, kv, kv, kv, st, st]
    return pl.pallas_call(
        _ctx_kernel,
        grid=(b,),
        in_specs=[pl.BlockSpec((1, n, d), per_b),
                  pl.BlockSpec((1, d), const), pl.BlockSpec((1, d), const),
                  pl.BlockSpec(w.shape, const),
                  pl.BlockSpec((1, HG_WIDTH), const), pl.BlockSpec((1, HG_WIDTH), const)],
        out_specs=[o[1] for o in outs],
        out_shape=[o[0] for o in outs],
        compiler_params=_params("arbitrary"),
        name="context_side",
    )(ctx, sh, sc, w, lbf, lbb)


def _hgrn_chunk_start(q, k, v, b, reverse):
    cs, us = HG_CHUNK, HG_SUB
    ns = cs // us
    last = 0 if reverse else cs - 1
    b_last = b[last:last + 1]
    qf = q.astype(F32)
    kf = k.astype(F32)
    q_ref_rows, k_blocks = [], []
    for s in range(ns):
        if reverse:
            keys = slice(cs - us * (s + 1), cs)
            ref = b[cs - us * s:cs - us * s + 1] if s > 0 else jnp.zeros_like(b_last)
        else:
            keys = slice(0, us * (s + 1))
            ref = b[us * s - 1:us * s] if s > 0 else jnp.zeros_like(b_last)
        q_ref_rows.append(jnp.broadcast_to(ref, (us, HG_DIM)))
        kh = (kf[keys] * jnp.exp(ref - b[keys])).astype(BF16)
        pad = jnp.zeros((cs - us * (s + 1), HG_DIM), BF16)
        if pad.shape[0]:
            kh = jnp.concatenate([pad, kh] if reverse else [kh, pad], axis=0)
        k_blocks.append(kh)
    q_ref = jnp.concatenate(q_ref_rows[::-1] if reverse else q_ref_rows, axis=0)
    qh = qf * jnp.exp(b - q_ref)
    sub = lax.broadcasted_iota(jnp.int32, (cs, HG_DIM), 0) // us
    if reverse:
        sub = ns - 1 - sub
    q_cat = jnp.concatenate([jnp.where(sub == s, qh, 0.0).astype(BF16) for s in range(ns)], axis=1)
    att = _dot_nt(q_cat, jnp.concatenate(k_blocks, axis=1))
    kdec = (kf * jnp.exp(b_last - b)).astype(BF16)
    return att, (qf * jnp.exp(b)).astype(BF16), v, jnp.exp(b_last), _dot_tn(v, kdec)


def _hgrn_chunk_finish(parts, st, reverse):
    att, q_dec, v, decay, update = parts
    ri = lax.broadcasted_iota(jnp.int32, att.shape, 0)
    ci = lax.broadcasted_iota(jnp.int32, att.shape, 1)
    att = jnp.where((ci >= ri) if reverse else (ci <= ri), att, 0.0)
    o = _dot(att.astype(BF16), v) + _dot_nt(q_dec, st.astype(BF16))
    return o, st * decay + update


def _hgrn_kernel(q_ref, kf_ref, gf_ref, kb_ref, gb_ref, v_ref, zg_ref, s0f_ref, s0b_ref, nw_ref,
                 y_ref, of_ref, ob_ref):
    n = q_ref.shape[1]
    cs = HG_CHUNK
    nc = n // cs
    batch = math.gcd(nc, HG_BATCH)

    def body(i, carry):
        sf, sb = carry
        started = []
        for j in range(batch):
            fwd = pl.ds(pl.multiple_of((i * batch + j) * cs, cs), cs)
            bwd = pl.ds(pl.multiple_of((nc - 1 - i * batch - j) * cs, cs), cs)
            started.append((
                fwd, _hgrn_chunk_start(q_ref[0, fwd, :], kf_ref[0, fwd, :], v_ref[0, fwd, :],
                                       gf_ref[0, fwd, :], False),
                bwd, _hgrn_chunk_start(q_ref[0, bwd, :], kb_ref[0, bwd, :], v_ref[0, bwd, :],
                                       gb_ref[0, bwd, :], True)))
        for fwd, parts_f, bwd, parts_b in started:
            of_ref[fwd, :], sf = _hgrn_chunk_finish(parts_f, sf, False)
            ob_ref[bwd, :], sb = _hgrn_chunk_finish(parts_b, sb, True)
        return sf, sb

    lax.fori_loop(0, nc // batch, body, (s0f_ref[0, 0], s0b_ref[0, 0]))

    rb = min(n, 512)

    def readout(j, carry):
        sl = pl.ds(pl.multiple_of(j * rb, rb), rb)
        o = of_ref[sl, :] + ob_ref[sl, :]
        o = o * lax.rsqrt(jnp.mean(o * o, axis=-1, keepdims=True) + NORM_EPS) * nw_ref[...]
        y_ref[0, sl, :] = (o * _silu(zg_ref[0, sl, :].astype(F32))).astype(BF16)
        return carry

    lax.fori_loop(0, n // rb, readout, 0)


def _hgrn(q, kf, gf, kb, gb, v, zg, s0f, s0b, norm_w):
    b, n, _ = q.shape
    head = lambda bi, hi: (bi, 0, hi)
    st = lambda bi, hi: (bi, hi, 0, 0)
    seq = pl.BlockSpec((1, n, HG_DIM), head)
    state = pl.BlockSpec((1, 1, HG_DIM, HG_DIM), st)
    return pl.pallas_call(
        _hgrn_kernel,
        grid=(b, HG_HEADS),
        in_specs=[seq, seq, seq, seq, seq, seq, seq, state, state,
                  pl.BlockSpec((1, HG_DIM), lambda bi, hi: (0, hi))],
        out_specs=seq,
        out_shape=jax.ShapeDtypeStruct((b, n, HG_WIDTH), BF16),
        scratch_shapes=[pltpu.VMEM((n, HG_DIM), F32), pltpu.VMEM((n, HG_DIM), F32)],
        compiler_params=_params("arbitrary", "arbitrary"),
        name="hgrn2",
    )(q, kf, gf, kb, gb, v, zg, s0f, s0b, norm_w)


def _attn_kernel(sink_ref, q_ref, kp_ref, kc_ref, kn_ref, ksp_ref, ksc_ref, ksn_ref,
                 vp_ref, vc_ref, vn_ref, vsp_ref, vsc_ref, vsn_ref,
                 xk_ref, xks_ref, xv_ref, xvs_ref, y_ref):
    i = pl.program_id(1)
    nb = pl.num_programs(1)
    low = lax.broadcasted_iota(jnp.int32, (1, LANES), 1) < HEAD_DIM
    ctx_len = xk_ref.shape[1]
    rows = 2 * BAND
    ri = lax.broadcasted_iota(jnp.int32, (rows, BAND), 0) % BAND
    ci = lax.broadcasted_iota(jnp.int32, (rows, BAND), 1)
    ok_prev = (ci >= ri) & (i > 0)
    ok_next = (ci <= ri) & (i < nb - 1)
    p0, c0, n0 = ctx_len, ctx_len + BAND, ctx_len + 2 * BAND
    top = lax.broadcasted_iota(jnp.int32, (rows, 1), 0) < BAND

    def keys_of(refs, keep_low):
        x = jnp.concatenate([r[0] for r in refs], axis=0)
        return jnp.where(low if keep_low else ~low, x, jnp.zeros_like(x))

    group = Q_HEADS // KV_HEADS
    k_plain, k_swap = (xk_ref, kp_ref, kc_ref, kn_ref), (xks_ref, ksp_ref, ksc_ref, ksn_ref)
    v_plain, v_swap = (xv_ref, vp_ref, vc_ref, vn_ref), (xvs_ref, vsp_ref, vsc_ref, vsn_ref)
    scores = {}
    for kvh in range(KV_HEADS):
        tile = kvh * (group // 2)
        q = jnp.concatenate([q_ref[0, :, tile * LANES:(tile + 1) * LANES],
                             q_ref[0, :, (tile + 1) * LANES:(tile + 2) * LANES]], axis=0)
        for sub in range(2):
            in_low = sub == 0
            scores[kvh, sub] = _dot_nt(q, keys_of(k_plain if (kvh == 0) == in_low else k_swap, in_low))
    for kvh in range(KV_HEADS):
        tile = kvh * (group // 2)
        acc = jnp.zeros((rows, LANES), F32)
        for sub in range(2):
            in_low = sub == 0
            v = keys_of(v_plain if (kvh == 0) == in_low else v_swap, in_low)
            sink = jnp.where(top, sink_ref[group * kvh + sub], sink_ref[group * kvh + 2 + sub])
            s = scores[kvh, sub]
            s = jnp.concatenate([s[:, :p0], jnp.where(ok_prev, s[:, p0:c0], -jnp.inf), s[:, c0:n0],
                                 jnp.where(ok_next, s[:, n0:], -jnp.inf)], axis=1)
            m = jnp.maximum(jnp.max(s, axis=1, keepdims=True), sink)
            e = jnp.exp(s - m)
            denom = jnp.sum(e, axis=1, keepdims=True) + jnp.exp(sink - m)
            acc = acc + _dot(e.astype(BF16), v) / denom
        y_ref[0, :, tile * LANES:(tile + 1) * LANES] = acc[:BAND].astype(BF16)
        y_ref[0, :, (tile + 1) * LANES:(tile + 2) * LANES] = acc[BAND:].astype(BF16)


def _attn(sink, aq, ak, aks, av, avs, xk, xks, xv, xvs):
    b, n, _ = aq.shape
    nb = n // BAND
    cur = lambda bi, i: (bi, i, 0)
    prev = lambda bi, i: (bi, jnp.maximum(i - 1, 0), 0)
    nxt = lambda bi, i: (bi, jnp.minimum(i + 1, nb - 1), 0)
    per_b = lambda bi, i: (bi, 0, 0)
    kv = lambda f: pl.BlockSpec((1, BAND, KV_WIDTH), f)
    cx = pl.BlockSpec((1, xk.shape[1], KV_WIDTH), per_b)
    return pl.pallas_call(
        _attn_kernel,
        grid=(b, nb),
        in_specs=[pl.BlockSpec(memory_space=pltpu.SMEM),
                  pl.BlockSpec((1, BAND, ATT_WIDTH), cur),
                  kv(prev), kv(cur), kv(nxt), kv(prev), kv(cur), kv(nxt),
                  kv(prev), kv(cur), kv(nxt), kv(prev), kv(cur), kv(nxt),
                  cx, cx, cx, cx],
        out_specs=pl.BlockSpec((1, BAND, ATT_WIDTH), cur),
        out_shape=jax.ShapeDtypeStruct((b, n, ATT_WIDTH), BF16),
        compiler_params=_params("arbitrary", "arbitrary"),
        name="window_attn",
    )(sink, aq, ak, ak, ak, aks, aks, aks, av, av, av, avs, avs, avs, xk, xks, xv, xvs)


def _route(hf, wr_t, bias):
    tm = hf.shape[0]
    ne = wr_t.shape[0]
    h_hi = hf.astype(BF16)
    h_lo = (hf - h_hi.astype(F32)).astype(BF16)
    w_hi = wr_t.astype(BF16)
    w_lo = (wr_t - w_hi.astype(F32)).astype(BF16)
    first = _dot_nt(jnp.concatenate([w_hi, w_lo], axis=0), h_hi)
    scores = jax.nn.sigmoid(first[:ne] + first[ne:] + _dot_nt(w_hi, h_lo))
    sel = scores + bias
    grp = sel.reshape(N_GROUPS, GROUP_SIZE, tm)
    j = lax.broadcasted_iota(jnp.int32, grp.shape, 1)
    m1 = jnp.max(grp, axis=1, keepdims=True)
    first = jnp.min(jnp.where(grp == m1, j, GROUP_SIZE), axis=1, keepdims=True)
    m2 = jnp.max(jnp.where(j == first, -jnp.inf, grp), axis=1, keepdims=True)
    gs = (m1 + m2).reshape(N_GROUPS, tm)
    gi = lax.broadcasted_iota(jnp.int32, gs.shape, 0)
    rank = jnp.zeros(gs.shape, jnp.int32)
    for g in range(N_GROUPS):
        other = gs[g:g + 1]
        rank = rank + ((other > gs) | ((other == gs) & (g < gi))).astype(jnp.int32)
    gsel = rank < TOPK_GROUPS
    emask = jnp.broadcast_to(gsel[:, None, :], grp.shape).reshape(N_EXPERTS, tm)
    cand = jnp.where(emask, sel, -jnp.inf)
    ei = lax.broadcasted_iota(jnp.int32, cand.shape, 0)
    chosen = jnp.zeros(cand.shape, jnp.bool_)
    for _ in range(TOP_K):
        best = jnp.max(cand, axis=0, keepdims=True)
        first = jnp.min(jnp.where(cand == best, ei, N_EXPERTS), axis=0, keepdims=True)
        hit = ei == first
        chosen = chosen | hit
        cand = jnp.where(hit, -jnp.inf, cand)
    w = jnp.where(chosen, scores, 0.0)
    return w / jnp.sum(w, axis=0, keepdims=True) * ROUTED_SCALE, jnp.where(chosen, 1.0, 0.0)


def _sort_rows(chosen, gates_t):
    ne, ts = chosen.shape
    sel = chosen.astype(BF16)
    chosen = chosen > 0.5
    r = lax.broadcasted_iota(jnp.int32, (ts, ts), 0)
    c = lax.broadcasted_iota(jnp.int32, (ts, ts), 1)
    seen = _dot(sel, jnp.where(r <= c, 1.0, 0.0).astype(BF16))
    total = _dot(sel, jnp.ones((ts, ts), BF16))
    padded = jnp.floor((total + (SORT_UNIT - 1)) * (1.0 / SORT_UNIT)) * SORT_UNIT
    er = lax.broadcasted_iota(jnp.int32, (ne, ne), 0)
    ec = lax.broadcasted_iota(jnp.int32, (ne, ne), 1)
    before = jnp.where(ec < er, 1.0, 0.0).astype(BF16)
    start = _dot(before, padded.astype(BF16))
    choice = _dot(before, sel)
    row = start + seen - 1.0
    dest, gate = [], []
    for k in range(TOP_K):
        mk = chosen & (choice == k)
        dest.append(jnp.sum(jnp.where(mk, row, 0.0), axis=0, keepdims=True))
        gate.append(jnp.sum(jnp.where(mk, gates_t, 0.0), axis=0, keepdims=True))
    sel_pad = jnp.concatenate([sel, jnp.zeros((LANES - ne, ts), BF16)], axis=0)
    counts = _dot_nt(jnp.ones((8, ts), BF16), sel_pad)
    dest = jnp.concatenate(dest, axis=0)
    pad = jnp.zeros((LANES - TOP_K, ts), F32)
    dest_cols = jnp.concatenate([dest, pad], axis=0).T
    gate_cols = jnp.concatenate(gate + [pad], axis=0).T
    return dest.astype(jnp.int32), dest_cols.astype(jnp.int32), gate_cols, counts


def _outproj_kernel(x_ref, yh_ref, ya_ref, w_ref, g1_ref, sh_ref, sc_ref, lw_ref, lb_ref,
                    wr_ref, rb_ref, x1_ref, h2_ref, dest_ref, destc_ref, gatec_ref, cnt_ref):
    y = _dot(yh_ref[0], w_ref[:HG_WIDTH, :]) + _dot(ya_ref[0], w_ref[HG_WIDTH:, :])
    x1 = _ln_rows(ALPHA * x_ref[0] + g1_ref[0] * y, LN_EPS) * lw_ref[...] + lb_ref[...]
    x1_ref[0] = x1
    hf = _ln_rows(x1, NORM_EPS) * (1.0 + sc_ref[0]) + sh_ref[0]
    h2_ref[0] = hf.astype(BF16)
    gates_t, chosen = _route(hf, wr_ref[...], rb_ref[...])
    for s in range(hf.shape[0] // SORT_TILE):
        sl = slice(s * SORT_TILE, (s + 1) * SORT_TILE)
        dest_ref[s], destc_ref[s], gatec_ref[s], cnt_ref[s] = _sort_rows(chosen[:, sl], gates_t[:, sl])


def _outproj(x, yh, ya, w, g1, sh2, sc2, lw, lb, wr_t, rbias, tm):
    b, n, d = x.shape
    row = lambda bi, i: (bi, i, 0)
    per_b = lambda bi, i: (bi, 0, 0)
    const = lambda bi, i: (0, 0)
    nt = b * n // SORT_TILE
    per_step = tm // SORT_TILE
    tiles = lambda bi, i: (bi * (n // tm) + i, 0, 0)
    return pl.pallas_call(
        _outproj_kernel,
        grid=(b, n // tm),
        in_specs=[pl.BlockSpec((1, tm, d), row),
                  pl.BlockSpec((1, tm, HG_WIDTH), row), pl.BlockSpec((1, tm, ATT_WIDTH), row),
                  pl.BlockSpec(w.shape, const),
                  pl.BlockSpec((1, 1, d), per_b), pl.BlockSpec((1, 1, d), per_b),
                  pl.BlockSpec((1, 1, d), per_b),
                  pl.BlockSpec((1, d), const), pl.BlockSpec((1, d), const),
                  pl.BlockSpec(wr_t.shape, const), pl.BlockSpec(rbias.shape, const)],
        out_specs=[pl.BlockSpec((1, tm, d), row), pl.BlockSpec((1, tm, d), row),
                   pl.BlockSpec((per_step, TOP_K, SORT_TILE), tiles),
                   pl.BlockSpec((per_step, SORT_TILE, LANES), tiles),
                   pl.BlockSpec((per_step, SORT_TILE, LANES), tiles),
                   pl.BlockSpec((per_step, 8, LANES), tiles)],
        out_shape=[jax.ShapeDtypeStruct((b, n, d), F32), jax.ShapeDtypeStruct((b, n, d), BF16),
                   jax.ShapeDtypeStruct((nt, TOP_K, SORT_TILE), jnp.int32),
                   jax.ShapeDtypeStruct((nt, SORT_TILE, LANES), jnp.int32),
                   jax.ShapeDtypeStruct((nt, SORT_TILE, LANES), F32),
                   jax.ShapeDtypeStruct((nt, 8, LANES), F32)],
        compiler_params=_params("arbitrary", "arbitrary"),
        name="outproj_ln_router",
    )(x, yh, ya, w, g1, sh2, sc2, lw, lb, wr_t, rbias)


def _moe_layout(counts, max_tiles):
    n_pad = (counts + (SORT_UNIT - 1)) // SORT_UNIT * SORT_UNIT
    total = jnp.sum(n_pad, axis=0)
    region = (total + (EXPERT_TILE - 1)) // EXPERT_TILE * EXPERT_TILE
    base = jnp.cumsum(region) - region
    chunk_row = base[None, :] + jnp.cumsum(n_pad, axis=0) - n_pad
    tile_end = jnp.cumsum(region // EXPERT_TILE)
    n_used = tile_end[-1]
    j = jnp.minimum(jnp.arange(max_tiles, dtype=jnp.int32), n_used - 1)
    tile_expert = jnp.sum((tile_end[None, :] <= j[:, None]).astype(jnp.int32), axis=1)
    units = n_pad // SORT_UNIT
    local_row = jnp.cumsum(n_pad, axis=1) - n_pad
    rare = jnp.ones(units.shape, jnp.bool_)
    counts_out, lists = [], []
    for size, width in PIECE_LISTS[:-1]:
        mask = units == size
        rare = rare & ~mask
        slot = jnp.cumsum(mask, axis=1) - 1
        pick = mask[:, None, :] & (slot[:, None, :] == jnp.arange(width)[None, :, None])
        counts_out.append(jnp.sum(mask, axis=1))
        lists += [jnp.sum(jnp.where(pick, v[:, None, :], 0), axis=2) for v in (local_row, chunk_row)]
    single = jnp.where(rare, units, 0)
    single_end = jnp.cumsum(single, axis=1)
    u = jnp.arange(LOCAL_UNITS)[None, :, None]
    first = (single_end - single)[:, None, :]
    pick = (u >= first) & (u < single_end[:, None, :])
    counts_out.append(single_end[:, -1])
    lists += [jnp.sum(jnp.where(pick, v[:, None, :] + (u - first) * SORT_UNIT, 0), axis=2)
              for v in (local_row, chunk_row)]
    header = jnp.stack(counts_out + [jnp.sum(n_pad, axis=1)], axis=1)
    header = jnp.pad(header, ((0, 0), (0, TABLE_HEADER - header.shape[1])))
    table = jnp.concatenate([header] + lists, axis=1)
    table = jnp.pad(table, ((0, 0), (0, TABLE_WIDTH - table.shape[1]))).astype(jnp.int32)[:, None, :]
    tails = jnp.concatenate([base + total, (region - total) // SORT_UNIT]).astype(jnp.int32)
    return table, tails, tile_expert, n_used.astype(jnp.int32)[None]


def _row_cases(rows_used, fn):
    @pl.when(rows_used <= USUAL_ROWS)
    def _():
        fn(USUAL_ROWS)

    @pl.when(rows_used > USUAL_ROWS)
    def _():
        fn(LOCAL_ROWS)


def _unit(ref, row, units=1):
    return ref.at[pl.ds(pl.multiple_of(row, SORT_UNIT), units * SORT_UNIT), :]


def _for_each_piece(table_ref, fn):
    offset = TABLE_HEADER
    for c, (size, width) in enumerate(PIECE_LISTS):
        def body(j, carry, offset=offset, size=size, width=width, priority=c % 2):
            fn(table_ref[0, 0, offset + j], table_ref[0, 0, offset + width + j], size, priority)
            return carry

        lax.fori_loop(0, table_ref[0, 0, c], body, 0)
        offset += 2 * width


def _await_units(units, wait_fn):
    def many(u, carry):
        wait_fn(WAIT_UNITS)
        return carry

    def single(u, carry):
        wait_fn(1)
        return carry

    lax.fori_loop(0, units // WAIT_UNITS, many, 0)
    lax.fori_loop(0, units % WAIT_UNITS, single, 0)


def _dispatch_kernel(chunks_ref, tails_ref, h_ref, dest_ref, xs_ref, buf_ref, zero_ref, pending_ref, sem):
    i = pl.program_id(0)
    slot = i % 2
    buf = buf_ref.at[slot]

    def drain(s):
        _await_units(pending_ref[s], lambda n: pltpu.make_async_copy(
            _unit(buf_ref.at[s], 0, n), _unit(xs_ref, 0, n), sem.at[s]).wait())

    @pl.when(i >= 2)
    def _():
        drain(slot)

    def permute(rows):
        r = lax.broadcasted_iota(jnp.int32, (rows, SORT_TILE), 0).astype(jnp.int16)
        p = jnp.zeros((rows, SORT_TILE), BF16)
        one = jnp.ones((rows, SORT_TILE), BF16)
        for k in range(TOP_K):
            hit = r == dest_ref[0, k:k + 1, :].astype(jnp.int16)
            p = jnp.where(hit, one, p)
        buf[0:rows, :] = _dot(p, h_ref[...]).astype(BF16)

    rows_used = chunks_ref[0, 0, TABLE_ROWS_USED]
    _row_cases(rows_used, permute)
    _for_each_piece(chunks_ref, lambda loc, glob, n, prio: pltpu.make_async_copy(
        _unit(buf, loc, n), _unit(xs_ref, glob, n), sem.at[slot]).start(priority=prio))
    pending_ref[slot] = rows_used // SORT_UNIT

    @pl.when(i == pl.num_programs(0) - 1)
    def _():
        zero_ref[...] = jnp.zeros_like(zero_ref)

        def per_expert(e, total):
            n = tails_ref[N_EXPERTS + e]
            start = tails_ref[e]

            def many(u, carry):
                pltpu.make_async_copy(zero_ref, _unit(xs_ref, start + u * (WAIT_UNITS * SORT_UNIT), WAIT_UNITS),
                                      sem.at[2]).start()
                return carry

            def single(u, carry):
                row = start + (n // WAIT_UNITS * WAIT_UNITS + u) * SORT_UNIT
                pltpu.make_async_copy(_unit(zero_ref, 0), _unit(xs_ref, row), sem.at[2]).start()
                return carry

            lax.fori_loop(0, n // WAIT_UNITS, many, 0)
            lax.fori_loop(0, n % WAIT_UNITS, single, 0)
            return total + n

        total = lax.fori_loop(0, N_EXPERTS, per_expert, 0)
        _await_units(total, lambda n: pltpu.make_async_copy(
            _unit(zero_ref, 0, n), _unit(xs_ref, 0, n), sem.at[2]).wait())

        @pl.when(i >= 1)
        def _():
            drain(1 - slot)

        drain(slot)


def _dispatch(chunks, tails, h2, dest, max_rows):
    t, d = h2.shape
    nt = t // SORT_TILE
    return pl.pallas_call(
        _dispatch_kernel,
        grid=(nt,),
        in_specs=[pl.BlockSpec((1, 1, TABLE_WIDTH), lambda i: (i, 0, 0), memory_space=pltpu.SMEM),
                  pl.BlockSpec(memory_space=pltpu.SMEM),
                  pl.BlockSpec((SORT_TILE, d), lambda i: (i, 0)),
                  pl.BlockSpec((1, TOP_K, SORT_TILE), lambda i: (i, 0, 0))],
        out_specs=pl.BlockSpec(memory_space=pl.ANY),
        out_shape=jax.ShapeDtypeStruct((max_rows, d), BF16),
        scratch_shapes=[pltpu.VMEM((2, LOCAL_ROWS, d), BF16), pltpu.VMEM((WAIT_UNITS * SORT_UNIT, d), BF16),
                        pltpu.SMEM((2,), jnp.int32), pltpu.SemaphoreType.DMA((3,))],
        compiler_params=_params("arbitrary"),
        name="moe_dispatch",
    )(chunks, tails, h2, dest)


def _expert_kernel(te_ref, nu_ref, xs_ref, wg_ref, wu_ref, wd_ref, ys_ref):
    @pl.when(pl.program_id(0) < nu_ref[0])
    def _():
        x = xs_ref[...]
        a = _silu(_dot(x, wg_ref[0].astype(BF16))) * _dot(x, wu_ref[0].astype(BF16))
        ys_ref[...] = _dot(a.astype(BF16), wd_ref[0].astype(BF16)).astype(BF16)


def _experts(tile_expert, n_used, xs, wg, wu, wd):
    rows, d = xs.shape
    ff = wg.shape[2]
    row = lambda j, te, nu: (jnp.minimum(j, jnp.maximum(nu[0] - 1, 0)), 0)
    exp = lambda j, te, nu: (te[j], 0, 0)
    return pl.pallas_call(
        _expert_kernel,
        grid_spec=pltpu.PrefetchScalarGridSpec(
            num_scalar_prefetch=2,
            grid=(rows // EXPERT_TILE,),
            in_specs=[pl.BlockSpec((EXPERT_TILE, d), row),
                      pl.BlockSpec((1, d, ff), exp), pl.BlockSpec((1, d, ff), exp),
                      pl.BlockSpec((1, ff, d), exp)],
            out_specs=pl.BlockSpec((EXPERT_TILE, d), row)),
        out_shape=jax.ShapeDtypeStruct((rows, d), BF16),
        compiler_params=_params("arbitrary"),
        name="moe_experts",
    )(tile_expert, n_used, xs, wg, wu, wd)


def _combine_kernel(chunks_ref, next_ref, ys_ref, dest_ref, gate_ref, h_ref, x1_ref, g2_ref, lw_ref, lb_ref,
                    swg_ref, swu_ref, swd_ref, o_ref, buf_ref, sem):
    i = pl.program_id(0)
    slot = i % 2

    def fetch(meta_ref, s):
        _for_each_piece(meta_ref, lambda loc, glob, n, prio: pltpu.make_async_copy(
            _unit(ys_ref, glob, n), _unit(buf_ref.at[s], loc, n), sem.at[s]).start(priority=prio))

    @pl.when(i == 0)
    def _():
        buf_ref[...] = jnp.zeros_like(buf_ref)
        fetch(chunks_ref, 0)

    @pl.when(i + 1 < pl.num_programs(0))
    def _():
        fetch(next_ref, 1 - slot)

    rows_used = chunks_ref[0, 0, TABLE_ROWS_USED]
    _await_units(rows_used // SORT_UNIT, lambda n: pltpu.make_async_copy(
        _unit(ys_ref, 0, n), _unit(buf_ref.at[slot], 0, n), sem.at[slot]).wait())

    def unpermute(rows):
        h = h_ref[...]
        a = _silu(_dot(h, swg_ref[...])) * _dot(h, swu_ref[...])
        lane = lax.broadcasted_iota(jnp.int32, (SORT_TILE, rows), 1).astype(jnp.int16)
        p = jnp.zeros((SORT_TILE, rows), BF16)
        for k in range(TOP_K):
            hit = lane == dest_ref[0, :, k:k + 1].astype(jnp.int16)
            p = jnp.where(hit, jnp.broadcast_to(gate_ref[0, :, k:k + 1].astype(BF16), p.shape), p)
        ffn = _dot(a.astype(BF16), swd_ref[...]) + _dot(p, buf_ref[slot, 0:rows, :])
        u = ALPHA * x1_ref[...] + g2_ref[0] * ffn
        o_ref[...] = _ln_rows(u, LN_EPS) * lw_ref[...] + lb_ref[...]

    _row_cases(rows_used, unpermute)


def _combine(chunks, ys, dest, gate, h2, x1, g2, lw, lb, swg, swu, swd, tiles_per_batch):
    t, d = h2.shape
    ff = swg.shape[1]
    nt = t // SORT_TILE
    tile = lambda i: (i, 0, 0)
    nxt = lambda i: (jnp.minimum(i + 1, nt - 1), 0, 0)
    row = lambda i: (i, 0)
    const = lambda i: (0, 0)
    return pl.pallas_call(
        _combine_kernel,
        grid=(nt,),
        in_specs=[pl.BlockSpec((1, 1, TABLE_WIDTH), tile, memory_space=pltpu.SMEM),
                  pl.BlockSpec((1, 1, TABLE_WIDTH), nxt, memory_space=pltpu.SMEM),
                  pl.BlockSpec(memory_space=pl.ANY),
                  pl.BlockSpec((1, SORT_TILE, LANES), tile), pl.BlockSpec((1, SORT_TILE, LANES), tile),
                  pl.BlockSpec((SORT_TILE, d), row), pl.BlockSpec((SORT_TILE, d), row),
                  pl.BlockSpec((1, 1, d), lambda i: (i // tiles_per_batch, 0, 0)),
                  pl.BlockSpec((1, d), const), pl.BlockSpec((1, d), const),
                  pl.BlockSpec((d, ff), const), pl.BlockSpec((d, ff), const), pl.BlockSpec((ff, d), const)],
        out_specs=pl.BlockSpec((SORT_TILE, d), row),
        out_shape=jax.ShapeDtypeStruct((t, d), F32),
        scratch_shapes=[pltpu.VMEM((2, LOCAL_ROWS, d), BF16), pltpu.SemaphoreType.DMA((2,))],
        compiler_params=_params("arbitrary"),
        name="moe_combine",
    )(chunks, chunks, ys, dest, gate, h2, x1, g2, lw, lb, swg, swu, swd)


def _moe(h2, x1, dest, dest_cols, gate_cols, counts, g2, lw, lb, wg, wu, wd, swg, swu, swd):
    b, n, d = x1.shape
    t = b * n
    nt = t // SORT_TILE
    max_rows = t * TOP_K + nt * N_EXPERTS * (SORT_UNIT - 1) + N_EXPERTS * (EXPERT_TILE - SORT_UNIT)
    max_rows = -(-max_rows // EXPERT_TILE) * EXPERT_TILE
    chunks, tails, tile_expert, n_used = _moe_layout(
        counts[:, 0, :N_EXPERTS].astype(jnp.int32), max_rows // EXPERT_TILE)
    h2 = h2.reshape(t, d)
    xs = _dispatch(chunks, tails, h2, dest, max_rows)
    ys = _experts(tile_expert, n_used, xs, wg, wu, wd)
    out = _combine(chunks, ys, dest_cols, gate_cols, h2, x1.reshape(t, d), g2, lw, lb, swg, swu, swd,
                   n // SORT_TILE)
    return out.reshape(b, n, d)


def _swap_kv_heads(w):
    return jnp.concatenate([w[:, HEAD_DIM:], w[:, :HEAD_DIM]], axis=1)


def _split_w_in(w_in):
    bounds = np.cumsum([HG_WIDTH] * 5 + [ATT_WIDTH, KV_WIDTH])
    return jnp.split(w_in, [int(v) for v in bounds], axis=1)


def _context_weight(w_in):
    _, zff, zfb, zi, _, _, ak, av = _split_w_in(w_in)
    return jnp.concatenate([zff, zfb, zi, ak, _swap_kv_heads(ak), av, _swap_kv_heads(av)],
                           axis=1).astype(BF16)


def _rope_tables(n):
    pos = np.arange(n)
    freqs = np.float32(ROPE_BASE) ** (-np.arange(ROT_PAIRS, dtype=np.float32) / np.float32(ROT_PAIRS))
    ang_row = (pos // GRID_W).astype(np.float32)[:, None] * freqs
    ang_col = (pos % GRID_W).astype(np.float32)[:, None] * freqs
    cos = np.concatenate([np.cos(ang_row)] * 2 + [np.cos(ang_col)] * 2, axis=1)
    sin = np.concatenate([-np.sin(ang_row), np.sin(ang_row), -np.sin(ang_col), np.sin(ang_col)], axis=1)
    reps = LANES // HEAD_DIM
    return (jnp.asarray(np.tile(cos, (1, reps)), dtype=F32), jnp.asarray(np.tile(sin, (1, reps)), dtype=F32))


def kernel(x, c, ctx, c_ctx, w_ada, b_ada, w_in, hg_lb_fwd, hg_lb_bwd, hg_norm_w, attn_sink, w_out, ln1_w, ln1_b, router_w, router_bias, exp_w_gate, exp_w_up, exp_w_down, shared_w_gate, shared_w_up, shared_w_down, ln2_w, ln2_b):
    b, n, d = x.shape
    layer = 0
    rows = -(-(b + 1) // 8) * 8
    cc = jnp.zeros((rows, d), F32).at[:b].set(c).at[b].set(c_ctx)
    mod = _ada(cc, w_ada[layer], b_ada[layer][None, :])
    sh1, sc1, g1, sh2, sc2, g2 = [m[:, None, :] for m in jnp.split(mod[:b], 6, axis=1)]
    csh1, csc1 = mod[b:b + 1, :d], mod[b:b + 1, d:2 * d]

    lb_f = jnp.cumsum(jax.nn.softmax(hg_lb_fwd.astype(F32), axis=0), axis=0)[layer][None, :]
    lb_b = jnp.cumsum(jax.nn.softmax(hg_lb_bwd.astype(F32), axis=0), axis=0)[layer][None, :]
    cos, sin = _rope_tables(n)

    xk, xks, xv, xvs, s0f, s0b = _ctx(ctx, csh1, csc1, _context_weight(w_in[layer]), lb_f, lb_b)
    tm = min(n, 512)
    (q, kf, gf, kb, gb, v, zg, aq, ak, aks, av, avs) = _inproj(
        x, sh1, sc1, w_in[layer].astype(BF16), lb_f, lb_b, cos, sin, min(n, 2 * INPROJ_PART))
    yh = _hgrn(q, kf, gf, kb, gb, v, zg, s0f, s0b, hg_norm_w[layer][None, :])
    ya = _attn(attn_sink[layer], aq, ak, aks, av, avs, xk, xks, xv, xvs)

    x1, h2, dest, dest_cols, gate_cols, counts = _outproj(
        x, yh, ya, w_out[layer].astype(BF16), g1, sh2, sc2, ln1_w[layer][None, :], ln1_b[layer][None, :],
        router_w[layer].T, router_bias[layer][:, None], tm)
    return _moe(h2, x1, dest, dest_cols, gate_cols, counts, g2, ln2_w[layer][None, :], ln2_b[layer][None, :],
                exp_w_gate[layer], exp_w_up[layer], exp_w_down[layer], shared_w_gate[layer].astype(BF16),
                shared_w_up[layer].astype(BF16), shared_w_down[layer].astype(BF16))
```
